```python
import math
import jax, jax.numpy as jnp
from jax import lax
import numpy as np

D_MODEL = 1024
BATCH = 8
SEQ = 4096
DEPTH = 1
DEC_BATCH = 32
DEC_SEQ = 64
PAST_LEN = 1024

CHUNK = 64
Q_BLOCK = 128
MIX_WIDTH = D_MODEL
N_DIFF_HEADS = 4
DIFF_HEAD_DIM = 64
DIFF_WIDTH = N_DIFF_HEADS * 2 * DIFF_HEAD_DIM
N_DSA_HEADS = 8
N_DSA_KV_HEADS = 2
DSA_HEAD_DIM = 64
DSA_WIDTH = N_DSA_HEADS * DSA_HEAD_DIM
N_IDX_HEADS = 4
IDX_HEAD_DIM = 64
DSA_TOPK = 256
N_ATTN_HEADS = N_DIFF_HEADS + N_DSA_HEADS
NUM_BUCKETS = 32
MAX_DISTANCE = 256
D_FF = -(-8 * D_MODEL // (3 * 256)) * 256
IN_SIZES = (DIFF_WIDTH, DIFF_WIDTH, DIFF_WIDTH,
            DSA_WIDTH, N_DSA_KV_HEADS * DSA_HEAD_DIM, N_DSA_KV_HEADS * DSA_HEAD_DIM,
            N_IDX_HEADS * IDX_HEAD_DIM, IDX_HEAD_DIM, N_IDX_HEADS)
IN_WIDTH = sum(IN_SIZES)
RMS_EPS = 1e-6
NEG_INF = -1e30

kernel_name = "hymba_diffattn_dsa_streaming_step"


def rms_norm(x, g):
    xf = x.astype(jnp.float32)
    y = xf * lax.rsqrt(jnp.mean(xf * xf, axis=-1, keepdims=True) + RMS_EPS)
    return (y * g.astype(jnp.float32)).astype(x.dtype)


def rel_bucket(rel):
    nb = NUM_BUCKETS // 2
    max_exact = nb // 2
    ret = jnp.where(rel > 0, nb, 0)
    n = jnp.abs(rel)
    nf = jnp.maximum(n, 1).astype(jnp.float32)
    large = max_exact + (jnp.log(nf / max_exact) / math.log(MAX_DISTANCE / max_exact)
                         * (nb - max_exact)).astype(jnp.int32)
    large = jnp.minimum(large, nb - 1)
    return ret + jnp.where(n < max_exact, n, large)


def chunk_visible(qpos, kpos):
    return (kpos[None, :] // CHUNK) <= (qpos[:, None] // CHUNK)


def gather_rows(a, idx):
    return jax.vmap(lambda ab, ib: ab[ib])(a, idx)


def in_projection(h, w_in):
    B, T, _ = h.shape
    z = h @ w_in
    cuts, acc = [], 0
    for s in IN_SIZES[:-1]:
        acc += s
        cuts.append(acc)
    dq, dk, dv, sq, sk, sv, iq, ik, iw = jnp.split(z, cuts, axis=-1)
    hd = (B, T, N_DIFF_HEADS, 2 * DIFF_HEAD_DIM)
    return (dq.reshape(hd), dk.reshape(hd), dv.reshape(hd),
            sq.reshape(B, T, N_DSA_HEADS, DSA_HEAD_DIM),
            sk.reshape(B, T, N_DSA_KV_HEADS, DSA_HEAD_DIM),
            sv.reshape(B, T, N_DSA_KV_HEADS, DSA_HEAD_DIM),
            iq.reshape(B, T, N_IDX_HEADS, IDX_HEAD_DIM), ik, iw)


def diff_attn(q, k, v, qpos, kpos, lam, lam_init, g_subln, tab):
    f32 = jnp.float32
    dh = DIFF_HEAD_DIM
    qf = q.astype(f32) * (dh ** -0.5)
    kf = k.astype(f32)
    vis = chunk_visible(qpos, kpos)
    bias = jnp.transpose(tab.astype(f32)[rel_bucket(kpos[None, :] - qpos[:, None])], (2, 0, 1))

    def attn_map(qa, ka):
        s = jnp.einsum('bqhd,bkhd->bhqk', qa, ka) + bias[None]
        return jax.nn.softmax(jnp.where(vis, s, NEG_INF), axis=-1)

    a = attn_map(qf[..., :dh], kf[..., :dh]) - lam * attn_map(qf[..., dh:], kf[..., dh:])
    o = jnp.einsum('bhqk,bkhe->bqhe', a, v.astype(f32))
    o = rms_norm(o, g_subln) * (1.0 - lam_init)
    B, Tq = o.shape[:2]
    return o.reshape(B, Tq, DIFF_WIDTH)


def dsa_attn(q, k, v, qi, ki, wi, qpos, kpos, topk, tab):
    f32 = jnp.float32
    B, Tq = q.shape[:2]
    G = N_DSA_HEADS // N_DSA_KV_HEADS
    vis = chunk_visible(qpos, kpos)
    dots = jnp.einsum('bqhd,bkd->bqhk', qi.astype(f32), ki.astype(f32)) * (IDX_HEAD_DIM ** -0.5)
    score = jnp.einsum('bqh,bqhk->bqk', wi.astype(f32) * (N_IDX_HEADS ** -0.5), jax.nn.relu(dots))
    score = jnp.where(vis[None], score, NEG_INF)
    _, sel = lax.top_k(score, topk)
    ksel_pos = kpos[sel]
    ok = (ksel_pos // CHUNK) <= (qpos[None, :, None] // CHUNK)
    kg = gather_rows(k, sel).astype(f32)
    vg = gather_rows(v, sel).astype(f32)
    qg = q.astype(f32).reshape(B, Tq, N_DSA_KV_HEADS, G, DSA_HEAD_DIM) * (DSA_HEAD_DIM ** -0.5)
    s = jnp.einsum('bqgnd,bqkgd->bqgnk', qg, kg)
    bias = tab.astype(f32)[rel_bucket(ksel_pos - qpos[None, :, None])]
    bias = bias.reshape(B, Tq, topk, N_DSA_KV_HEADS, G).transpose(0, 1, 3, 4, 2)
    s = jnp.where(ok[:, :, None, None, :], s + bias, NEG_INF)
    p = jax.nn.softmax(s, axis=-1)
    o = jnp.einsum('bqgnk,bqkgd->bqgnd', p, vg)
    return o.reshape(B, Tq, DSA_WIDTH)


def token_mix(h, past, w_in, w_out, lam, lam_init, g_subln, tab):
    B, T, _ = h.shape
    dq, dk, dv, sq, sk, sv, iq, ik, iw = in_projection(h, w_in)
    new_rows = (dk, dv, sk, sv, ik)
    if past is None:
        P = 0
        keys = new_rows
    else:
        P = past[0].shape[1]
        keys = tuple(jnp.concatenate([p.astype(h.dtype), n], axis=1) for p, n in zip(past, new_rows))
    ck_d, cv_d, ck_s, cv_s, cki = keys
    L = P + T
    qpos = P + jnp.arange(T, dtype=jnp.int32)
    kpos = jnp.arange(L, dtype=jnp.int32)
    topk = min(DSA_TOPK, L // 4)
    tab_d, tab_s = tab[:, :N_DIFF_HEADS], tab[:, N_DIFF_HEADS:]

    def mix_block(args):
        q_d, q_s, q_i, w_i, qp = args
        od = diff_attn(q_d, ck_d, cv_d, qp, kpos, lam, lam_init, g_subln, tab_d)
        os_ = dsa_attn(q_s, ck_s, cv_s, q_i, cki, w_i, qp, kpos, topk, tab_s)
        return jnp.concatenate([od, os_], axis=-1)

    q_args = (dq, sq, iq, iw)
    if T > Q_BLOCK and T % Q_BLOCK == 0:
        nb = T // Q_BLOCK
        blk = lambda a: a.reshape((B, nb, Q_BLOCK) + a.shape[2:]).swapaxes(0, 1)
        o = lax.map(mix_block, tuple(blk(a) for a in q_args) + (qpos.reshape(nb, Q_BLOCK),))
        o = o.swapaxes(0, 1).reshape(B, T, MIX_WIDTH)
    else:
        o = mix_block(q_args + (qpos,))
    return o.astype(h.dtype) @ w_out, new_rows


def swiglu(h, w_gate_up, w_down):
    g, u = jnp.split(h @ w_gate_up, 2, axis=-1)
    return (jax.nn.silu(g) * u) @ w_down


def layer(x, c, past, l, lp, tab):
    (w_ada, b_ada, g_pre_mix, g_post_mix, g_pre_ffn, g_post_ffn, w_in,
     lq1, lk1, lq2, lk2, g_subln, w_out, w_gate_up, w_down) = lp
    f32 = jnp.float32
    mod = (jax.nn.silu(c.astype(f32)) @ w_ada.astype(f32) + b_ada.astype(f32)).astype(x.dtype)
    sh1, sc1, ga1, sh2, sc2, ga2 = [m[:, None, :] for m in jnp.split(mod, 6, axis=-1)]
    lam_init = 0.8 - 0.6 * math.exp(-0.3 * l)
    lam = (jnp.exp(jnp.sum(lq1.astype(f32) * lk1.astype(f32)))
           - jnp.exp(jnp.sum(lq2.astype(f32) * lk2.astype(f32))) + lam_init)
    h = rms_norm(x, g_pre_mix) * (1 + sc1) + sh1
    m, new_rows = token_mix(h, past, w_in, w_out, lam, lam_init, g_subln, tab)
    x = x + ga1 * rms_norm(m, g_post_mix)
    h = rms_norm(x, g_pre_ffn) * (1 + sc2) + sh2
    x = x + ga2 * rms_norm(swiglu(h, w_gate_up, w_down), g_post_ffn)
    return x, new_rows


def setup_inputs(seed: int = 0) -> dict:
    key = jax.random.key(seed)
    ks = jax.random.split(key, 32)
    nrm = lambda k, s: jax.random.normal(k, s, dtype=jnp.float32)
    D = D_MODEL
    return {
        "x_prompt": nrm(ks[0], (BATCH, SEQ, D)),
        "x_sample": nrm(ks[1], (DEC_BATCH, DEC_SEQ, D)),
        "cache_diff_k": nrm(ks[2], (DEPTH, DEC_BATCH, PAST_LEN, N_DIFF_HEADS, 2 * DIFF_HEAD_DIM)),
        "cache_diff_v": nrm(ks[3], (DEPTH, DEC_BATCH, PAST_LEN, N_DIFF_HEADS, 2 * DIFF_HEAD_DIM)),
        "cache_dsa_k": nrm(ks[4], (DEPTH, DEC_BATCH, PAST_LEN, N_DSA_KV_HEADS, DSA_HEAD_DIM)),
        "cache_dsa_v": nrm(ks[5], (DEPTH, DEC_BATCH, PAST_LEN, N_DSA_KV_HEADS, DSA_HEAD_DIM)),
        "cache_dsa_kidx": nrm(ks[6], (DEPTH, DEC_BATCH, PAST_LEN, IDX_HEAD_DIM)),
        "c_prompt": nrm(ks[7], (BATCH, D)),
        "c_sample": nrm(ks[8], (DEC_BATCH, D)),
        "w_ada": nrm(ks[9], (DEPTH, D, 6 * D)) * (0.5 * D ** -0.5),
        "b_ada": nrm(ks[10], (DEPTH, 6 * D)) * 0.01,
        "g_pre_mix": 1.0 + 0.05 * nrm(ks[11], (DEPTH, D)),
        "g_post_mix": 1.0 + 0.05 * nrm(ks[12], (DEPTH, D)),
        "g_pre_ffn": 1.0 + 0.05 * nrm(ks[13], (DEPTH, D)),
        "g_post_ffn": 1.0 + 0.05 * nrm(ks[14], (DEPTH, D)),
        "w_in": nrm(ks[15], (DEPTH, D, IN_WIDTH)) * D ** -0.5,
        "lambda_q1": 0.1 * nrm(ks[16], (DEPTH, DIFF_HEAD_DIM)),
        "lambda_k1": 0.1 * nrm(ks[17], (DEPTH, DIFF_HEAD_DIM)),
        "lambda_q2": 0.1 * nrm(ks[18], (DEPTH, DIFF_HEAD_DIM)),
        "lambda_k2": 0.1 * nrm(ks[19], (DEPTH, DIFF_HEAD_DIM)),
        "g_subln": 1.0 + 0.05 * nrm(ks[20], (DEPTH, 2 * DIFF_HEAD_DIM)),
        "w_out": nrm(ks[21], (DEPTH, MIX_WIDTH, D)) * MIX_WIDTH ** -0.5,
        "w_gate_up": nrm(ks[22], (DEPTH, D, 2 * D_FF)) * D ** -0.5,
        "w_down": nrm(ks[23], (DEPTH, D_FF, D)) * D_FF ** -0.5,
        "rel_bias_table": 0.5 * nrm(ks[24], (NUM_BUCKETS, N_ATTN_HEADS)),
    }


def reference(x_prompt, x_sample, cache_diff_k, cache_diff_v, cache_dsa_k, cache_dsa_v, cache_dsa_kidx,
              c_prompt, c_sample, w_ada, b_ada, g_pre_mix, g_post_mix, g_pre_ffn, g_post_ffn, w_in,
              lambda_q1, lambda_k1, lambda_q2, lambda_k2, g_subln, w_out, w_gate_up, w_down,
              rel_bias_table):
    yp, ys = x_prompt, x_sample
    rows_p, rows_s = [], []
    for l in range(DEPTH):
        lp = (w_ada[l], b_ada[l], g_pre_mix[l], g_post_mix[l], g_pre_ffn[l], g_post_ffn[l], w_in[l],
              lambda_q1[l], lambda_k1[l], lambda_q2[l], lambda_k2[l], g_subln[l], w_out[l],
              w_gate_up[l], w_down[l])
        yp, rp = layer(yp, c_prompt, None, l, lp, rel_bias_table)
        past = (cache_diff_k[l], cache_diff_v[l], cache_dsa_k[l], cache_dsa_v[l], cache_dsa_kidx[l])
        ys, rs = layer(ys, c_sample, past, l, lp, rel_bias_table)
        rows_p.append(rp)
        rows_s.append(rs)
    stack = lambda rows, i: jnp.stack([r[i] for r in rows], axis=0)
    return (yp, ys,
            stack(rows_p, 0), stack(rows_p, 1), stack(rows_p, 2), stack(rows_p, 3), stack(rows_p, 4),
            stack(rows_s, 0), stack(rows_s, 1), stack(rows_s, 2), stack(rows_s, 3), stack(rows_s, 4))
```

```python
import functools
import math

import jax
import jax.numpy as jnp
from jax import lax
from jax.experimental import pallas as pl
from jax.experimental.pallas import tpu as pltpu

F32 = jnp.float32
BF16 = jnp.bfloat16
I32 = jnp.int32

D_MODEL = 1024
CHUNK = 64
N_DIFF = 4
DIFF_DH = 64
DIFF_W = N_DIFF * 2 * DIFF_DH
N_DSA = 8
N_DSA_KV = 2
DSA_DH = 64
DSA_W = N_DSA * DSA_DH
DSA_KV_W = N_DSA_KV * DSA_DH
N_IDX = 4
IDX_DH = 64
TOPK = 256
N_HEADS = N_DIFF + N_DSA
NUM_BUCKETS = 32
D_FF = 2816
EPS = 1e-6
NEG = -1e30
LAM_INIT = 0.8 - 0.6 * math.exp(-0.3 * 0)

LANE = 128
KEY_BLOCK = 512
VMEM_LIMIT = 56 * 1024 * 1024
INT_MIN = -2 ** 31
FAR_BUCKET = 15
BUCKET_STEPS = (13, 20, 30, 46, 70, 108, 166)
N_BIAS_TILES = 4


def _rms(x):
    return x * lax.rsqrt(jnp.mean(x * x, axis=-1, keepdims=True) + EPS)


def _dot(a, b):
    return jnp.dot(a, b, preferred_element_type=F32)


def _dot_t(a, b):
    return lax.dot_general(a, b, (((1,), (1,)), ((), ())), preferred_element_type=F32)


def _mod_kernel(c_ref, w_ref, b_ref, o_ref):
    c = c_ref[...]
    s = c * jax.nn.sigmoid(c)
    o_ref[...] = jnp.dot(s, w_ref[...], preferred_element_type=F32,
                         precision=lax.Precision.HIGHEST) + b_ref[...]


def _modulation(c_all, w_ada, b_ada):
    n = c_all.shape[0]
    tn = 1024
    return pl.pallas_call(
        _mod_kernel,
        grid=(6 * D_MODEL // tn,),
        in_specs=[pl.BlockSpec((n, D_MODEL), lambda j: (0, 0)),
                  pl.BlockSpec((D_MODEL, tn), lambda j: (0, j)),
                  pl.BlockSpec((1, tn), lambda j: (0, j))],
        out_specs=pl.BlockSpec((n, tn), lambda j: (0, j)),
        out_shape=jax.ShapeDtypeStruct((n, 6 * D_MODEL), F32),
        compiler_params=pltpu.CompilerParams(dimension_semantics=("arbitrary",),
                                             vmem_limit_bytes=VMEM_LIMIT),
    )(c_all, w_ada, b_ada.reshape(1, -1))


_C_DQ = 0
_C_DK = _C_DQ + 2 * DIFF_W
_C_DV = _C_DK + DIFF_W
_C_SQ = _C_DV + DIFF_W
_C_SK = _C_SQ + N_DSA * LANE
_C_SV = _C_SK + DSA_KV_W
_C_END = _C_SV + DSA_KV_W
IDX_W = 384


def _proj_kernel(x_ref, sc_ref, sh_ref, g_ref, wm_ref, wih_ref, wil_ref,
                 dk32, dv32, sk32, sv32, zi32, dq16, dk16, dv16, sq16, sk16, sv16):
    bt, tt, _ = x_ref.shape
    x = x_ref[...]
    h = (_rms(x) * g_ref[...]) * (1.0 + sc_ref[...]) + sh_ref[...]
    h = h.reshape(bt * tt, D_MODEL)
    hb = h.astype(BF16)
    hl = (h - hb.astype(F32)).astype(BF16)
    z = _dot(hb, wm_ref[...])
    wih = wih_ref[...]
    zi = _dot(hb, wih) + _dot(hb, wil_ref[...]) + _dot(hl, wih)

    def put(ref, v):
        ref[...] = v.reshape(bt, tt, v.shape[-1]).astype(ref.dtype)

    zdk = z[:, _C_DK:_C_DV]
    zdv = z[:, _C_DV:_C_SQ]
    zsk = z[:, _C_SK:_C_SV]
    zsv = z[:, _C_SV:_C_END]
    put(dk32, zdk)
    put(dv32, zdv)
    put(sk32, zsk)
    put(sv32, zsv)
    put(zi32, zi)
    put(dq16, z[:, _C_DQ:_C_DK])
    put(dk16, zdk)
    put(dv16, zdv)
    put(sq16, z[:, _C_SQ:_C_SK])
    put(sk16, zsk)
    put(sv16, zsv)


def _projection(x, sc, sh, g, wm, wih, wil, bt, tt):
    B, T, _ = x.shape
    grid = (B // bt, T // tt)
    row = lambda w: pl.BlockSpec((bt, tt, w), lambda b, t: (b, t, 0))
    per_b = pl.BlockSpec((bt, 1, D_MODEL), lambda b, t: (b, 0, 0))
    full = lambda a: pl.BlockSpec(a.shape, lambda b, t: (0,) * a.ndim,
                                  pipeline_mode=pl.Buffered(1))
    outs = [(DIFF_W, F32), (DIFF_W, F32), (DSA_KV_W, F32), (DSA_KV_W, F32), (IDX_W, F32),
            (2 * DIFF_W, BF16), (DIFF_W, BF16), (DIFF_W, BF16),
            (N_DSA * LANE, BF16), (DSA_KV_W, BF16), (DSA_KV_W, BF16)]
    return pl.pallas_call(
        _proj_kernel,
        grid=grid,
        in_specs=[row(D_MODEL), per_b, per_b, full(g), full(wm), full(wih), full(wil)],
        out_specs=[row(w) for w, _ in outs],
        out_shape=[jax.ShapeDtypeStruct((B, T, w), dt) for w, dt in outs],
        compiler_params=pltpu.CompilerParams(dimension_semantics=("arbitrary", "arbitrary"),
                                             vmem_limit_bytes=VMEM_LIMIT),
    )(x, sc, sh, g, wm, wih, wil)


def _init_bias_tiles(tab_ref, tb_ref):
    r = lax.broadcasted_iota(I32, (LANE, LANE), 0)
    c = lax.broadcasted_iota(I32, (LANE, LANE), 1)
    buckets = []
    for t in range(N_BIAS_TILES - 1):
        rel = c - r - LANE * t
        n = jnp.abs(rel)
        large = jnp.full_like(n, 8)
        for step in BUCKET_STEPS:
            large = large + jnp.where(n >= step, 1, 0)
        buckets.append(jnp.where(rel > 0, NUM_BUCKETS // 2, 0) + jnp.where(n < 8, n, large))

    def per_head(h, carry):
        for t, bucket in enumerate(buckets):
            val = jnp.zeros((LANE, LANE), F32)
            for b in range(NUM_BUCKETS):
                val = jnp.where(bucket == b, tab_ref[b, h], val)
            tb_ref[h, t] = val
        tb_ref[h, N_BIAS_TILES - 1] = jnp.full((LANE, LANE), tab_ref[FAR_BUCKET, h], F32)
        return carry

    lax.fori_loop(0, N_HEADS, per_head, 0)


def _attn_kernel(tab_ref, lam_ref, gsub_ref, dq_ref, sq_ref, iq_ref, iw_ref,
                 dk_ref, dv_ref, sk_ref, sv_ref, ik_ref, o_ref,
                 tb_ref, key_ref, acc_ref, m_ref, l_ref, *, tq, q_offset):
    tk = KEY_BLOCK
    nsub = tk // LANE
    b = pl.program_id(0)
    i = pl.program_id(1)

    @pl.when((b == 0) & (i == 0))
    def _():
        _init_bias_tiles(tab_ref, tb_ref)

    qpos0 = q_offset + i * tq
    lim_last = CHUNK * ((qpos0 + tq - 1) // CHUNK + 1)
    nblk = (lim_last + tk - 1) // tk
    rows = lax.broadcasted_iota(I32, (tq, 1), 0)
    lim_row = CHUNK * ((qpos0 + rows) // CHUNK + 1)
    col0 = lax.broadcasted_iota(I32, (tq, tk), 1)

    def bias_tile(head, jb):
        parts = []
        for c in range(nsub):
            t = jnp.clip((qpos0 - (jb * tk + c * LANE)) // LANE, 0, N_BIAS_TILES - 1)
            parts.append(tb_ref[head, t, 0:tq, :])
        return jnp.concatenate(parts, axis=1)

    iw = iw_ref[...]
    w_idx = [iw[:, IDX_DH + h:IDX_DH + h + 1] * (1.0 / 16.0) for h in range(N_IDX)]

    def score_block(jb, carry):
        kc = ik_ref[pl.ds(pl.multiple_of(jb * tk, tk), tk), :]
        sc = jnp.zeros((tq, tk), F32)
        for h in range(N_IDX):
            d = _dot_t(iq_ref[:, h * 256:(h + 1) * 256], kc)
            sc = sc + jnp.maximum(d, 0.0) * w_idx[h]
        sc = jnp.where(col0 + jb * tk < lim_row, sc, NEG)
        bits = pltpu.bitcast(sc, I32)
        key_ref[jb] = bits ^ ((bits >> 31) & 0x7FFFFFFF)
        return carry

    lax.fori_loop(0, nblk, score_block, 0)

    def count_below(cand, below):
        def body(jb, acc):
            kb = key_ref[jb]
            for c in range(nsub):
                part = kb[:, c * LANE:(c + 1) * LANE]
                hit = (part < cand) if below else (part >= cand)
                acc = acc + jnp.where(hit, 1.0, 0.0)
            return acc
        acc = lax.fori_loop(0, nblk, body, jnp.zeros((tq, LANE), F32))
        return jnp.sum(acc, axis=1, keepdims=True)

    thr = jnp.where(count_below(jnp.zeros((tq, 1), I32), False) >= TOPK, 0, INT_MIN).astype(I32)

    def thr_bit(it, thr):
        cand = thr + jnp.left_shift(jnp.int32(1), 30 - it)
        return jnp.where(count_below(cand, False) >= TOPK, cand, thr)

    thr = lax.fori_loop(0, 31, thr_bit, thr)

    big = jnp.int32(2 ** 30)

    def rank_block(jb, carry):
        kb = key_ref[jb]
        key_ref[jb] = jnp.where(kb > thr, -1, jnp.where(kb == thr, col0 + jb * tk, big))
        return carry

    lax.fori_loop(0, nblk, rank_block, 0)

    def cut_bit(it, cut):
        cand = cut + jnp.left_shift(jnp.int32(1), 12 - it)
        return jnp.where(count_below(cand, True) <= TOPK, cand, cut)

    cut = lax.fori_loop(0, 13, cut_bit, jnp.zeros((tq, 1), I32))

    def mask_block(jb, carry):
        sel = jnp.where(key_ref[jb] < cut, 0.0, NEG)
        sel = jnp.where(col0 + jb * tk < lim_row, sel, NEG)
        key_ref[jb] = pltpu.bitcast(sel.astype(F32), I32)
        return carry

    lax.fori_loop(0, nblk, mask_block, 0)

    def reset(n):
        for s in range(n):
            m_ref[s] = jnp.full((tq, 1), NEG, F32)
            l_ref[s] = jnp.zeros((tq, 1), F32)
            acc_ref[s] = jnp.zeros((tq, LANE), F32)

    def online(slot, s, v):
        m_old = m_ref[slot]
        m_new = jnp.maximum(m_old, jnp.max(s, axis=1, keepdims=True))
        alpha = jnp.exp(m_old - m_new)
        p = jnp.exp(s - m_new)
        l_ref[slot] = alpha * l_ref[slot] + jnp.sum(p, axis=1, keepdims=True)
        acc_ref[slot] = alpha * acc_ref[slot] + _dot(p.astype(BF16), v)
        m_ref[slot] = m_new

    reset(N_DSA)

    def dsa_block(jb, carry):
        ks = pl.ds(pl.multiple_of(jb * tk, tk), tk)
        sel = pltpu.bitcast(key_ref[jb], F32)
        kb = sk_ref[ks, :]
        vb = sv_ref[ks, :]
        for h in range(N_DSA):
            s = _dot_t(sq_ref[:, h * LANE:(h + 1) * LANE], kb)
            s = s + bias_tile(N_DIFF + h, jb) + sel
            online(h, s, vb)
        return carry

    lax.fori_loop(0, nblk, dsa_block, 0)

    for h in range(N_DSA):
        g = h // (N_DSA // N_DSA_KV)
        out = acc_ref[h][:, g * DSA_DH:(g + 1) * DSA_DH] / l_ref[h]
        o_ref[:, DIFF_W + h * DSA_DH:DIFF_W + (h + 1) * DSA_DH] = out.astype(o_ref.dtype)

    reset(2 * N_DIFF)

    def diff_block(jb, carry):
        ks = pl.ds(pl.multiple_of(jb * tk, tk), tk)
        vis = jnp.where(col0 + jb * tk < lim_row, 0.0, NEG)
        for h in range(N_DIFF):
            kb = dk_ref[ks, h * LANE:(h + 1) * LANE]
            vb = dv_ref[ks, h * LANE:(h + 1) * LANE]
            bias = bias_tile(h, jb) + vis
            for part in range(2):
                slot = 2 * h + part
                s = _dot_t(dq_ref[:, slot * LANE:(slot + 1) * LANE], kb) + bias
                online(slot, s, vb)
        return carry

    lax.fori_loop(0, nblk, diff_block, 0)

    lam = lam_ref[...]
    lam_full = (jnp.exp(jnp.sum(lam[0:1] * lam[1:2], axis=1, keepdims=True))
                - jnp.exp(jnp.sum(lam[2:3] * lam[3:4], axis=1, keepdims=True)) + LAM_INIT)
    for h in range(N_DIFF):
        od = acc_ref[2 * h] / l_ref[2 * h] - lam_full * (acc_ref[2 * h + 1] / l_ref[2 * h + 1])
        od = (_rms(od) * gsub_ref[...]) * (1.0 - LAM_INIT)
        o_ref[:, h * LANE:(h + 1) * LANE] = od.astype(o_ref.dtype)


def _attention(tab, lam4, gsub, dq, sq, iq, zi, dk, dv, sk, sv, ik, tq, q_offset):
    B, T, _ = dq.shape
    Lp = dk.shape[1]
    nblk_max = Lp // KEY_BLOCK
    qspec = lambda w, blk=0: pl.BlockSpec((None, tq, w), lambda b, i: (b, i, blk))
    kspec = lambda w: pl.BlockSpec((None, Lp, w), lambda b, i: (b, 0, 0))
    small = lambda a: pl.BlockSpec(a.shape, lambda b, i: (0,) * a.ndim)
    kern = functools.partial(_attn_kernel, tq=tq, q_offset=q_offset)
    return pl.pallas_call(
        kern,
        grid=(B, T // tq),
        in_specs=[pl.BlockSpec(memory_space=pltpu.SMEM), small(lam4), small(gsub),
                  qspec(2 * DIFF_W), qspec(N_DSA * LANE), qspec(N_IDX * 256),
                  qspec(LANE, IDX_W // LANE - 1),
                  kspec(DIFF_W), kspec(DIFF_W), kspec(DSA_KV_W), kspec(DSA_KV_W), kspec(256)],
        out_specs=pl.BlockSpec((None, tq, D_MODEL), lambda b, i: (b, i, 0)),
        out_shape=jax.ShapeDtypeStruct((B, T, D_MODEL), BF16),
        scratch_shapes=[pltpu.VMEM((N_HEADS, N_BIAS_TILES, LANE, LANE), F32),
                        pltpu.VMEM((nblk_max, tq, KEY_BLOCK), I32),
                        pltpu.VMEM((2 * N_DIFF, tq, LANE), F32),
                        pltpu.VMEM((2 * N_DIFF, tq, 1), F32),
                        pltpu.VMEM((2 * N_DIFF, tq, 1), F32)],
        compiler_params=pltpu.CompilerParams(dimension_semantics=("arbitrary", "arbitrary"),
                                             vmem_limit_bytes=VMEM_LIMIT),
    )(tab, lam4, gsub, dq, sq, iq, zi, dk, dv, sk, sv, ik)


FF_CHUNK = D_FF // 2


def _ffn_kernel(x_ref, o_ref, ga1_ref, sc2_ref, sh2_ref, ga2_ref, gpm_ref, gpf_ref, gpo_ref,
                wo_ref, wgu_ref, wd_ref, y_ref):
    bt, tt, _ = x_ref.shape
    rows = bt * tt
    m = _dot(o_ref[...].reshape(rows, D_MODEL), wo_ref[...]).reshape(bt, tt, D_MODEL)
    x1 = x_ref[...] + ga1_ref[...] * (_rms(m) * gpm_ref[...])
    h2 = (_rms(x1) * gpf_ref[...]) * (1.0 + sc2_ref[...]) + sh2_ref[...]
    h2 = h2.reshape(rows, D_MODEL).astype(BF16)
    f = jnp.zeros((rows, D_MODEL), F32)
    for c in range(D_FF // FF_CHUNK):
        g = _dot(h2, wgu_ref[:, c * FF_CHUNK:(c + 1) * FF_CHUNK])
        u = _dot(h2, wgu_ref[:, D_FF + c * FF_CHUNK:D_FF + (c + 1) * FF_CHUNK])
        a = (g * jax.nn.sigmoid(g) * u).astype(BF16)
        f = f + _dot(a, wd_ref[c * FF_CHUNK:(c + 1) * FF_CHUNK, :])
    f = f.reshape(bt, tt, D_MODEL)
    y_ref[...] = x1 + ga2_ref[...] * (_rms(f) * gpo_ref[...])


def _out_ffn(x, o, ga1, sc2, sh2, ga2, gpm, gpf, gpo, wo, wgu, wd, bt, tt):
    B, T, _ = x.shape
    row = pl.BlockSpec((bt, tt, D_MODEL), lambda b, t: (b, t, 0))
    per_b = pl.BlockSpec((bt, 1, D_MODEL), lambda b, t: (b, 0, 0))
    full = lambda a: pl.BlockSpec(a.shape, lambda b, t: (0,) * a.ndim,
                                  pipeline_mode=pl.Buffered(1))
    return pl.pallas_call(
        _ffn_kernel,
        grid=(B // bt, T // tt),
        in_specs=[row, row, per_b, per_b, per_b, per_b, full(gpm), full(gpf), full(gpo),
                  full(wo), full(wgu), full(wd)],
        out_specs=row,
        out_shape=jax.ShapeDtypeStruct((B, T, D_MODEL), F32),
        compiler_params=pltpu.CompilerParams(dimension_semantics=("arbitrary", "arbitrary"),
                                             vmem_limit_bytes=VMEM_LIMIT),
    )(x, o, ga1, sc2, sh2, ga2, gpm, gpf, gpo, wo, wgu, wd)


def _split_hi_lo(a):
    hi = a.astype(BF16)
    lo = (a - hi.astype(F32)).astype(BF16)
    return hi, lo


def _prep_in_weights(w_in):
    cuts = [0, 512, 1024, 1536, 2048, 2176, 2304, 2560, 2624, 2628]
    wdq, wdk, wdv, wsq, wsk, wsv, wiq, wik, wiw = [w_in[:, a:b] for a, b in zip(cuts[:-1], cuts[1:])]
    zeros64 = jnp.zeros((D_MODEL, DIFF_DH), F32)
    scale = DIFF_DH ** -0.5
    dq_cols = []
    for h in range(N_DIFF):
        q1 = wdq[:, h * 128:h * 128 + 64] * scale
        q2 = wdq[:, h * 128 + 64:(h + 1) * 128] * scale
        dq_cols += [q1, zeros64, zeros64, q2]
    sq_cols = []
    for h in range(N_DSA):
        q = wsq[:, h * 64:(h + 1) * 64] * scale
        sq_cols += [q, zeros64] if h < N_DSA // N_DSA_KV else [zeros64, q]
    wm = jnp.concatenate(dq_cols + [wdk, wdv] + sq_cols + [wsk, wsv], axis=1).astype(BF16)
    wi = jnp.concatenate([wiq, wik, wiw, jnp.zeros((D_MODEL, IDX_W - 324), F32)], axis=1)
    wih, wil = _split_hi_lo(wi)
    return wm, wih, wil


def _idx_operands(iq, ik):
    qh, ql = _split_hi_lo(iq)
    kh, kl = _split_hi_lo(ik)
    zq = jnp.zeros(qh.shape[:-1] + (IDX_DH,), BF16)
    parts = []
    for h in range(N_IDX):
        s = slice(h * IDX_DH, (h + 1) * IDX_DH)
        parts += [qh[..., s], qh[..., s], ql[..., s], zq]
    zk = jnp.zeros(kh.shape[:-1] + (IDX_DH,), BF16)
    return jnp.concatenate(parts, axis=-1), jnp.concatenate([kh, kl, kh, zk], axis=-1)


def _pad_keys(a, lp):
    return jnp.pad(a, ((0, 0), (0, lp - a.shape[1]), (0, 0)))


def _group(x, mods, past, params, bt, tt, tq):
    (g_pre_mix, g_post_mix, g_pre_ffn, g_post_ffn, wm, wih, wil, lam4, gsub,
     wo, wgu, wd, tab) = params
    B, T, _ = x.shape
    sh1, sc1, ga1, sh2, sc2, ga2 = mods
    (dk32, dv32, sk32, sv32, zi32, dq16, dk16, dv16, sq16, sk16, sv16) = _projection(
        x, sc1, sh1, g_pre_mix, wm, wih, wil, bt, tt)
    ik32 = zi32[..., 256:256 + IDX_DH]
    if past is None:
        q_offset = 0
        iqc, ikc = _idx_operands(zi32[..., :256], ik32)
        keys = (dk16, dv16, sk16, sv16, ikc)
    else:
        pk, pv, psk, psv, pik = past
        P = pk.shape[1]
        q_offset = P
        lp = -(-(P + T) // KEY_BLOCK) * KEY_BLOCK
        iqc, ikc = _idx_operands(zi32[..., :256], jnp.concatenate([pik, ik32], axis=1))
        cat = lambda p, n: _pad_keys(
            jnp.concatenate([p.reshape(B, P, -1).astype(BF16), n], axis=1), lp)
        keys = (cat(pk, dk16), cat(pv, dv16), cat(psk, sk16), cat(psv, sv16), _pad_keys(ikc, lp))
    o = _attention(tab, lam4, gsub, dq16, sq16, iqc, zi32, *keys, tq, q_offset)
    y = _out_ffn(x, o, ga1, sc2, sh2, ga2, g_post_mix, g_pre_ffn, g_post_ffn, wo, wgu, wd, bt, tt)
    rows = (dk32.reshape(1, B, T, N_DIFF, 2 * DIFF_DH), dv32.reshape(1, B, T, N_DIFF, 2 * DIFF_DH),
            sk32.reshape(1, B, T, N_DSA_KV, DSA_DH), sv32.reshape(1, B, T, N_DSA_KV, DSA_DH),
            ik32.reshape(1, B, T, IDX_DH))
    return y, rows


def kernel(x_prompt, x_sample, cache_diff_k, cache_diff_v, cache_dsa_k, cache_dsa_v, cache_dsa_kidx,
           c_prompt, c_sample, w_ada, b_ada, g_pre_mix, g_post_mix, g_pre_ffn, g_post_ffn, w_in,
           lambda_q1, lambda_k1, lambda_q2, lambda_k2, g_subln, w_out, w_gate_up, w_down,
           rel_bias_table):
    Bp = x_prompt.shape[0]
    c_all = jnp.concatenate([c_prompt, c_sample], axis=0)
    mod = _modulation(c_all, w_ada[0], b_ada[0])
    mods = [mod[:, j * D_MODEL:(j + 1) * D_MODEL][:, None, :] for j in range(6)]
    mods_p = [m[:Bp] for m in mods]
    mods_s = [m[Bp:] for m in mods]

    wm, wih, wil = _prep_in_weights(w_in[0])
    lam4 = jnp.concatenate([lambda_q1, lambda_k1, lambda_q2, lambda_k2], axis=0)
    row = lambda g: g[0].reshape(1, -1)
    params = (row(g_pre_mix), row(g_post_mix), row(g_pre_ffn), row(g_post_ffn), wm, wih, wil,
              lam4, row(g_subln), w_out[0].astype(BF16), w_gate_up[0].astype(BF16),
              w_down[0].astype(BF16), rel_bias_table)

    yp, rp = _group(x_prompt, mods_p, None, params, bt=1, tt=512, tq=128)
    past = (cache_diff_k[0], cache_diff_v[0], cache_dsa_k[0], cache_dsa_v[0], cache_dsa_kidx[0])
    ys, rs = _group(x_sample, mods_s, past, params, bt=8, tt=64, tq=64)
    return (yp, ys) + rp + rs
```

```python
import functools
import math

import jax
import jax.numpy as jnp
from jax import lax
from jax.experimental import pallas as pl
from jax.experimental.pallas import tpu as pltpu

F32 = jnp.float32
BF16 = jnp.bfloat16
I32 = jnp.int32

D_MODEL = 1024
CHUNK = 64
N_DIFF = 4
DIFF_DH = 64
DIFF_W = N_DIFF * 2 * DIFF_DH
N_DSA = 8
N_DSA_KV = 2
DSA_GROUP = N_DSA // N_DSA_KV
DSA_DH = 64
DSA_W = N_DSA * DSA_DH
DSA_KV_W = N_DSA_KV * DSA_DH
N_IDX = 4
IDX_DH = 64
TOPK = 256
N_HEADS = N_DIFF + N_DSA
NUM_BUCKETS = 32
D_FF = 2816
EPS = 1e-6
NEG = -1e30
LAM_INIT = 0.8 - 0.6 * math.exp(-0.3 * 0)

LANE = 128
SUBLANE = 8
KEY_BLOCK = 256
V_ROWS = 144
VMEM_LIMIT = 56 * 1024 * 1024
INT_MIN = -2 ** 31
FAR_BUCKET = 15
BUCKET_STEPS = (13, 20, 30, 46, 70, 108, 166)
N_BIAS_TILES = 4
N_STAGE = 4


def _rms(x):
    return x * lax.rsqrt(jnp.mean(x * x, axis=-1, keepdims=True) + EPS)


def _dot(a, b):
    return jnp.dot(a, b, preferred_element_type=F32)


def _dot_t(a, b):
    return lax.dot_general(a, b, (((1,), (1,)), ((), ())), preferred_element_type=F32)


def _mod_kernel(c_ref, w_ref, b_ref, o_ref):
    c = c_ref[...]
    s = c * jax.nn.sigmoid(c)
    o_ref[...] = jnp.dot(s, w_ref[...], preferred_element_type=F32,
                         precision=lax.Precision.HIGHEST) + b_ref[...]


def _modulation(c_all, w_ada, b_ada):
    n = c_all.shape[0]
    tn = 1024
    return pl.pallas_call(
        _mod_kernel,
        grid=(6 * D_MODEL // tn,),
        in_specs=[pl.BlockSpec((n, D_MODEL), lambda j: (0, 0)),
                  pl.BlockSpec((D_MODEL, tn), lambda j: (0, j)),
                  pl.BlockSpec((1, tn), lambda j: (0, j))],
        out_specs=pl.BlockSpec((n, tn), lambda j: (0, j)),
        out_shape=jax.ShapeDtypeStruct((n, 6 * D_MODEL), F32),
        name="adaln_mod",
        compiler_params=pltpu.CompilerParams(dimension_semantics=("arbitrary",),
                                             vmem_limit_bytes=VMEM_LIMIT),
    )(c_all, w_ada, b_ada.reshape(1, -1))


_C_DQ = 0
_C_DK = _C_DQ + 2 * DIFF_W
_C_DV = _C_DK + DIFF_W
_C_SQ = _C_DV + DIFF_W
_C_SK = _C_SQ + N_DSA * LANE
_C_SV = _C_SK + DSA_KV_W
_C_END = _C_SV + DSA_KV_W
IDX_W = 384


def _proj_kernel(x_ref, sc_ref, sh_ref, g_ref, wm_ref, wih_ref, wil_ref,
                 dk32, dv32, sk32, sv32, zi32, dq16, dk16, dv16, sq16, sk16, sv16):
    bt, tt, _ = x_ref.shape
    x = x_ref[...]
    h = (_rms(x) * g_ref[...]) * (1.0 + sc_ref[...]) + sh_ref[...]
    h = h.reshape(bt * tt, D_MODEL)
    hb = h.astype(BF16)
    hl = (h - hb.astype(F32)).astype(BF16)
    z = _dot(hb, wm_ref[...])
    wih = wih_ref[...]
    zi = _dot(hb, wih) + _dot(hb, wil_ref[...]) + _dot(hl, wih)

    def put(ref, v):
        ref[...] = v.reshape(bt, tt, v.shape[-1]).astype(ref.dtype)

    zdk = z[:, _C_DK:_C_DV]
    zdv = z[:, _C_DV:_C_SQ]
    zsk = z[:, _C_SK:_C_SV]
    zsv = z[:, _C_SV:_C_END]
    put(dk32, zdk)
    put(dv32, zdv)
    put(sk32, zsk)
    put(sv32, zsv)
    put(zi32, zi)
    put(dq16, z[:, _C_DQ:_C_DK])
    put(dk16, zdk)
    put(dv16, zdv)
    put(sq16, z[:, _C_SQ:_C_SK])
    put(sk16, zsk)
    put(sv16, zsv)


def _projection(x, sc, sh, g, wm, wih, wil, bt, tt):
    B, T, _ = x.shape
    grid = (B // bt, T // tt)
    row = lambda w: pl.BlockSpec((bt, tt, w), lambda b, t: (b, t, 0))
    per_b = pl.BlockSpec((bt, 1, D_MODEL), lambda b, t: (b, 0, 0))
    full = lambda a: pl.BlockSpec(a.shape, lambda b, t: (0,) * a.ndim,
                                  pipeline_mode=pl.Buffered(1))
    outs = [(DIFF_W, F32), (DIFF_W, F32), (DSA_KV_W, F32), (DSA_KV_W, F32), (IDX_W, F32),
            (2 * DIFF_W, BF16), (DIFF_W, BF16), (DIFF_W, BF16),
            (N_DSA * LANE, BF16), (DSA_KV_W, BF16), (DSA_KV_W, BF16)]
    return pl.pallas_call(
        _proj_kernel,
        grid=grid,
        in_specs=[row(D_MODEL), per_b, per_b, full(g), full(wm), full(wih), full(wil)],
        out_specs=[row(w) for w, _ in outs],
        out_shape=[jax.ShapeDtypeStruct((B, T, w), dt) for w, dt in outs],
        name="in_proj",
        compiler_params=pltpu.CompilerParams(dimension_semantics=("arbitrary", "arbitrary"),
                                             vmem_limit_bytes=VMEM_LIMIT),
    )(x, sc, sh, g, wm, wih, wil)


def _fold8(x, op):
    acc = x[0:SUBLANE]
    for r in range(1, x.shape[0] // SUBLANE):
        acc = op(acc, x[r * SUBLANE:(r + 1) * SUBLANE])
    return acc


def _init_bias_tiles(tab_ref, tb_ref):
    k = lax.broadcasted_iota(I32, (LANE, LANE), 0)
    q = lax.broadcasted_iota(I32, (LANE, LANE), 1)
    buckets = []
    for t in range(N_BIAS_TILES - 1):
        rel = k - q - LANE * t
        n = jnp.abs(rel)
        large = jnp.full_like(n, 8)
        for step in BUCKET_STEPS:
            large = large + jnp.where(n >= step, 1, 0)
        buckets.append(jnp.where(rel > 0, NUM_BUCKETS // 2, 0) + jnp.where(n < 8, n, large))

    def per_head(h, carry):
        for t, bucket in enumerate(buckets):
            val = jnp.zeros((LANE, LANE), F32)
            for b in range(NUM_BUCKETS):
                val = jnp.where(bucket == b, tab_ref[b, h], val)
            tb_ref[h, t] = val
        tb_ref[h, N_BIAS_TILES - 1] = jnp.full((LANE, LANE), tab_ref[FAR_BUCKET, h], F32)
        return carry

    lax.fori_loop(0, N_HEADS, per_head, 0)


def _attn_kernel(tab_ref, lam_ref, gsub_ref, dq_ref, sq_ref, iq_ref, iw_ref,
                 dk_ref, dv_ref, sk_ref, sv_ref, ik_ref, o_ref,
                 tb_ref, key_ref, s_ref, acc_ref, ot_ref, *, tq, q_offset):
    tk = KEY_BLOCK
    b = pl.program_id(0)
    i = pl.program_id(1)

    @pl.when((b == 0) & (i == 0))
    def _():
        _init_bias_tiles(tab_ref, tb_ref)

    qpos0 = q_offset + i * tq
    lim_last = CHUNK * ((qpos0 + tq - 1) // CHUNK + 1)
    nblk = (lim_last + tk - 1) // tk
    qq = lax.broadcasted_iota(I32, (1, tq), 1)
    lim_q = CHUNK * ((qpos0 + qq) // CHUNK + 1)
    kk = lax.broadcasted_iota(I32, (tk, tq), 0)

    def key_rows(jb):
        return pl.ds(pl.multiple_of(jb * tk, tk), tk)

    def bias_tile(head, jb):
        rows = []
        for a in range(tk // LANE):
            parts = []
            for c in range(tq // LANE):
                t = jnp.clip((qpos0 + c * LANE - (jb * tk + a * LANE)) // LANE, 0, N_BIAS_TILES - 1)
                parts.append(tb_ref[head, t])
            rows.append(jnp.concatenate(parts, axis=1))
        return jnp.concatenate(rows, axis=0)

    iw = iw_ref[...]

    def score_block(jb, carry):
        kc = ik_ref[key_rows(jb), :]
        sc = jnp.zeros((tk, tq), F32)
        for h in range(N_IDX):
            d = _dot_t(kc, iq_ref[:, h * 256:(h + 1) * 256])
            sc = sc + jnp.maximum(d, 0.0) * iw[h:h + 1]
        sc = jnp.where(kk + jb * tk < lim_q, sc, NEG)
        bits = pltpu.bitcast(sc, I32)
        key_ref[jb] = bits ^ ((bits >> 31) & 0x7FFFFFFF)
        return carry

    lax.fori_loop(0, nblk, score_block, 0)

    def count(cand, below):
        def body(jb, acc):
            kb = key_ref[jb]
            hit = (kb < cand) if below else (kb >= cand)
            return acc + _fold8(jnp.where(hit, 1.0, 0.0), jnp.add)
        acc = lax.fori_loop(0, nblk, body, jnp.zeros((SUBLANE, tq), F32))
        return jnp.sum(acc, axis=0, keepdims=True)

    thr = jnp.where(count(jnp.zeros((1, tq), I32), False) >= TOPK, 0, INT_MIN).astype(I32)

    def thr_bit(it, thr):
        cand = thr + jnp.left_shift(jnp.int32(1), 30 - it)
        return jnp.where(count(cand, False) >= TOPK, cand, thr)

    thr = lax.fori_loop(0, 31, thr_bit, thr)

    big = jnp.int32(2 ** 30)

    def rank_block(jb, carry):
        kb = key_ref[jb]
        key_ref[jb] = jnp.where(kb > thr, -1, jnp.where(kb == thr, kk + jb * tk, big))
        return carry

    lax.fori_loop(0, nblk, rank_block, 0)

    def cut_bit(it, cut):
        cand = cut + jnp.left_shift(jnp.int32(1), 12 - it)
        return jnp.where(count(cand, True) <= TOPK, cand, cut)

    cut = lax.fori_loop(0, 13, cut_bit, jnp.zeros((1, tq), I32))

    def mask_block(jb, carry):
        sel = jnp.where(key_ref[jb] < cut, 0.0, NEG)
        sel = jnp.where(kk + jb * tk < lim_q, sel, NEG)
        key_ref[jb] = pltpu.bitcast(sel.astype(F32), I32)
        return carry

    lax.fori_loop(0, nblk, mask_block, 0)

    def attend(scores, values):
        def pass1(jb, mx):
            out = []
            for n, s in enumerate(scores(jb)):
                s_ref[n, jb] = s
                out.append(jnp.maximum(mx[n], _fold8(s, jnp.maximum)))
            return tuple(out)

        mx = lax.fori_loop(0, nblk, pass1,
                           tuple(jnp.full((SUBLANE, tq), -3e38, F32) for _ in range(N_STAGE)))
        m = [jnp.max(x, axis=0, keepdims=True) for x in mx]
        for n in range(N_STAGE):
            acc_ref[n] = jnp.zeros((V_ROWS, tq), F32)

        def pass2(jb, carry):
            for n in range(N_STAGE):
                p = jnp.exp(s_ref[n, jb] - m[n]).astype(BF16)
                acc_ref[n] += _dot(values(n, jb), p)
            return carry

        lax.fori_loop(0, nblk, pass2, 0)

    for g in range(N_DSA_KV):
        def dsa_scores(jb, g=g):
            kb = sk_ref[key_rows(jb), :]
            sel = pltpu.bitcast(key_ref[jb], F32)
            out = []
            for n in range(DSA_GROUP):
                h = g * DSA_GROUP + n
                s = _dot_t(kb, sq_ref[:, h * LANE:(h + 1) * LANE])
                out.append(s + bias_tile(N_DIFF + h, jb) + sel)
            return out

        attend(dsa_scores, lambda n, jb: sv_ref[jb])
        for n in range(DSA_GROUP):
            h = g * DSA_GROUP + n
            a = acc_ref[n]
            out = a[g * DSA_DH:(g + 1) * DSA_DH] / a[2 * DSA_DH:2 * DSA_DH + 1]
            ot_ref[DIFF_W + h * DSA_DH:DIFF_W + (h + 1) * DSA_DH, :] = out

    lam = lam_ref[...]
    lam_full = (jnp.exp(jnp.sum(lam[0:1] * lam[1:2], axis=1, keepdims=True))
                - jnp.exp(jnp.sum(lam[2:3] * lam[3:4], axis=1, keepdims=True)) + LAM_INIT)
    for pair in range(N_DIFF // 2):
        def diff_scores(jb, pair=pair):
            vis = jnp.where(kk + jb * tk < lim_q, 0.0, NEG)
            out = []
            for hh in range(2):
                h = 2 * pair + hh
                kb = dk_ref[key_rows(jb), h * LANE:(h + 1) * LANE]
                bias = bias_tile(h, jb) + vis
                for part in range(2):
                    slot = 2 * h + part
                    out.append(_dot_t(kb, dq_ref[:, slot * LANE:(slot + 1) * LANE]) + bias)
            return out

        attend(diff_scores, lambda n, jb, pair=pair: dv_ref[2 * pair + n // 2, jb])
        for hh in range(2):
            h = 2 * pair + hh
            a1 = acc_ref[2 * hh]
            a2 = acc_ref[2 * hh + 1]
            od = (a1[0:LANE] / a1[LANE:LANE + 1]
                  - lam_full * (a2[0:LANE] / a2[LANE:LANE + 1]))
            od = od * lax.rsqrt(jnp.mean(od * od, axis=0, keepdims=True) + EPS)
            ot_ref[h * LANE:(h + 1) * LANE, :] = (od * gsub_ref[...]) * (1.0 - LAM_INIT)

    o_ref[...] = ot_ref[...].T.astype(o_ref.dtype)


def _attention(tab, lam4, gsub_t, dq, sq, iq, iw_t, dk, dv_t, sk, sv_t, ik, tq, q_offset):
    B, T, _ = dq.shape
    Lp = dk.shape[1]
    nblk_max = Lp // KEY_BLOCK
    qspec = lambda w: pl.BlockSpec((None, tq, w), lambda b, i: (b, i, 0))
    once = pl.Buffered(1)
    kspec = lambda w: pl.BlockSpec((None, Lp, w), lambda b, i: (b, 0, 0), pipeline_mode=once)
    small = lambda a: pl.BlockSpec(a.shape, lambda b, i: (0,) * a.ndim)
    kern = functools.partial(_attn_kernel, tq=tq, q_offset=q_offset)
    return pl.pallas_call(
        kern,
        grid=(B, T // tq),
        in_specs=[pl.BlockSpec(memory_space=pltpu.SMEM), small(lam4), small(gsub_t),
                  qspec(2 * DIFF_W), qspec(N_DSA * LANE), qspec(N_IDX * 256),
                  pl.BlockSpec((None, SUBLANE, tq), lambda b, i: (b, 0, i)),
                  kspec(DIFF_W),
                  pl.BlockSpec((None, N_DIFF, nblk_max, V_ROWS, KEY_BLOCK),
                               lambda b, i: (b, 0, 0, 0, 0), pipeline_mode=once),
                  kspec(DSA_KV_W),
                  pl.BlockSpec((None, nblk_max, V_ROWS, KEY_BLOCK),
                               lambda b, i: (b, 0, 0, 0), pipeline_mode=once),
                  kspec(256)],
        out_specs=pl.BlockSpec((None, tq, D_MODEL), lambda b, i: (b, i, 0)),
        out_shape=jax.ShapeDtypeStruct((B, T, D_MODEL), BF16),
        name="attn",
        scratch_shapes=[pltpu.VMEM((N_HEADS, N_BIAS_TILES, LANE, LANE), F32),
                        pltpu.VMEM((nblk_max, KEY_BLOCK, tq), I32),
                        pltpu.VMEM((N_STAGE, nblk_max, KEY_BLOCK, tq), F32),
                        pltpu.VMEM((N_STAGE, V_ROWS, tq), F32),
                        pltpu.VMEM((D_MODEL, tq), F32)],
        compiler_params=pltpu.CompilerParams(dimension_semantics=("arbitrary", "arbitrary"),
                                             vmem_limit_bytes=VMEM_LIMIT),
    )(tab, lam4, gsub_t, dq, sq, iq, iw_t, dk, dv_t, sk, sv_t, ik)


FF_CHUNK = D_FF // 2


def _ffn_kernel(x_ref, o_ref, ga1_ref, sc2_ref, sh2_ref, ga2_ref, gpm_ref, gpf_ref, gpo_ref,
                wo_ref, wgu_ref, wd_ref, y_ref):
    bt, tt, _ = x_ref.shape
    rows = bt * tt
    m = _dot(o_ref[...].reshape(rows, D_MODEL), wo_ref[...]).reshape(bt, tt, D_MODEL)
    x1 = x_ref[...] + ga1_ref[...] * (_rms(m) * gpm_ref[...])
    h2 = (_rms(x1) * gpf_ref[...]) * (1.0 + sc2_ref[...]) + sh2_ref[...]
    h2 = h2.reshape(rows, D_MODEL).astype(BF16)
    f = jnp.zeros((rows, D_MODEL), F32)
    for c in range(D_FF // FF_CHUNK):
        g = _dot(h2, wgu_ref[:, c * FF_CHUNK:(c + 1) * FF_CHUNK])
        u = _dot(h2, wgu_ref[:, D_FF + c * FF_CHUNK:D_FF + (c + 1) * FF_CHUNK])
        a = (g * jax.nn.sigmoid(g) * u).astype(BF16)
        f = f + _dot(a, wd_ref[c * FF_CHUNK:(c + 1) * FF_CHUNK, :])
    f = f.reshape(bt, tt, D_MODEL)
    y_ref[...] = x1 + ga2_ref[...] * (_rms(f) * gpo_ref[...])


def _out_ffn(x, o, ga1, sc2, sh2, ga2, gpm, gpf, gpo, wo, wgu, wd, bt, tt):
    B, T, _ = x.shape
    row = pl.BlockSpec((bt, tt, D_MODEL), lambda b, t: (b, t, 0))
    per_b = pl.BlockSpec((bt, 1, D_MODEL), lambda b, t: (b, 0, 0))
    full = lambda a: pl.BlockSpec(a.shape, lambda b, t: (0,) * a.ndim,
                                  pipeline_mode=pl.Buffered(1))
    return pl.pallas_call(
        _ffn_kernel,
        grid=(B // bt, T // tt),
        in_specs=[row, row, per_b, per_b, per_b, per_b, full(gpm), full(gpf), full(gpo),
                  full(wo), full(wgu), full(wd)],
        out_specs=row,
        out_shape=jax.ShapeDtypeStruct((B, T, D_MODEL), F32),
        name="out_ffn",
        compiler_params=pltpu.CompilerParams(dimension_semantics=("arbitrary", "arbitrary"),
                                             vmem_limit_bytes=VMEM_LIMIT),
    )(x, o, ga1, sc2, sh2, ga2, gpm, gpf, gpo, wo, wgu, wd)


def _split_hi_lo(a):
    hi = a.astype(BF16)
    lo = (a - hi.astype(F32)).astype(BF16)
    return hi, lo


def _prep_in_weights(w_in):
    cuts = [0, 512, 1024, 1536, 2048, 2176, 2304, 2560, 2624, 2628]
    wdq, wdk, wdv, wsq, wsk, wsv, wiq, wik, wiw = [w_in[:, a:b] for a, b in zip(cuts[:-1], cuts[1:])]
    zeros64 = jnp.zeros((D_MODEL, DIFF_DH), F32)
    scale = DIFF_DH ** -0.5
    dq_cols = []
    for h in range(N_DIFF):
        q1 = wdq[:, h * 128:h * 128 + 64] * scale
        q2 = wdq[:, h * 128 + 64:(h + 1) * 128] * scale
        dq_cols += [q1, zeros64, zeros64, q2]
    sq_cols = []
    for h in range(N_DSA):
        q = wsq[:, h * 64:(h + 1) * 64] * scale
        sq_cols += [q, zeros64] if h < DSA_GROUP else [zeros64, q]
    wm = jnp.concatenate(dq_cols + [wdk, wdv] + sq_cols + [wsk, wsv], axis=1).astype(BF16)
    wi = jnp.concatenate([wiq, wik, wiw, jnp.zeros((D_MODEL, IDX_W - 324), F32)], axis=1)
    wih, wil = _split_hi_lo(wi)
    return wm, wih, wil


def _idx_operands(iq, ik):
    qh, ql = _split_hi_lo(iq)
    kh, kl = _split_hi_lo(ik)
    zq = jnp.zeros(qh.shape[:-1] + (IDX_DH,), BF16)
    parts = []
    for h in range(N_IDX):
        s = slice(h * IDX_DH, (h + 1) * IDX_DH)
        parts += [qh[..., s], qh[..., s], ql[..., s], zq]
    zk = jnp.zeros(kh.shape[:-1] + (IDX_DH,), BF16)
    return jnp.concatenate(parts, axis=-1), jnp.concatenate([kh, kl, kh, zk], axis=-1)


def _pad_axis1(a, n):
    return jnp.pad(a, ((0, 0), (0, n - a.shape[1])) + ((0, 0),) * (a.ndim - 2))


def _value_operand(v):
    B, L, H, W = v.shape
    vt = jnp.transpose(v, (0, 2, 3, 1))
    extra = jnp.concatenate([jnp.ones((B, H, 1, L), BF16),
                             jnp.zeros((B, H, V_ROWS - W - 1, L), BF16)], axis=2)
    vt = jnp.concatenate([vt, extra], axis=2).reshape(B, H, V_ROWS, L // KEY_BLOCK, KEY_BLOCK)
    return jnp.transpose(vt, (0, 1, 3, 2, 4))


def _group(x, mods, past, params, bt, tt, tq):
    (g_pre_mix, g_post_mix, g_pre_ffn, g_post_ffn, wm, wih, wil, lam4, g_subln,
     wo, wgu, wd, tab) = params
    B, T, _ = x.shape
    sh1, sc1, ga1, sh2, sc2, ga2 = mods
    (dk32, dv32, sk32, sv32, zi32, dq16, dk16, dv16, sq16, sk16, sv16) = _projection(
        x, sc1, sh1, g_pre_mix, wm, wih, wil, bt, tt)
    ik32 = zi32[..., 256:256 + IDX_DH]
    iw = zi32[..., 256 + IDX_DH:256 + IDX_DH + N_IDX] * (1.0 / 16.0)
    if past is None:
        q_offset = 0
        lp = T
        iqc, ikc = _idx_operands(zi32[..., :256], ik32)
        dk, dv, sk, sv = dk16, dv16, sk16, sv16
    else:
        pk, pv, psk, psv, pik = past
        P = pk.shape[1]
        q_offset = P
        lp = -(-(P + T) // KEY_BLOCK) * KEY_BLOCK
        iqc, ikc = _idx_operands(zi32[..., :256], jnp.concatenate([pik, ik32], axis=1))
        cat = lambda p, n: _pad_axis1(
            jnp.concatenate([p.reshape(B, P, -1).astype(BF16), n], axis=1), lp)
        dk, dv, sk, sv = cat(pk, dk16), cat(pv, dv16), cat(psk, sk16), cat(psv, sv16)
        ikc = _pad_axis1(ikc, lp)
    tpad = -(-T // tq) * tq
    qpad = lambda a: _pad_axis1(a, tpad)
    iw_t = _pad_axis1(jnp.transpose(qpad(iw), (0, 2, 1)), SUBLANE)
    dv_t = _value_operand(dv.reshape(B, lp, N_DIFF, 2 * DIFF_DH))
    sv_t = _value_operand(sv.reshape(B, lp, 1, DSA_KV_W))[:, 0]
    gsub_t = jnp.broadcast_to(g_subln.reshape(-1, 1), (2 * DIFF_DH, tq))
    o = _attention(tab, lam4, gsub_t, qpad(dq16), qpad(sq16), qpad(iqc), iw_t,
                   dk, dv_t, sk, sv_t, ikc, tq, q_offset)[:, :T]
    y = _out_ffn(x, o, ga1, sc2, sh2, ga2, g_post_mix, g_pre_ffn, g_post_ffn, wo, wgu, wd, bt, tt)
    rows = (dk32.reshape(1, B, T, N_DIFF, 2 * DIFF_DH), dv32.reshape(1, B, T, N_DIFF, 2 * DIFF_DH),
            sk32.reshape(1, B, T, N_DSA_KV, DSA_DH), sv32.reshape(1, B, T, N_DSA_KV, DSA_DH),
            ik32.reshape(1, B, T, IDX_DH))
    return y, rows


def kernel(x_prompt, x_sample, cache_diff_k, cache_diff_v, cache_dsa_k, cache_dsa_v, cache_dsa_kidx,
           c_prompt, c_sample, w_ada, b_ada, g_pre_mix, g_post_mix, g_pre_ffn, g_post_ffn, w_in,
           lambda_q1, lambda_k1, lambda_q2, lambda_k2, g_subln, w_out, w_gate_up, w_down,
           rel_bias_table):
    Bp = x_prompt.shape[0]
    c_all = jnp.concatenate([c_prompt, c_sample], axis=0)
    mod = _modulation(c_all, w_ada[0], b_ada[0])
    mods = [mod[:, j * D_MODEL:(j + 1) * D_MODEL][:, None, :] for j in range(6)]
    mods_p = [m[:Bp] for m in mods]
    mods_s = [m[Bp:] for m in mods]

    wm, wih, wil = _prep_in_weights(w_in[0])
    lam4 = jnp.concatenate([lambda_q1, lambda_k1, lambda_q2, lambda_k2], axis=0)
    row = lambda g: g[0].reshape(1, -1)
    params = (row(g_pre_mix), row(g_post_mix), row(g_pre_ffn), row(g_post_ffn), wm, wih, wil,
              lam4, g_subln[0], w_out[0].astype(BF16), w_gate_up[0].astype(BF16),
              w_down[0].astype(BF16), rel_bias_table)

    yp, rp = _group(x_prompt, mods_p, None, params, bt=1, tt=512, tq=256)
    past = (cache_diff_k[0], cache_diff_v[0], cache_dsa_k[0], cache_dsa_v[0], cache_dsa_kidx[0])
    ys, rs = _group(x_sample, mods_s, past, params, bt=8, tt=64, tq=128)
    return (yp, ys) + rp + rs
```

```python
import functools
import math

import jax
import jax.numpy as jnp
from jax import lax
from jax.experimental import pallas as pl
from jax.experimental.pallas import tpu as pltpu

F32 = jnp.float32
BF16 = jnp.bfloat16
I32 = jnp.int32

D_MODEL = 1024
CHUNK = 64
N_DIFF = 4
DIFF_DH = 64
DIFF_W = N_DIFF * 2 * DIFF_DH
N_DSA = 8
N_DSA_KV = 2
DSA_GROUP = N_DSA // N_DSA_KV
DSA_DH = 64
DSA_W = N_DSA * DSA_DH
DSA_KV_W = N_DSA_KV * DSA_DH
N_IDX = 4
IDX_DH = 64
TOPK = 256
N_HEADS = N_DIFF + N_DSA
NUM_BUCKETS = 32
D_FF = 2816
EPS = 1e-6
NEG = -1e30
LAM_INIT = 0.8 - 0.6 * math.exp(-0.3 * 0)

LANE = 128
SUBLANE = 8
KEY_BLOCK = 256
V_ROWS = 144
VMEM_LIMIT = 56 * 1024 * 1024
INT_MIN = -2 ** 31
FAR_BUCKET = 15
BUCKET_STEPS = (13, 20, 30, 46, 70, 108, 166)
N_BIAS_TILES = 4
N_STAGE = 4


def _rms(x):
    return x * lax.rsqrt(jnp.mean(x * x, axis=-1, keepdims=True) + EPS)


def _dot(a, b):
    return jnp.dot(a, b, preferred_element_type=F32)


def _dot_t(a, b):
    return lax.dot_general(a, b, (((1,), (1,)), ((), ())), preferred_element_type=F32)


def _mod_kernel(c_ref, w_ref, b_ref, o_ref):
    c = c_ref[...]
    s = c * jax.nn.sigmoid(c)
    o_ref[...] = jnp.dot(s, w_ref[...], preferred_element_type=F32,
                         precision=lax.Precision.HIGHEST) + b_ref[...]


def _modulation(c_all, w_ada, b_ada):
    n = c_all.shape[0]
    tn = 1024
    return pl.pallas_call(
        _mod_kernel,
        grid=(6 * D_MODEL // tn,),
        in_specs=[pl.BlockSpec((n, D_MODEL), lambda j: (0, 0)),
                  pl.BlockSpec((D_MODEL, tn), lambda j: (0, j)),
                  pl.BlockSpec((1, tn), lambda j: (0, j))],
        out_specs=pl.BlockSpec((n, tn), lambda j: (0, j)),
        out_shape=jax.ShapeDtypeStruct((n, 6 * D_MODEL), F32),
        name="adaln_mod",
        compiler_params=pltpu.CompilerParams(dimension_semantics=("arbitrary",),
                                             vmem_limit_bytes=VMEM_LIMIT),
    )(c_all, w_ada, b_ada.reshape(1, -1))


_C_DQ = 0
_C_DK = _C_DQ + 2 * DIFF_W
_C_DV = _C_DK + DIFF_W
_C_SQ = _C_DV + DIFF_W
_C_SK = _C_SQ + N_DSA * LANE
_C_SV = _C_SK + DSA_KV_W
_C_END = _C_SV + DSA_KV_W
IDX_W = 384


def _proj_kernel(x_ref, sc_ref, sh_ref, g_ref, wm_ref, wih_ref, wil_ref,
                 dk32, dv32, sk32, sv32, zi32, dq16, dk16, dv16, sq16, sk16, sv16):
    bt, tt, _ = x_ref.shape
    x = x_ref[...]
    h = (_rms(x) * g_ref[...]) * (1.0 + sc_ref[...]) + sh_ref[...]
    h = h.reshape(bt * tt, D_MODEL)
    hb = h.astype(BF16)
    hl = (h - hb.astype(F32)).astype(BF16)
    z = _dot(hb, wm_ref[...])
    wih = wih_ref[...]
    zi = _dot(hb, wih) + _dot(hb, wil_ref[...]) + _dot(hl, wih)

    def put(ref, v):
        ref[...] = v.reshape(bt, tt, v.shape[-1]).astype(ref.dtype)

    zdk = z[:, _C_DK:_C_DV]
    zdv = z[:, _C_DV:_C_SQ]
    zsk = z[:, _C_SK:_C_SV]
    zsv = z[:, _C_SV:_C_END]
    put(dk32, zdk)
    put(dv32, zdv)
    put(sk32, zsk)
    put(sv32, zsv)
    put(zi32, zi)
    put(dq16, z[:, _C_DQ:_C_DK])
    put(dk16, zdk)
    put(dv16, zdv)
    put(sq16, z[:, _C_SQ:_C_SK])
    put(sk16, zsk)
    put(sv16, zsv)


def _projection(x, sc, sh, g, wm, wih, wil, bt, tt):
    B, T, _ = x.shape
    grid = (B // bt, T // tt)
    row = lambda w: pl.BlockSpec((bt, tt, w), lambda b, t: (b, t, 0))
    per_b = pl.BlockSpec((bt, 1, D_MODEL), lambda b, t: (b, 0, 0))
    full = lambda a: pl.BlockSpec(a.shape, lambda b, t: (0,) * a.ndim,
                                  pipeline_mode=pl.Buffered(1))
    outs = [(DIFF_W, F32), (DIFF_W, F32), (DSA_KV_W, F32), (DSA_KV_W, F32), (IDX_W, F32),
            (2 * DIFF_W, BF16), (DIFF_W, BF16), (DIFF_W, BF16),
            (N_DSA * LANE, BF16), (DSA_KV_W, BF16), (DSA_KV_W, BF16)]
    return pl.pallas_call(
        _proj_kernel,
        grid=grid,
        in_specs=[row(D_MODEL), per_b, per_b, full(g), full(wm), full(wih), full(wil)],
        out_specs=[row(w) for w, _ in outs],
        out_shape=[jax.ShapeDtypeStruct((B, T, w), dt) for w, dt in outs],
        name="in_proj",
        compiler_params=pltpu.CompilerParams(dimension_semantics=("arbitrary", "arbitrary"),
                                             vmem_limit_bytes=VMEM_LIMIT),
    )(x, sc, sh, g, wm, wih, wil)


def _fold8(x, op):
    acc = x[0:SUBLANE]
    for r in range(1, x.shape[0] // SUBLANE):
        acc = op(acc, x[r * SUBLANE:(r + 1) * SUBLANE])
    return acc


def _init_bias_tiles(tab_ref, tb_ref):
    k = lax.broadcasted_iota(I32, (LANE, LANE), 0)
    q = lax.broadcasted_iota(I32, (LANE, LANE), 1)
    buckets = []
    for t in range(N_BIAS_TILES - 1):
        rel = k - q - LANE * t
        n = jnp.abs(rel)
        large = jnp.full_like(n, 8)
        for step in BUCKET_STEPS:
            large = large + jnp.where(n >= step, 1, 0)
        buckets.append(jnp.where(rel > 0, NUM_BUCKETS // 2, 0) + jnp.where(n < 8, n, large))

    def per_head(h, carry):
        for t, bucket in enumerate(buckets):
            val = jnp.zeros((LANE, LANE), F32)
            for b in range(NUM_BUCKETS):
                val = jnp.where(bucket == b, tab_ref[b, h], val)
            tb_ref[h, t] = val
        tb_ref[h, N_BIAS_TILES - 1] = jnp.full((LANE, LANE), tab_ref[FAR_BUCKET, h], F32)
        return carry

    lax.fori_loop(0, N_HEADS, per_head, 0)


def _attn_kernel(tab_ref, lam_ref, gsub_ref, dq_ref, sq_ref, iq_ref, iw_ref,
                 dk_ref, dv_ref, sk_ref, sv_ref, ik_ref, o_ref,
                 tb_ref, key_ref, s_ref, acc_ref, ot_ref, *, tq, q_offset):
    tk = KEY_BLOCK
    b = pl.program_id(0)
    i = pl.program_id(1)

    @pl.when((b == 0) & (i == 0))
    def _():
        _init_bias_tiles(tab_ref, tb_ref)

    qpos0 = q_offset + i * tq
    lim_last = CHUNK * ((qpos0 + tq - 1) // CHUNK + 1)
    nblk = (lim_last + tk - 1) // tk
    qq = lax.broadcasted_iota(I32, (1, tq), 1)
    lim_q = CHUNK * ((qpos0 + qq) // CHUNK + 1)
    kk = lax.broadcasted_iota(I32, (tk, tq), 0)

    def key_rows(jb):
        return pl.ds(pl.multiple_of(jb * tk, tk), tk)

    def over_blocks(body, init):
        def two(jp, carry):
            return body(2 * jp + 1, body(2 * jp, carry))
        carry = lax.fori_loop(0, nblk // 2, two, init)
        return lax.fori_loop(2 * (nblk // 2), nblk, body, carry)

    def bias_tile(head, jb):
        rows = []
        for a in range(tk // LANE):
            parts = []
            for c in range(tq // LANE):
                t = jnp.clip((qpos0 + c * LANE - (jb * tk + a * LANE)) // LANE, 0, N_BIAS_TILES - 1)
                parts.append(tb_ref[head, t])
            rows.append(jnp.concatenate(parts, axis=1))
        return jnp.concatenate(rows, axis=0)

    iw = iw_ref[...]

    def score_block(jb, carry):
        kc = ik_ref[key_rows(jb), :]
        sc = jnp.zeros((tk, tq), F32)
        for h in range(N_IDX):
            d = _dot_t(kc, iq_ref[:, h * 256:(h + 1) * 256])
            sc = sc + jnp.maximum(d, 0.0) * iw[h:h + 1]
        sc = jnp.where(kk + jb * tk < lim_q, sc, NEG)
        bits = pltpu.bitcast(sc, I32)
        key_ref[jb] = bits ^ ((bits >> 31) & 0x7FFFFFFF)
        return carry

    over_blocks(score_block, 0)

    def count(cand, below):
        def body(jb, acc):
            kb = key_ref[jb]
            hit = (kb < cand) if below else (kb >= cand)
            return acc + _fold8(jnp.where(hit, 1.0, 0.0), jnp.add)
        acc = over_blocks(body, jnp.zeros((SUBLANE, tq), F32))
        return jnp.sum(acc, axis=0, keepdims=True)

    topk = float(TOPK)
    zero = jnp.zeros((1, tq), I32)
    c_zero = count(zero, False)
    c_pos = count(zero + 1, False)
    lo = jnp.where(c_pos >= topk, 1, jnp.where(c_zero >= topk, 0, INT_MIN)).astype(I32)
    hi = jnp.where(c_pos >= topk, 2 ** 31 - 1, jnp.where(c_zero >= topk, 1, 0)).astype(I32)
    c_all = (nblk * tk).astype(F32)
    c_lo = jnp.where(c_pos >= topk, c_pos, jnp.where(c_zero >= topk, c_zero, c_all))

    def unsettled(lo, hi, c_lo):
        return jnp.where(c_lo == topk, 0.0, jnp.where(hi - 1 <= lo, 0.0, 1.0))

    def halve(state):
        lo, hi, c_lo, _ = state
        mid = (lo >> 1) + (hi >> 1) + (lo & hi & 1)
        cnt = count(mid, False)
        live = unsettled(lo, hi, c_lo)
        up = jnp.where(cnt >= topk, live, 0.0) > 0.0
        down = jnp.where(cnt >= topk, 0.0, live) > 0.0
        lo = jnp.where(up, mid, lo)
        c_lo = jnp.where(up, cnt, c_lo)
        hi = jnp.where(down, mid, hi)
        return lo, hi, c_lo, jnp.max(unsettled(lo, hi, c_lo))

    thr, _, c_thr, _ = lax.while_loop(lambda st: st[3] > 0, halve,
                                      (lo, hi, c_lo, jnp.max(unsettled(lo, hi, c_lo))))
    tied = jnp.max(jnp.where(c_thr > topk, 1.0, 0.0))

    def select_block(jb, sel):
        sel = jnp.where(kk + jb * tk < lim_q, sel, NEG)
        key_ref[jb] = pltpu.bitcast(sel, I32)

    @pl.when(tied == 0)
    def _():
        def mask_block(jb, carry):
            select_block(jb, jnp.where(key_ref[jb] >= thr, 0.0, NEG))
            return carry
        over_blocks(mask_block, 0)

    @pl.when(tied > 0)
    def _():
        def rank_block(jb, carry):
            kb = key_ref[jb]
            key_ref[jb] = jnp.where(kb > thr, -1, jnp.where(kb == thr, kk + jb * tk, 2 ** 30))
            return carry
        over_blocks(rank_block, 0)

        def cut_bit(it, cut):
            cand = cut + jnp.left_shift(jnp.int32(1), 12 - it)
            return jnp.where(count(cand, True) <= topk, cand, cut)
        cut = lax.fori_loop(0, 13, cut_bit, zero)

        def mask_block(jb, carry):
            select_block(jb, jnp.where(key_ref[jb] < cut, 0.0, NEG))
            return carry
        over_blocks(mask_block, 0)

    def attend(scores, values):
        def pass1(jb, mx):
            out = []
            for n, s in enumerate(scores(jb)):
                s_ref[n, jb] = s
                out.append(jnp.maximum(mx[n], _fold8(s, jnp.maximum)))
            return tuple(out)

        mx = over_blocks(pass1, tuple(jnp.full((SUBLANE, tq), -3e38, F32) for _ in range(N_STAGE)))
        m = [jnp.max(x, axis=0, keepdims=True) for x in mx]
        for n in range(N_STAGE):
            acc_ref[n] = jnp.zeros((V_ROWS, tq), F32)

        def pass2(jb, carry):
            for n in range(N_STAGE):
                p = jnp.exp(s_ref[n, jb] - m[n]).astype(BF16)
                acc_ref[n] += _dot(values(n, jb), p)
            return carry

        over_blocks(pass2, 0)

    for g in range(N_DSA_KV):
        def dsa_scores(jb, g=g):
            kb = sk_ref[key_rows(jb), :]
            sel = pltpu.bitcast(key_ref[jb], F32)
            out = []
            for n in range(DSA_GROUP):
                h = g * DSA_GROUP + n
                s = _dot_t(kb, sq_ref[:, h * LANE:(h + 1) * LANE])
                out.append(s + bias_tile(N_DIFF + h, jb) + sel)
            return out

        attend(dsa_scores, lambda n, jb: sv_ref[jb])
        for n in range(DSA_GROUP):
            h = g * DSA_GROUP + n
            a = acc_ref[n]
            out = a[g * DSA_DH:(g + 1) * DSA_DH] / a[2 * DSA_DH:2 * DSA_DH + 1]
            ot_ref[DIFF_W + h * DSA_DH:DIFF_W + (h + 1) * DSA_DH, :] = out

    lam = lam_ref[...]
    lam_full = (jnp.exp(jnp.sum(lam[0:1] * lam[1:2], axis=1, keepdims=True))
                - jnp.exp(jnp.sum(lam[2:3] * lam[3:4], axis=1, keepdims=True)) + LAM_INIT)
    for pair in range(N_DIFF // 2):
        def diff_scores(jb, pair=pair):
            vis = jnp.where(kk + jb * tk < lim_q, 0.0, NEG)
            out = []
            for hh in range(2):
                h = 2 * pair + hh
                kb = dk_ref[key_rows(jb), h * LANE:(h + 1) * LANE]
                bias = bias_tile(h, jb) + vis
                for part in range(2):
                    slot = 2 * h + part
                    out.append(_dot_t(kb, dq_ref[:, slot * LANE:(slot + 1) * LANE]) + bias)
            return out

        attend(diff_scores, lambda n, jb, pair=pair: dv_ref[2 * pair + n // 2, jb])
        for hh in range(2):
            h = 2 * pair + hh
            a1 = acc_ref[2 * hh]
            a2 = acc_ref[2 * hh + 1]
            od = (a1[0:LANE] / a1[LANE:LANE + 1]
                  - lam_full * (a2[0:LANE] / a2[LANE:LANE + 1]))
            od = od * lax.rsqrt(jnp.mean(od * od, axis=0, keepdims=True) + EPS)
            ot_ref[h * LANE:(h + 1) * LANE, :] = (od * gsub_ref[...]) * (1.0 - LAM_INIT)

    o_ref[...] = ot_ref[...].T.astype(o_ref.dtype)


def _attention(tab, lam4, gsub_t, dq, sq, iq, iw_t, dk, dv_t, sk, sv_t, ik, tq, q_offset):
    B, T, _ = dq.shape
    Lp = dk.shape[1]
    nblk_max = Lp // KEY_BLOCK
    qspec = lambda w: pl.BlockSpec((None, tq, w), lambda b, i: (b, i, 0))
    once = pl.Buffered(1)
    kspec = lambda w: pl.BlockSpec((None, Lp, w), lambda b, i: (b, 0, 0), pipeline_mode=once)
    small = lambda a: pl.BlockSpec(a.shape, lambda b, i: (0,) * a.ndim)
    kern = functools.partial(_attn_kernel, tq=tq, q_offset=q_offset)
    return pl.pallas_call(
        kern,
        grid=(B, T // tq),
        in_specs=[pl.BlockSpec(memory_space=pltpu.SMEM), small(lam4), small(gsub_t),
                  qspec(2 * DIFF_W), qspec(N_DSA * LANE), qspec(N_IDX * 256),
                  pl.BlockSpec((None, SUBLANE, tq), lambda b, i: (b, 0, i)),
                  kspec(DIFF_W),
                  pl.BlockSpec((None, N_DIFF, nblk_max, V_ROWS, KEY_BLOCK),
                               lambda b, i: (b, 0, 0, 0, 0), pipeline_mode=once),
                  kspec(DSA_KV_W),
                  pl.BlockSpec((None, nblk_max, V_ROWS, KEY_BLOCK),
                               lambda b, i: (b, 0, 0, 0), pipeline_mode=once),
                  kspec(256)],
        out_specs=pl.BlockSpec((None, tq, D_MODEL), lambda b, i: (b, i, 0)),
        out_shape=jax.ShapeDtypeStruct((B, T, D_MODEL), BF16),
        name="attn",
        scratch_shapes=[pltpu.VMEM((N_HEADS, N_BIAS_TILES, LANE, LANE), F32),
                        pltpu.VMEM((nblk_max, KEY_BLOCK, tq), I32),
                        pltpu.VMEM((N_STAGE, nblk_max, KEY_BLOCK, tq), F32),
                        pltpu.VMEM((N_STAGE, V_ROWS, tq), F32),
                        pltpu.VMEM((D_MODEL, tq), F32)],
        compiler_params=pltpu.CompilerParams(dimension_semantics=("arbitrary", "arbitrary"),
                                             vmem_limit_bytes=VMEM_LIMIT),
    )(tab, lam4, gsub_t, dq, sq, iq, iw_t, dk, dv_t, sk, sv_t, ik)


FF_CHUNK = D_FF // 2


def _ffn_kernel(x_ref, o_ref, ga1_ref, sc2_ref, sh2_ref, ga2_ref, gpm_ref, gpf_ref, gpo_ref,
                wo_ref, wgu_ref, wd_ref, y_ref):
    bt, tt, _ = x_ref.shape
    rows = bt * tt
    m = _dot(o_ref[...].reshape(rows, D_MODEL), wo_ref[...]).reshape(bt, tt, D_MODEL)
    x1 = x_ref[...] + ga1_ref[...] * (_rms(m) * gpm_ref[...])
    h2 = (_rms(x1) * gpf_ref[...]) * (1.0 + sc2_ref[...]) + sh2_ref[...]
    h2 = h2.reshape(rows, D_MODEL).astype(BF16)
    f = jnp.zeros((rows, D_MODEL), F32)
    for c in range(D_FF // FF_CHUNK):
        g = _dot(h2, wgu_ref[:, c * FF_CHUNK:(c + 1) * FF_CHUNK])
        u = _dot(h2, wgu_ref[:, D_FF + c * FF_CHUNK:D_FF + (c + 1) * FF_CHUNK])
        a = (g * jax.nn.sigmoid(g) * u).astype(BF16)
        f = f + _dot(a, wd_ref[c * FF_CHUNK:(c + 1) * FF_CHUNK, :])
    f = f.reshape(bt, tt, D_MODEL)
    y_ref[...] = x1 + ga2_ref[...] * (_rms(f) * gpo_ref[...])


def _out_ffn(x, o, ga1, sc2, sh2, ga2, gpm, gpf, gpo, wo, wgu, wd, bt, tt):
    B, T, _ = x.shape
    row = pl.BlockSpec((bt, tt, D_MODEL), lambda b, t: (b, t, 0))
    per_b = pl.BlockSpec((bt, 1, D_MODEL), lambda b, t: (b, 0, 0))
    full = lambda a: pl.BlockSpec(a.shape, lambda b, t: (0,) * a.ndim,
                                  pipeline_mode=pl.Buffered(1))
    return pl.pallas_call(
        _ffn_kernel,
        grid=(B // bt, T // tt),
        in_specs=[row, row, per_b, per_b, per_b, per_b, full(gpm), full(gpf), full(gpo),
                  full(wo), full(wgu), full(wd)],
        out_specs=row,
        out_shape=jax.ShapeDtypeStruct((B, T, D_MODEL), F32),
        name="out_ffn",
        compiler_params=pltpu.CompilerParams(dimension_semantics=("arbitrary", "arbitrary"),
                                             vmem_limit_bytes=VMEM_LIMIT),
    )(x, o, ga1, sc2, sh2, ga2, gpm, gpf, gpo, wo, wgu, wd)


def _split_hi_lo(a):
    hi = a.astype(BF16)
    lo = (a - hi.astype(F32)).astype(BF16)
    return hi, lo


def _prep_in_weights(w_in):
    cuts = [0, 512, 1024, 1536, 2048, 2176, 2304, 2560, 2624, 2628]
    wdq, wdk, wdv, wsq, wsk, wsv, wiq, wik, wiw = [w_in[:, a:b] for a, b in zip(cuts[:-1], cuts[1:])]
    zeros64 = jnp.zeros((D_MODEL, DIFF_DH), F32)
    scale = DIFF_DH ** -0.5
    dq_cols = []
    for h in range(N_DIFF):
        q1 = wdq[:, h * 128:h * 128 + 64] * scale
        q2 = wdq[:, h * 128 + 64:(h + 1) * 128] * scale
        dq_cols += [q1, zeros64, zeros64, q2]
    sq_cols = []
    for h in range(N_DSA):
        q = wsq[:, h * 64:(h + 1) * 64] * scale
        sq_cols += [q, zeros64] if h < DSA_GROUP else [zeros64, q]
    wm = jnp.concatenate(dq_cols + [wdk, wdv] + sq_cols + [wsk, wsv], axis=1).astype(BF16)
    wi = jnp.concatenate([wiq, wik, wiw, jnp.zeros((D_MODEL, IDX_W - 324), F32)], axis=1)
    wih, wil = _split_hi_lo(wi)
    return wm, wih, wil


def _idx_operands(iq, ik):
    qh, ql = _split_hi_lo(iq)
    kh, kl = _split_hi_lo(ik)
    zq = jnp.zeros(qh.shape[:-1] + (IDX_DH,), BF16)
    parts = []
    for h in range(N_IDX):
        s = slice(h * IDX_DH, (h + 1) * IDX_DH)
        parts += [qh[..., s], qh[..., s], ql[..., s], zq]
    zk = jnp.zeros(kh.shape[:-1] + (IDX_DH,), BF16)
    return jnp.concatenate(parts, axis=-1), jnp.concatenate([kh, kl, kh, zk], axis=-1)


def _pad_axis1(a, n):
    return jnp.pad(a, ((0, 0), (0, n - a.shape[1])) + ((0, 0),) * (a.ndim - 2))


def _value_operand(v):
    B, L, H, W = v.shape
    vt = jnp.transpose(v, (0, 2, 3, 1))
    extra = jnp.concatenate([jnp.ones((B, H, 1, L), BF16),
                             jnp.zeros((B, H, V_ROWS - W - 1, L), BF16)], axis=2)
    vt = jnp.concatenate([vt, extra], axis=2).reshape(B, H, V_ROWS, L // KEY_BLOCK, KEY_BLOCK)
    return jnp.transpose(vt, (0, 1, 3, 2, 4))


def _group(x, mods, past, params, bt, tt, tq):
    (g_pre_mix, g_post_mix, g_pre_ffn, g_post_ffn, wm, wih, wil, lam4, g_subln,
     wo, wgu, wd, tab) = params
    B, T, _ = x.shape
    sh1, sc1, ga1, sh2, sc2, ga2 = mods
    (dk32, dv32, sk32, sv32, zi32, dq16, dk16, dv16, sq16, sk16, sv16) = _projection(
        x, sc1, sh1, g_pre_mix, wm, wih, wil, bt, tt)
    ik32 = zi32[..., 256:256 + IDX_DH]
    iw = zi32[..., 256 + IDX_DH:256 + IDX_DH + N_IDX] * (1.0 / 16.0)
    if past is None:
        q_offset = 0
        lp = T
        iqc, ikc = _idx_operands(zi32[..., :256], ik32)
        dk, dv, sk, sv = dk16, dv16, sk16, sv16
    else:
        pk, pv, psk, psv, pik = past
        P = pk.shape[1]
        q_offset = P
        lp = -(-(P + T) // KEY_BLOCK) * KEY_BLOCK
        iqc, ikc = _idx_operands(zi32[..., :256], jnp.concatenate([pik, ik32], axis=1))
        cat = lambda p, n: _pad_axis1(
            jnp.concatenate([p.reshape(B, P, -1).astype(BF16), n], axis=1), lp)
        dk, dv, sk, sv = cat(pk, dk16), cat(pv, dv16), cat(psk, sk16), cat(psv, sv16)
        ikc = _pad_axis1(ikc, lp)
    tpad = -(-T // tq) * tq
    qpad = lambda a: _pad_axis1(a, tpad)
    iw_t = _pad_axis1(jnp.transpose(qpad(iw), (0, 2, 1)), SUBLANE)
    dv_t = _value_operand(dv.reshape(B, lp, N_DIFF, 2 * DIFF_DH))
    sv_t = _value_operand(sv.reshape(B, lp, 1, DSA_KV_W))[:, 0]
    gsub_t = jnp.broadcast_to(g_subln.reshape(-1, 1), (2 * DIFF_DH, tq))
    o = _attention(tab, lam4, gsub_t, qpad(dq16), qpad(sq16), qpad(iqc), iw_t,
                   dk, dv_t, sk, sv_t, ikc, tq, q_offset)[:, :T]
    y = _out_ffn(x, o, ga1, sc2, sh2, ga2, g_post_mix, g_pre_ffn, g_post_ffn, wo, wgu, wd, bt, tt)
    rows = (dk32.reshape(1, B, T, N_DIFF, 2 * DIFF_DH), dv32.reshape(1, B, T, N_DIFF, 2 * DIFF_DH),
            sk32.reshape(1, B, T, N_DSA_KV, DSA_DH), sv32.reshape(1, B, T, N_DSA_KV, DSA_DH),
            ik32.reshape(1, B, T, IDX_DH))
    return y, rows


def kernel(x_prompt, x_sample, cache_diff_k, cache_diff_v, cache_dsa_k, cache_dsa_v, cache_dsa_kidx,
           c_prompt, c_sample, w_ada, b_ada, g_pre_mix, g_post_mix, g_pre_ffn, g_post_ffn, w_in,
           lambda_q1, lambda_k1, lambda_q2, lambda_k2, g_subln, w_out, w_gate_up, w_down,
           rel_bias_table):
    Bp = x_prompt.shape[0]
    c_all = jnp.concatenate([c_prompt, c_sample], axis=0)
    mod = _modulation(c_all, w_ada[0], b_ada[0])
    mods = [mod[:, j * D_MODEL:(j + 1) * D_MODEL][:, None, :] for j in range(6)]
    mods_p = [m[:Bp] for m in mods]
    mods_s = [m[Bp:] for m in mods]

    wm, wih, wil = _prep_in_weights(w_in[0])
    lam4 = jnp.concatenate([lambda_q1, lambda_k1, lambda_q2, lambda_k2], axis=0)
    row = lambda g: g[0].reshape(1, -1)
    params = (row(g_pre_mix), row(g_post_mix), row(g_pre_ffn), row(g_post_ffn), wm, wih, wil,
              lam4, g_subln[0], w_out[0].astype(BF16), w_gate_up[0].astype(BF16),
              w_down[0].astype(BF16), rel_bias_table)

    yp, rp = _group(x_prompt, mods_p, None, params, bt=1, tt=512, tq=256)
    past = (cache_diff_k[0], cache_diff_v[0], cache_dsa_k[0], cache_dsa_v[0], cache_dsa_kidx[0])
    ys, rs = _group(x_sample, mods_s, past, params, bt=8, tt=64, tq=128)
    return (yp, ys) + rp + rs
```

```python
import functools
import math

import jax
import jax.numpy as jnp
from jax import lax
from jax.experimental import pallas as pl
from jax.experimental.pallas import tpu as pltpu

F32 = jnp.float32
BF16 = jnp.bfloat16
I32 = jnp.int32

D_MODEL = 1024
CHUNK = 64
N_DIFF = 4
DIFF_DH = 64
DIFF_W = N_DIFF * 2 * DIFF_DH
N_DSA = 8
N_DSA_KV = 2
DSA_GROUP = N_DSA // N_DSA_KV
DSA_DH = 64
DSA_W = N_DSA * DSA_DH
DSA_KV_W = N_DSA_KV * DSA_DH
N_IDX = 4
IDX_DH = 64
TOPK = 256
N_HEADS = N_DIFF + N_DSA
NUM_BUCKETS = 32
D_FF = 2816
EPS = 1e-6
NEG = -1e30
LAM_INIT = 0.8 - 0.6 * math.exp(-0.3 * 0)

LANE = 128
SUBLANE = 8
KEY_BLOCK = 256
V_ROWS = 144
VMEM_LIMIT = 56 * 1024 * 1024
INT_MIN = -2 ** 31
FAR_BUCKET = 15
BUCKET_STEPS = (13, 20, 30, 46, 70, 108, 166)
N_BIAS_TILES = 4
N_STAGE = 2


def _rms(x):
    return x * lax.rsqrt(jnp.mean(x * x, axis=-1, keepdims=True) + EPS)


def _dot(a, b):
    return jnp.dot(a, b, preferred_element_type=F32)


def _dot_t(a, b):
    return lax.dot_general(a, b, (((1,), (1,)), ((), ())), preferred_element_type=F32)


def _mod_kernel(c_ref, w_ref, b_ref, o_ref):
    c = c_ref[...]
    s = c * jax.nn.sigmoid(c)
    o_ref[...] = jnp.dot(s, w_ref[...], preferred_element_type=F32,
                         precision=lax.Precision.HIGHEST) + b_ref[...]


def _modulation(c_all, w_ada, b_ada):
    n = c_all.shape[0]
    tn = 1024
    return pl.pallas_call(
        _mod_kernel,
        grid=(6 * D_MODEL // tn,),
        in_specs=[pl.BlockSpec((n, D_MODEL), lambda j: (0, 0)),
                  pl.BlockSpec((D_MODEL, tn), lambda j: (0, j)),
                  pl.BlockSpec((1, tn), lambda j: (0, j))],
        out_specs=pl.BlockSpec((n, tn), lambda j: (0, j)),
        out_shape=jax.ShapeDtypeStruct((n, 6 * D_MODEL), F32),
        name="adaln_mod",
        compiler_params=pltpu.CompilerParams(dimension_semantics=("arbitrary",),
                                             vmem_limit_bytes=VMEM_LIMIT),
    )(c_all, w_ada, b_ada.reshape(1, -1))


_C_DQ = 0
_C_DK = _C_DQ + 2 * DIFF_W
_C_DV = _C_DK + DIFF_W
_C_SQ = _C_DV + DIFF_W
_C_SK = _C_SQ + N_DSA * LANE
_C_SV = _C_SK + DSA_KV_W
_C_END = _C_SV + DSA_KV_W
IDX_W = 384


def _proj_kernel(x_ref, sc_ref, sh_ref, g_ref, wm_ref, wih_ref, wil_ref,
                 dk32, dv32, sk32, sv32, zi32, dq16, dk16, dv16, sq16, sk16, sv16):
    bt, tt, _ = x_ref.shape
    x = x_ref[...]
    h = (_rms(x) * g_ref[...]) * (1.0 + sc_ref[...]) + sh_ref[...]
    h = h.reshape(bt * tt, D_MODEL)
    hb = h.astype(BF16)
    hl = (h - hb.astype(F32)).astype(BF16)
    z = _dot(hb, wm_ref[...])
    wih = wih_ref[...]
    zi = _dot(hb, wih) + _dot(hb, wil_ref[...]) + _dot(hl, wih)

    def put(ref, v):
        ref[...] = v.reshape(bt, tt, v.shape[-1]).astype(ref.dtype)

    zdk = z[:, _C_DK:_C_DV]
    zdv = z[:, _C_DV:_C_SQ]
    zsk = z[:, _C_SK:_C_SV]
    zsv = z[:, _C_SV:_C_END]
    put(dk32, zdk)
    put(dv32, zdv)
    put(sk32, zsk)
    put(sv32, zsv)
    put(zi32, zi)
    put(dq16, z[:, _C_DQ:_C_DK])
    put(dk16, zdk)
    put(dv16, zdv)
    put(sq16, z[:, _C_SQ:_C_SK])
    put(sk16, zsk)
    put(sv16, zsv)


def _projection(x, sc, sh, g, wm, wih, wil, bt, tt):
    B, T, _ = x.shape
    grid = (B // bt, T // tt)
    row = lambda w: pl.BlockSpec((bt, tt, w), lambda b, t: (b, t, 0))
    per_b = pl.BlockSpec((bt, 1, D_MODEL), lambda b, t: (b, 0, 0))
    full = lambda a: pl.BlockSpec(a.shape, lambda b, t: (0,) * a.ndim,
                                  pipeline_mode=pl.Buffered(1))
    outs = [(DIFF_W, F32), (DIFF_W, F32), (DSA_KV_W, F32), (DSA_KV_W, F32), (IDX_W, F32),
            (2 * DIFF_W, BF16), (DIFF_W, BF16), (DIFF_W, BF16),
            (N_DSA * LANE, BF16), (DSA_KV_W, BF16), (DSA_KV_W, BF16)]
    return pl.pallas_call(
        _proj_kernel,
        grid=grid,
        in_specs=[row(D_MODEL), per_b, per_b, full(g), full(wm), full(wih), full(wil)],
        out_specs=[row(w) for w, _ in outs],
        out_shape=[jax.ShapeDtypeStruct((B, T, w), dt) for w, dt in outs],
        name="in_proj",
        compiler_params=pltpu.CompilerParams(dimension_semantics=("arbitrary", "arbitrary"),
                                             vmem_limit_bytes=VMEM_LIMIT),
    )(x, sc, sh, g, wm, wih, wil)


def _fold8(x, op):
    acc = x[0:SUBLANE]
    for r in range(1, x.shape[0] // SUBLANE):
        acc = op(acc, x[r * SUBLANE:(r + 1) * SUBLANE])
    return acc


def _init_bias_tiles(tab_ref, tb_ref):
    k = lax.broadcasted_iota(I32, (LANE, LANE), 0)
    q = lax.broadcasted_iota(I32, (LANE, LANE), 1)
    buckets = []
    for t in range(N_BIAS_TILES - 1):
        rel = k - q - LANE * t
        n = jnp.abs(rel)
        large = jnp.full_like(n, 8)
        for step in BUCKET_STEPS:
            large = large + jnp.where(n >= step, 1, 0)
        buckets.append(jnp.where(rel > 0, NUM_BUCKETS // 2, 0) + jnp.where(n < 8, n, large))

    def per_head(h, carry):
        for t, bucket in enumerate(buckets):
            val = jnp.zeros((LANE, LANE), F32)
            for b in range(NUM_BUCKETS):
                val = jnp.where(bucket == b, tab_ref[b, h], val)
            tb_ref[h, t] = val
        tb_ref[h, N_BIAS_TILES - 1] = jnp.full((LANE, LANE), tab_ref[FAR_BUCKET, h], F32)
        return carry

    lax.fori_loop(0, N_HEADS, per_head, 0)


def _attn_kernel(tab_ref, lam_ref, gsub_ref, dq_ref, sq_ref, iq_ref, iw_ref,
                 dk_ref, dv_ref, sk_ref, sv_ref, ik_ref, o_ref,
                 tb_ref, key_ref, s_ref, acc_ref, ot_ref, *, tq, q_offset):
    tk = KEY_BLOCK
    b = pl.program_id(0)
    i = pl.program_id(1)

    @pl.when((b == 0) & (i == 0))
    def _():
        _init_bias_tiles(tab_ref, tb_ref)

    qpos0 = q_offset + i * tq
    lim_last = CHUNK * ((qpos0 + tq - 1) // CHUNK + 1)
    nblk = (lim_last + tk - 1) // tk
    qq = lax.broadcasted_iota(I32, (1, tq), 1)
    lim_q = CHUNK * ((qpos0 + qq) // CHUNK + 1)
    kk = lax.broadcasted_iota(I32, (tk, tq), 0)

    def key_rows(jb):
        return pl.ds(pl.multiple_of(jb * tk, tk), tk)

    def over_blocks(body, init, unroll=2):
        def run(first, n, carry):
            for u in range(n):
                carry = body(first + u, carry)
            return carry
        carry = lax.fori_loop(0, nblk // unroll, lambda jp, c: run(unroll * jp, unroll, c), init)
        n = unroll // 2
        while n >= 1:
            first = (nblk // (2 * n)) * (2 * n)
            carry = lax.fori_loop(0, (nblk // n) % 2, lambda _, c, first=first, n=n: run(first, n, c),
                                  carry)
            n //= 2
        return carry

    def bias_tile(head, jb):
        rows = []
        for a in range(tk // LANE):
            parts = []
            for c in range(tq // LANE):
                t = jnp.clip((qpos0 + c * LANE - (jb * tk + a * LANE)) // LANE, 0, N_BIAS_TILES - 1)
                parts.append(tb_ref[head, t])
            rows.append(jnp.concatenate(parts, axis=1))
        return jnp.concatenate(rows, axis=0)

    iw = iw_ref[...]

    def score_block(jb, carry):
        kc = ik_ref[key_rows(jb), :]
        sc = jnp.zeros((tk, tq), F32)
        for h in range(N_IDX):
            d = _dot_t(kc, iq_ref[:, h * 256:(h + 1) * 256])
            sc = sc + jnp.maximum(d, 0.0) * iw[h:h + 1]
        sc = jnp.where(kk + jb * tk < lim_q, sc, NEG)
        bits = pltpu.bitcast(sc, I32)
        key_ref[jb] = bits ^ ((bits >> 31) & 0x7FFFFFFF)
        return carry

    over_blocks(score_block, 0)

    def count(cand, below):
        def body(jb, acc):
            kb = key_ref[jb]
            hit = (kb < cand) if below else (kb >= cand)
            return acc + _fold8(jnp.where(hit, 1.0, 0.0), jnp.add)
        acc = over_blocks(body, jnp.zeros((SUBLANE, tq), F32))
        return jnp.sum(acc, axis=0, keepdims=True)

    topk = float(TOPK)
    zero = jnp.zeros((1, tq), I32)
    c_zero = count(zero, False)
    c_pos = count(zero + 1, False)
    lo = jnp.where(c_pos >= topk, 1, jnp.where(c_zero >= topk, 0, INT_MIN)).astype(I32)
    hi = jnp.where(c_pos >= topk, 2 ** 31 - 1, jnp.where(c_zero >= topk, 1, 0)).astype(I32)
    c_all = (nblk * tk).astype(F32)
    c_lo = jnp.where(c_pos >= topk, c_pos, jnp.where(c_zero >= topk, c_zero, c_all))

    def unsettled(lo, hi, c_lo):
        return jnp.where(c_lo == topk, 0.0, jnp.where(hi - 1 <= lo, 0.0, 1.0))

    def halve(state):
        lo, hi, c_lo, _ = state
        mid = (lo >> 1) + (hi >> 1) + (lo & hi & 1)
        cnt = count(mid, False)
        live = unsettled(lo, hi, c_lo)
        up = jnp.where(cnt >= topk, live, 0.0) > 0.0
        down = jnp.where(cnt >= topk, 0.0, live) > 0.0
        lo = jnp.where(up, mid, lo)
        c_lo = jnp.where(up, cnt, c_lo)
        hi = jnp.where(down, mid, hi)
        return lo, hi, c_lo, jnp.max(unsettled(lo, hi, c_lo))

    thr, _, c_thr, _ = lax.while_loop(lambda st: st[3] > 0, halve,
                                      (lo, hi, c_lo, jnp.max(unsettled(lo, hi, c_lo))))
    tied = jnp.max(jnp.where(c_thr > topk, 1.0, 0.0))

    def select_block(jb, sel):
        sel = jnp.where(kk + jb * tk < lim_q, sel, NEG)
        key_ref[jb] = pltpu.bitcast(sel, I32)

    @pl.when(tied == 0)
    def _():
        def mask_block(jb, carry):
            select_block(jb, jnp.where(key_ref[jb] >= thr, 0.0, NEG))
            return carry
        over_blocks(mask_block, 0)

    @pl.when(tied > 0)
    def _():
        def rank_block(jb, carry):
            kb = key_ref[jb]
            key_ref[jb] = jnp.where(kb > thr, -1, jnp.where(kb == thr, kk + jb * tk, 2 ** 30))
            return carry
        over_blocks(rank_block, 0)

        def cut_bit(it, cut):
            cand = cut + jnp.left_shift(jnp.int32(1), 12 - it)
            return jnp.where(count(cand, True) <= topk, cand, cut)
        cut = lax.fori_loop(0, 13, cut_bit, zero)

        def mask_block(jb, carry):
            select_block(jb, jnp.where(key_ref[jb] < cut, 0.0, NEG))
            return carry
        over_blocks(mask_block, 0)

    lam = lam_ref[...]
    lam_full = (jnp.exp(jnp.sum(lam[0:1] * lam[1:2], axis=1, keepdims=True))
                - jnp.exp(jnp.sum(lam[2:3] * lam[3:4], axis=1, keepdims=True)) + LAM_INIT)

    def dsa_group(pair):
        heads = [N_STAGE * pair + n for n in range(N_STAGE)]

        def scores(jb):
            kb = sk_ref[key_rows(jb), :]
            sel = pltpu.bitcast(key_ref[jb], F32)
            return [_dot_t(kb, sq_ref[:, h * LANE:(h + 1) * LANE]) + bias_tile(N_DIFF + h, jb) + sel
                    for h in heads]

        def finish():
            for n, h in enumerate(heads):
                g = h // DSA_GROUP
                a = acc_ref[n]
                out = a[g * DSA_DH:(g + 1) * DSA_DH] / a[2 * DSA_DH:2 * DSA_DH + 1]
                ot_ref[DIFF_W + h * DSA_DH:DIFF_W + (h + 1) * DSA_DH, :] = out

        return scores, (lambda n, jb: sv_ref[jb]), finish

    def diff_group(h):
        def scores(jb):
            kb = dk_ref[key_rows(jb), h * LANE:(h + 1) * LANE]
            bias = bias_tile(h, jb) + jnp.where(kk + jb * tk < lim_q, 0.0, NEG)
            return [_dot_t(kb, dq_ref[:, (2 * h + part) * LANE:(2 * h + part + 1) * LANE]) + bias
                    for part in range(2)]

        def finish():
            a1 = acc_ref[0]
            a2 = acc_ref[1]
            od = (a1[0:LANE] / a1[LANE:LANE + 1]
                  - lam_full * (a2[0:LANE] / a2[LANE:LANE + 1]))
            od = od * lax.rsqrt(jnp.mean(od * od, axis=0, keepdims=True) + EPS)
            ot_ref[h * LANE:(h + 1) * LANE, :] = (od * gsub_ref[...]) * (1.0 - LAM_INIT)

        return scores, (lambda n, jb: dv_ref[h, jb]), finish

    groups = ([dsa_group(p) for p in range(N_DSA // N_STAGE)]
              + [diff_group(h) for h in range(N_DIFF)])
    m_prev = None
    for k in range(len(groups) + 1):
        produce = groups[k] if k < len(groups) else None
        consume = groups[k - 1] if k > 0 else None

        def step(jb, mx, k=k, produce=produce, consume=consume, m_prev=m_prev):
            if consume is not None:
                for n in range(N_STAGE):
                    p = jnp.exp(s_ref[(k - 1) % 2, n, jb] - m_prev[n]).astype(BF16)
                    acc_ref[n] += _dot(consume[1](n, jb), p)
            if produce is None:
                return mx
            out = []
            for n, s in enumerate(produce[0](jb)):
                s_ref[k % 2, n, jb] = s
                out.append(jnp.maximum(mx[n], _fold8(s, jnp.maximum)))
            return tuple(out)

        if consume is not None:
            for n in range(N_STAGE):
                acc_ref[n] = jnp.zeros((V_ROWS, tq), F32)
        mx = over_blocks(step, tuple(jnp.full((SUBLANE, tq), -3e38, F32) for _ in range(N_STAGE)),
                         unroll=4)
        if consume is not None:
            consume[2]()
        m_prev = [jnp.max(x, axis=0, keepdims=True) for x in mx]

    o_ref[...] = ot_ref[...].T.astype(o_ref.dtype)


def _attention(tab, lam4, gsub_t, dq, sq, iq, iw_t, dk, dv_t, sk, sv_t, ik, tq, q_offset):
    B, T, _ = dq.shape
    Lp = dk.shape[1]
    nblk_max = Lp // KEY_BLOCK
    qspec = lambda w: pl.BlockSpec((None, tq, w), lambda b, i: (b, i, 0))
    once = pl.Buffered(1)
    kspec = lambda w: pl.BlockSpec((None, Lp, w), lambda b, i: (b, 0, 0), pipeline_mode=once)
    small = lambda a: pl.BlockSpec(a.shape, lambda b, i: (0,) * a.ndim)
    kern = functools.partial(_attn_kernel, tq=tq, q_offset=q_offset)
    return pl.pallas_call(
        kern,
        grid=(B, T // tq),
        in_specs=[pl.BlockSpec(memory_space=pltpu.SMEM), small(lam4), small(gsub_t),
                  qspec(2 * DIFF_W), qspec(N_DSA * LANE), qspec(N_IDX * 256),
                  pl.BlockSpec((None, SUBLANE, tq), lambda b, i: (b, 0, i)),
                  kspec(DIFF_W),
                  pl.BlockSpec((None, N_DIFF, nblk_max, V_ROWS, KEY_BLOCK),
                               lambda b, i: (b, 0, 0, 0, 0), pipeline_mode=once),
                  kspec(DSA_KV_W),
                  pl.BlockSpec((None, nblk_max, V_ROWS, KEY_BLOCK),
                               lambda b, i: (b, 0, 0, 0), pipeline_mode=once),
                  kspec(256)],
        out_specs=pl.BlockSpec((None, tq, D_MODEL), lambda b, i: (b, i, 0)),
        out_shape=jax.ShapeDtypeStruct((B, T, D_MODEL), BF16),
        name="attn",
        scratch_shapes=[pltpu.VMEM((N_HEADS, N_BIAS_TILES, LANE, LANE), F32),
                        pltpu.VMEM((nblk_max, KEY_BLOCK, tq), I32),
                        pltpu.VMEM((2, N_STAGE, nblk_max, KEY_BLOCK, tq), F32),
                        pltpu.VMEM((N_STAGE, V_ROWS, tq), F32),
                        pltpu.VMEM((D_MODEL, tq), F32)],
        compiler_params=pltpu.CompilerParams(dimension_semantics=("arbitrary", "arbitrary"),
                                             vmem_limit_bytes=VMEM_LIMIT),
    )(tab, lam4, gsub_t, dq, sq, iq, iw_t, dk, dv_t, sk, sv_t, ik)


FF_CHUNK = D_FF // 2


def _ffn_kernel(x_ref, o_ref, ga1_ref, sc2_ref, sh2_ref, ga2_ref, gpm_ref, gpf_ref, gpo_ref,
                wo_ref, wgu_ref, wd_ref, y_ref):
    bt, tt, _ = x_ref.shape
    rows = bt * tt
    m = _dot(o_ref[...].reshape(rows, D_MODEL), wo_ref[...]).reshape(bt, tt, D_MODEL)
    x1 = x_ref[...] + ga1_ref[...] * (_rms(m) * gpm_ref[...])
    h2 = (_rms(x1) * gpf_ref[...]) * (1.0 + sc2_ref[...]) + sh2_ref[...]
    h2 = h2.reshape(rows, D_MODEL).astype(BF16)
    f = jnp.zeros((rows, D_MODEL), F32)
    for c in range(D_FF // FF_CHUNK):
        g = _dot(h2, wgu_ref[:, c * FF_CHUNK:(c + 1) * FF_CHUNK])
        u = _dot(h2, wgu_ref[:, D_FF + c * FF_CHUNK:D_FF + (c + 1) * FF_CHUNK])
        a = (g * jax.nn.sigmoid(g) * u).astype(BF16)
        f = f + _dot(a, wd_ref[c * FF_CHUNK:(c + 1) * FF_CHUNK, :])
    f = f.reshape(bt, tt, D_MODEL)
    y_ref[...] = x1 + ga2_ref[...] * (_rms(f) * gpo_ref[...])


def _out_ffn(x, o, ga1, sc2, sh2, ga2, gpm, gpf, gpo, wo, wgu, wd, bt, tt):
    B, T, _ = x.shape
    row = pl.BlockSpec((bt, tt, D_MODEL), lambda b, t: (b, t, 0))
    per_b = pl.BlockSpec((bt, 1, D_MODEL), lambda b, t: (b, 0, 0))
    full = lambda a: pl.BlockSpec(a.shape, lambda b, t: (0,) * a.ndim,
                                  pipeline_mode=pl.Buffered(1))
    return pl.pallas_call(
        _ffn_kernel,
        grid=(B // bt, T // tt),
        in_specs=[row, row, per_b, per_b, per_b, per_b, full(gpm), full(gpf), full(gpo),
                  full(wo), full(wgu), full(wd)],
        out_specs=row,
        out_shape=jax.ShapeDtypeStruct((B, T, D_MODEL), F32),
        name="out_ffn",
        compiler_params=pltpu.CompilerParams(dimension_semantics=("arbitrary", "arbitrary"),
                                             vmem_limit_bytes=VMEM_LIMIT),
    )(x, o, ga1, sc2, sh2, ga2, gpm, gpf, gpo, wo, wgu, wd)


def _split_hi_lo(a):
    hi = a.astype(BF16)
    lo = (a - hi.astype(F32)).astype(BF16)
    return hi, lo


def _prep_in_weights(w_in):
    cuts = [0, 512, 1024, 1536, 2048, 2176, 2304, 2560, 2624, 2628]
    wdq, wdk, wdv, wsq, wsk, wsv, wiq, wik, wiw = [w_in[:, a:b] for a, b in zip(cuts[:-1], cuts[1:])]
    zeros64 = jnp.zeros((D_MODEL, DIFF_DH), F32)
    scale = DIFF_DH ** -0.5
    dq_cols = []
    for h in range(N_DIFF):
        q1 = wdq[:, h * 128:h * 128 + 64] * scale
        q2 = wdq[:, h * 128 + 64:(h + 1) * 128] * scale
        dq_cols += [q1, zeros64, zeros64, q2]
    sq_cols = []
    for h in range(N_DSA):
        q = wsq[:, h * 64:(h + 1) * 64] * scale
        sq_cols += [q, zeros64] if h < DSA_GROUP else [zeros64, q]
    wm = jnp.concatenate(dq_cols + [wdk, wdv] + sq_cols + [wsk, wsv], axis=1).astype(BF16)
    wi = jnp.concatenate([wiq, wik, wiw, jnp.zeros((D_MODEL, IDX_W - 324), F32)], axis=1)
    wih, wil = _split_hi_lo(wi)
    return wm, wih, wil


def _idx_operands(iq, ik):
    qh, ql = _split_hi_lo(iq)
    kh, kl = _split_hi_lo(ik)
    zq = jnp.zeros(qh.shape[:-1] + (IDX_DH,), BF16)
    parts = []
    for h in range(N_IDX):
        s = slice(h * IDX_DH, (h + 1) * IDX_DH)
        parts += [qh[..., s], qh[..., s], ql[..., s], zq]
    zk = jnp.zeros(kh.shape[:-1] + (IDX_DH,), BF16)
    return jnp.concatenate(parts, axis=-1), jnp.concatenate([kh, kl, kh, zk], axis=-1)


def _pad_axis1(a, n):
    return jnp.pad(a, ((0, 0), (0, n - a.shape[1])) + ((0, 0),) * (a.ndim - 2))


def _value_operand(v):
    B, L, H, W = v.shape
    vt = jnp.transpose(v, (0, 2, 3, 1))
    extra = jnp.concatenate([jnp.ones((B, H, 1, L), BF16),
                             jnp.zeros((B, H, V_ROWS - W - 1, L), BF16)], axis=2)
    vt = jnp.concatenate([vt, extra], axis=2).reshape(B, H, V_ROWS, L // KEY_BLOCK, KEY_BLOCK)
    return jnp.transpose(vt, (0, 1, 3, 2, 4))


def _group(x, mods, past, params, bt, tt, tq):
    (g_pre_mix, g_post_mix, g_pre_ffn, g_post_ffn, wm, wih, wil, lam4, g_subln,
     wo, wgu, wd, tab) = params
    B, T, _ = x.shape
    sh1, sc1, ga1, sh2, sc2, ga2 = mods
    (dk32, dv32, sk32, sv32, zi32, dq16, dk16, dv16, sq16, sk16, sv16) = _projection(
        x, sc1, sh1, g_pre_mix, wm, wih, wil, bt, tt)
    ik32 = zi32[..., 256:256 + IDX_DH]
    iw = zi32[..., 256 + IDX_DH:256 + IDX_DH + N_IDX] * (1.0 / 16.0)
    if past is None:
        q_offset = 0
        lp = T
        iqc, ikc = _idx_operands(zi32[..., :256], ik32)
        dk, dv, sk, sv = dk16, dv16, sk16, sv16
    else:
        pk, pv, psk, psv, pik = past
        P = pk.shape[1]
        q_offset = P
        lp = -(-(P + T) // KEY_BLOCK) * KEY_BLOCK
        iqc, ikc = _idx_operands(zi32[..., :256], jnp.concatenate([pik, ik32], axis=1))
        cat = lambda p, n: _pad_axis1(
            jnp.concatenate([p.reshape(B, P, -1).astype(BF16), n], axis=1), lp)
        dk, dv, sk, sv = cat(pk, dk16), cat(pv, dv16), cat(psk, sk16), cat(psv, sv16)
        ikc = _pad_axis1(ikc, lp)
    tpad = -(-T // tq) * tq
    qpad = lambda a: _pad_axis1(a, tpad)
    iw_t = _pad_axis1(jnp.transpose(qpad(iw), (0, 2, 1)), SUBLANE)
    dv_t = _value_operand(dv.reshape(B, lp, N_DIFF, 2 * DIFF_DH))
    sv_t = _value_operand(sv.reshape(B, lp, 1, DSA_KV_W))[:, 0]
    gsub_t = jnp.broadcast_to(g_subln.reshape(-1, 1), (2 * DIFF_DH, tq))
    o = _attention(tab, lam4, gsub_t, qpad(dq16), qpad(sq16), qpad(iqc), iw_t,
                   dk, dv_t, sk, sv_t, ikc, tq, q_offset)[:, :T]
    y = _out_ffn(x, o, ga1, sc2, sh2, ga2, g_post_mix, g_pre_ffn, g_post_ffn, wo, wgu, wd, bt, tt)
    rows = (dk32.reshape(1, B, T, N_DIFF, 2 * DIFF_DH), dv32.reshape(1, B, T, N_DIFF, 2 * DIFF_DH),
            sk32.reshape(1, B, T, N_DSA_KV, DSA_DH), sv32.reshape(1, B, T, N_DSA_KV, DSA_DH),
            ik32.reshape(1, B, T, IDX_DH))
    return y, rows


def kernel(x_prompt, x_sample, cache_diff_k, cache_diff_v, cache_dsa_k, cache_dsa_v, cache_dsa_kidx,
           c_prompt, c_sample, w_ada, b_ada, g_pre_mix, g_post_mix, g_pre_ffn, g_post_ffn, w_in,
           lambda_q1, lambda_k1, lambda_q2, lambda_k2, g_subln, w_out, w_gate_up, w_down,
           rel_bias_table):
    Bp = x_prompt.shape[0]
    c_all = jnp.concatenate([c_prompt, c_sample], axis=0)
    mod = _modulation(c_all, w_ada[0], b_ada[0])
    mods = [mod[:, j * D_MODEL:(j + 1) * D_MODEL][:, None, :] for j in range(6)]
    mods_p = [m[:Bp] for m in mods]
    mods_s = [m[Bp:] for m in mods]

    wm, wih, wil = _prep_in_weights(w_in[0])
    lam4 = jnp.concatenate([lambda_q1, lambda_k1, lambda_q2, lambda_k2], axis=0)
    row = lambda g: g[0].reshape(1, -1)
    params = (row(g_pre_mix), row(g_post_mix), row(g_pre_ffn), row(g_post_ffn), wm, wih, wil,
              lam4, g_subln[0], w_out[0].astype(BF16), w_gate_up[0].astype(BF16),
              w_down[0].astype(BF16), rel_bias_table)

    yp, rp = _group(x_prompt, mods_p, None, params, bt=1, tt=512, tq=256)
    past = (cache_diff_k[0], cache_diff_v[0], cache_dsa_k[0], cache_dsa_v[0], cache_dsa_kidx[0])
    ys, rs = _group(x_sample, mods_s, past, params, bt=8, tt=64, tq=128)
    return (yp, ys) + rp + rs
```

```python
import functools
import math

import jax
import jax.numpy as jnp
from jax import lax
from jax.experimental import pallas as pl
from jax.experimental.pallas import tpu as pltpu

F32 = jnp.float32
BF16 = jnp.bfloat16
I32 = jnp.int32

D_MODEL = 1024
CHUNK = 64
N_DIFF = 4
DIFF_DH = 64
DIFF_W = N_DIFF * 2 * DIFF_DH
N_DSA = 8
N_DSA_KV = 2
DSA_GROUP = N_DSA // N_DSA_KV
DSA_DH = 64
DSA_W = N_DSA * DSA_DH
DSA_KV_W = N_DSA_KV * DSA_DH
N_IDX = 4
IDX_DH = 64
TOPK = 256
N_HEADS = N_DIFF + N_DSA
NUM_BUCKETS = 32
D_FF = 2816
EPS = 1e-6
NEG = -1e30
LAM_INIT = 0.8 - 0.6 * math.exp(-0.3 * 0)

LANE = 128
SUBLANE = 8
KEY_BLOCK = 256
V_ROWS = 144
VMEM_LIMIT = 56 * 1024 * 1024
INT_MIN = -2 ** 31
FAR_BUCKET = 15
BUCKET_STEPS = (13, 20, 30, 46, 70, 108, 166)
KEY_BYTES = 4
N_BIAS_TILES = 4
N_STAGE = 2


def _rms(x):
    return x * lax.rsqrt(jnp.mean(x * x, axis=-1, keepdims=True) + EPS)


def _dot(a, b):
    return jnp.dot(a, b, preferred_element_type=F32)


def _dot_t(a, b):
    return lax.dot_general(a, b, (((1,), (1,)), ((), ())), preferred_element_type=F32)


def _mod_kernel(c_ref, w_ref, b_ref, o_ref):
    c = c_ref[...]
    s = c * jax.nn.sigmoid(c)
    o_ref[...] = jnp.dot(s, w_ref[...], preferred_element_type=F32,
                         precision=lax.Precision.HIGHEST) + b_ref[...]


def _modulation(c_all, w_ada, b_ada):
    n = c_all.shape[0]
    tn = 1024
    return pl.pallas_call(
        _mod_kernel,
        grid=(6 * D_MODEL // tn,),
        in_specs=[pl.BlockSpec((n, D_MODEL), lambda j: (0, 0)),
                  pl.BlockSpec((D_MODEL, tn), lambda j: (0, j)),
                  pl.BlockSpec((1, tn), lambda j: (0, j))],
        out_specs=pl.BlockSpec((n, tn), lambda j: (0, j)),
        out_shape=jax.ShapeDtypeStruct((n, 6 * D_MODEL), F32),
        name="adaln_mod",
        compiler_params=pltpu.CompilerParams(dimension_semantics=("arbitrary",),
                                             vmem_limit_bytes=VMEM_LIMIT),
    )(c_all, w_ada, b_ada.reshape(1, -1))


_C_DQ = 0
_C_DK = _C_DQ + 2 * DIFF_W
_C_DV = _C_DK + DIFF_W
_C_SQ = _C_DV + DIFF_W
_C_SK = _C_SQ + N_DSA * LANE
_C_SV = _C_SK + DSA_KV_W
_C_END = _C_SV + DSA_KV_W
IDX_W = 384


def _proj_kernel(x_ref, sc_ref, sh_ref, g_ref, wm_ref, wih_ref, wil_ref,
                 dk32, dv32, sk32, sv32, zi32, dq16, dk16, dv16, sq16, sk16, sv16):
    bt, tt, _ = x_ref.shape
    x = x_ref[...]
    h = (_rms(x) * g_ref[...]) * (1.0 + sc_ref[...]) + sh_ref[...]
    h = h.reshape(bt * tt, D_MODEL)
    hb = h.astype(BF16)
    hl = (h - hb.astype(F32)).astype(BF16)
    z = _dot(hb, wm_ref[...])
    wih = wih_ref[...]
    zi = _dot(hb, wih) + _dot(hb, wil_ref[...]) + _dot(hl, wih)

    def put(ref, v):
        ref[...] = v.reshape(bt, tt, v.shape[-1]).astype(ref.dtype)

    zdk = z[:, _C_DK:_C_DV]
    zdv = z[:, _C_DV:_C_SQ]
    zsk = z[:, _C_SK:_C_SV]
    zsv = z[:, _C_SV:_C_END]
    put(dk32, zdk)
    put(dv32, zdv)
    put(sk32, zsk)
    put(sv32, zsv)
    put(zi32, zi)
    put(dq16, z[:, _C_DQ:_C_DK])
    put(dk16, zdk)
    put(dv16, zdv)
    put(sq16, z[:, _C_SQ:_C_SK])
    put(sk16, zsk)
    put(sv16, zsv)


def _projection(x, sc, sh, g, wm, wih, wil, bt, tt):
    B, T, _ = x.shape
    grid = (B // bt, T // tt)
    row = lambda w: pl.BlockSpec((bt, tt, w), lambda b, t: (b, t, 0))
    per_b = pl.BlockSpec((bt, 1, D_MODEL), lambda b, t: (b, 0, 0))
    full = lambda a: pl.BlockSpec(a.shape, lambda b, t: (0,) * a.ndim,
                                  pipeline_mode=pl.Buffered(1))
    outs = [(DIFF_W, F32), (DIFF_W, F32), (DSA_KV_W, F32), (DSA_KV_W, F32), (IDX_W, F32),
            (2 * DIFF_W, BF16), (DIFF_W, BF16), (DIFF_W, BF16),
            (N_DSA * LANE, BF16), (DSA_KV_W, BF16), (DSA_KV_W, BF16)]
    return pl.pallas_call(
        _proj_kernel,
        grid=grid,
        in_specs=[row(D_MODEL), per_b, per_b, full(g), full(wm), full(wih), full(wil)],
        out_specs=[row(w) for w, _ in outs],
        out_shape=[jax.ShapeDtypeStruct((B, T, w), dt) for w, dt in outs],
        name="in_proj",
        compiler_params=pltpu.CompilerParams(dimension_semantics=("arbitrary", "arbitrary"),
                                             vmem_limit_bytes=VMEM_LIMIT),
    )(x, sc, sh, g, wm, wih, wil)


def _fold_rows(x, rows, op):
    acc = x[0:rows]
    for r in range(1, x.shape[0] // rows):
        acc = op(acc, x[r * rows:(r + 1) * rows])
    return acc


def _init_bias_tiles(tab_ref, tb_ref):
    k = lax.broadcasted_iota(I32, (LANE, LANE), 0)
    q = lax.broadcasted_iota(I32, (LANE, LANE), 1)
    buckets = []
    for t in range(N_BIAS_TILES - 1):
        rel = k - q - LANE * t
        n = jnp.abs(rel)
        large = jnp.full_like(n, 8)
        for step in BUCKET_STEPS:
            large = large + jnp.where(n >= step, 1, 0)
        buckets.append(jnp.where(rel > 0, NUM_BUCKETS // 2, 0) + jnp.where(n < 8, n, large))

    def per_head(h, carry):
        for t, bucket in enumerate(buckets):
            val = jnp.zeros((LANE, LANE), F32)
            for b in range(NUM_BUCKETS):
                val = jnp.where(bucket == b, tab_ref[b, h], val)
            tb_ref[h, t] = val
        tb_ref[h, N_BIAS_TILES - 1] = jnp.full((LANE, LANE), tab_ref[FAR_BUCKET, h], F32)
        return carry

    lax.fori_loop(0, N_HEADS, per_head, 0)


def _attn_kernel(tab_ref, lam_ref, gsub_ref, dq_ref, sq_ref, iq_ref, iw_ref,
                 dk_ref, dv_ref, sk_ref, sv_ref, ik_ref, o_ref,
                 tb_ref, ltri_ref, key_ref, byte_ref, s_ref, acc_ref, ot_ref, *, tq, q_offset):
    tk = KEY_BLOCK
    b = pl.program_id(0)
    i = pl.program_id(1)

    @pl.when((b == 0) & (i == 0))
    def _():
        _init_bias_tiles(tab_ref, tb_ref)
        ltri_ref[...] = jnp.where(lax.broadcasted_iota(I32, (tk, tk), 1)
                                  < lax.broadcasted_iota(I32, (tk, tk), 0), 1.0, 0.0).astype(BF16)

    qpos0 = q_offset + i * tq
    lim_last = CHUNK * ((qpos0 + tq - 1) // CHUNK + 1)
    nblk = (lim_last + tk - 1) // tk
    qq = lax.broadcasted_iota(I32, (1, tq), 1)
    lim_q = CHUNK * ((qpos0 + qq) // CHUNK + 1)
    kk = lax.broadcasted_iota(I32, (tk, tq), 0)

    def key_rows(jb):
        return pl.ds(pl.multiple_of(jb * tk, tk), tk)

    def over_blocks(body, init, unroll=2):
        def run(first, n, carry):
            for u in range(n):
                carry = body(first + u, carry)
            return carry
        carry = lax.fori_loop(0, nblk // unroll, lambda jp, c: run(unroll * jp, unroll, c), init)
        n = unroll // 2
        while n >= 1:
            first = (nblk // (2 * n)) * (2 * n)
            carry = lax.fori_loop(0, (nblk // n) % 2, lambda _, c, first=first, n=n: run(first, n, c),
                                  carry)
            n //= 2
        return carry

    def bias_tile(head, jb):
        rows = []
        for a in range(tk // LANE):
            parts = []
            for c in range(tq // LANE):
                t = jnp.clip((qpos0 + c * LANE - (jb * tk + a * LANE)) // LANE, 0, N_BIAS_TILES - 1)
                parts.append(tb_ref[head, t])
            rows.append(jnp.concatenate(parts, axis=1))
        return jnp.concatenate(rows, axis=0)

    iw = iw_ref[...]

    def score_block(jb, carry):
        kc = ik_ref[key_rows(jb), :]
        sc = jnp.zeros((tk, tq), F32)
        for h in range(N_IDX):
            d = _dot_t(kc, iq_ref[:, h * 256:(h + 1) * 256])
            sc = sc + jnp.maximum(d, 0.0) * iw[h:h + 1]
        sc = jnp.where(kk + jb * tk < lim_q, sc, NEG)
        bits = pltpu.bitcast(sc, I32)
        key = bits ^ ((bits >> 31) & 0x7FFFFFFF)
        key_ref[jb] = key
        for lvl in range(KEY_BYTES):
            byte = ((key >> (8 * lvl)) & 0xFF) if lvl < KEY_BYTES - 1 else (key >> 24) + 128
            byte_ref[lvl, jb] = byte.astype(F32).astype(BF16)
        return carry

    over_blocks(score_block, 0)

    one16 = jnp.ones((), BF16)
    zero16 = jnp.zeros((), BF16)

    def count_digits(lvl, cand):
        def body(jb, acc):
            hit = jnp.where(byte_ref[lvl, jb] >= cand, one16, zero16)
            return acc + _fold_rows(hit, 2 * SUBLANE, jnp.add).astype(F32)
        acc = over_blocks(body, jnp.zeros((2 * SUBLANE, tq), F32))
        return jnp.sum(acc, axis=0, keepdims=True)

    want = jnp.full((1, tq), float(TOPK), F32)
    thr = jnp.zeros((1, tq), I32)
    for lvl in reversed(range(KEY_BYTES)):
        def digit_bit(it, state, lvl=lvl, want=want):
            digit, above = state
            cand = digit + jnp.left_shift(jnp.int32(1), 7 - it).astype(F32)
            cnt = count_digits(lvl, cand.astype(BF16))
            ok = cnt >= want
            return jnp.where(ok, cand, digit), jnp.where(ok, above, cnt)

        digit, above = lax.fori_loop(0, 8, digit_bit,
                                     (jnp.zeros((1, tq), F32), jnp.zeros((1, tq), F32)))
        want = want - above
        digit_i = digit.astype(I32)
        thr = thr | ((digit_i - 128) << 24 if lvl == KEY_BYTES - 1 else digit_i << (8 * lvl))
        if lvl > 0:
            def narrow(jb, carry, lvl=lvl, digit16=digit.astype(BF16)):
                byte_ref[lvl - 1, jb] = jnp.where(byte_ref[lvl, jb] == digit16,
                                                  byte_ref[lvl - 1, jb], -one16)
                return carry
            over_blocks(narrow, 0)

    def mask_block(jb, before):
        kb = key_ref[jb]
        eq = jnp.where(kb == thr, 1.0, 0.0)
        rank = _dot(ltri_ref[...], eq.astype(BF16)) + before
        keep = jnp.where(rank < want, eq, 0.0)
        sel = jnp.where(kb > thr, 0.0, (keep - 1.0) * (-NEG))
        sel = jnp.where(kk + jb * tk < lim_q, sel, NEG)
        key_ref[jb] = pltpu.bitcast(sel, I32)
        return before + jnp.sum(_fold_rows(eq, SUBLANE, jnp.add), axis=0, keepdims=True)

    over_blocks(mask_block, jnp.zeros((1, tq), F32))

    lam = lam_ref[...]
    lam_full = (jnp.exp(jnp.sum(lam[0:1] * lam[1:2], axis=1, keepdims=True))
                - jnp.exp(jnp.sum(lam[2:3] * lam[3:4], axis=1, keepdims=True)) + LAM_INIT)

    def dsa_group(pair):
        heads = [N_STAGE * pair + n for n in range(N_STAGE)]

        def scores(jb):
            kb = sk_ref[key_rows(jb), :]
            sel = pltpu.bitcast(key_ref[jb], F32)
            return [_dot_t(kb, sq_ref[:, h * LANE:(h + 1) * LANE]) + bias_tile(N_DIFF + h, jb) + sel
                    for h in heads]

        def finish():
            for n, h in enumerate(heads):
                g = h // DSA_GROUP
                a = acc_ref[n]
                out = a[g * DSA_DH:(g + 1) * DSA_DH] / a[2 * DSA_DH:2 * DSA_DH + 1]
                ot_ref[DIFF_W + h * DSA_DH:DIFF_W + (h + 1) * DSA_DH, :] = out

        return scores, (lambda n, jb: sv_ref[jb]), finish

    def diff_group(h):
        def scores(jb):
            kb = dk_ref[key_rows(jb), h * LANE:(h + 1) * LANE]
            bias = bias_tile(h, jb) + jnp.where(kk + jb * tk < lim_q, 0.0, NEG)
            return [_dot_t(kb, dq_ref[:, (2 * h + part) * LANE:(2 * h + part + 1) * LANE]) + bias
                    for part in range(2)]

        def finish():
            a1 = acc_ref[0]
            a2 = acc_ref[1]
            od = (a1[0:LANE] / a1[LANE:LANE + 1]
                  - lam_full * (a2[0:LANE] / a2[LANE:LANE + 1]))
            od = od * lax.rsqrt(jnp.mean(od * od, axis=0, keepdims=True) + EPS)
            ot_ref[h * LANE:(h + 1) * LANE, :] = (od * gsub_ref[...]) * (1.0 - LAM_INIT)

        return scores, (lambda n, jb: dv_ref[h, jb]), finish

    groups = ([dsa_group(p) for p in range(N_DSA // N_STAGE)]
              + [diff_group(h) for h in range(N_DIFF)])
    m_prev = None
    for k in range(len(groups) + 1):
        produce = groups[k] if k < len(groups) else None
        consume = groups[k - 1] if k > 0 else None

        def step(jb, mx, k=k, produce=produce, consume=consume, m_prev=m_prev):
            if consume is not None:
                for n in range(N_STAGE):
                    p = jnp.exp(s_ref[(k - 1) % 2, n, jb] - m_prev[n]).astype(BF16)
                    acc_ref[n] += _dot(consume[1](n, jb), p)
            if produce is None:
                return mx
            out = []
            for n, s in enumerate(produce[0](jb)):
                s_ref[k % 2, n, jb] = s
                out.append(jnp.maximum(mx[n], _fold_rows(s, SUBLANE, jnp.maximum)))
            return tuple(out)

        if consume is not None:
            for n in range(N_STAGE):
                acc_ref[n] = jnp.zeros((V_ROWS, tq), F32)
        mx = over_blocks(step, tuple(jnp.full((SUBLANE, tq), -3e38, F32) for _ in range(N_STAGE)),
                         unroll=4)
        if consume is not None:
            consume[2]()
        m_prev = [jnp.max(x, axis=0, keepdims=True) for x in mx]

    o_ref[...] = ot_ref[...].T.astype(o_ref.dtype)


def _attention(tab, lam4, gsub_t, dq, sq, iq, iw_t, dk, dv_t, sk, sv_t, ik, tq, q_offset):
    B, T, _ = dq.shape
    Lp = dk.shape[1]
    nblk_max = Lp // KEY_BLOCK
    qspec = lambda w: pl.BlockSpec((None, tq, w), lambda b, i: (b, i, 0))
    once = pl.Buffered(1)
    kspec = lambda w: pl.BlockSpec((None, Lp, w), lambda b, i: (b, 0, 0), pipeline_mode=once)
    small = lambda a: pl.BlockSpec(a.shape, lambda b, i: (0,) * a.ndim)
    kern = functools.partial(_attn_kernel, tq=tq, q_offset=q_offset)
    return pl.pallas_call(
        kern,
        grid=(B, T // tq),
        in_specs=[pl.BlockSpec(memory_space=pltpu.SMEM), small(lam4), small(gsub_t),
                  qspec(2 * DIFF_W), qspec(N_DSA * LANE), qspec(N_IDX * 256),
                  pl.BlockSpec((None, SUBLANE, tq), lambda b, i: (b, 0, i)),
                  kspec(DIFF_W),
                  pl.BlockSpec((None, N_DIFF, nblk_max, V_ROWS, KEY_BLOCK),
                               lambda b, i: (b, 0, 0, 0, 0), pipeline_mode=once),
                  kspec(DSA_KV_W),
                  pl.BlockSpec((None, nblk_max, V_ROWS, KEY_BLOCK),
                               lambda b, i: (b, 0, 0, 0), pipeline_mode=once),
                  kspec(256)],
        out_specs=pl.BlockSpec((None, tq, D_MODEL), lambda b, i: (b, i, 0)),
        out_shape=jax.ShapeDtypeStruct((B, T, D_MODEL), BF16),
        name="attn",
        scratch_shapes=[pltpu.VMEM((N_HEADS, N_BIAS_TILES, LANE, LANE), F32),
                        pltpu.VMEM((KEY_BLOCK, KEY_BLOCK), BF16),
                        pltpu.VMEM((nblk_max, KEY_BLOCK, tq), I32),
                        pltpu.VMEM((KEY_BYTES, nblk_max, KEY_BLOCK, tq), BF16),
                        pltpu.VMEM((2, N_STAGE, nblk_max, KEY_BLOCK, tq), F32),
                        pltpu.VMEM((N_STAGE, V_ROWS, tq), F32),
                        pltpu.VMEM((D_MODEL, tq), F32)],
        compiler_params=pltpu.CompilerParams(dimension_semantics=("arbitrary", "arbitrary"),
                                             vmem_limit_bytes=VMEM_LIMIT),
    )(tab, lam4, gsub_t, dq, sq, iq, iw_t, dk, dv_t, sk, sv_t, ik)


FF_CHUNK = D_FF // 2


def _ffn_kernel(x_ref, o_ref, ga1_ref, sc2_ref, sh2_ref, ga2_ref, gpm_ref, gpf_ref, gpo_ref,
                wo_ref, wgu_ref, wd_ref, y_ref):
    bt, tt, _ = x_ref.shape
    rows = bt * tt
    m = _dot(o_ref[...].reshape(rows, D_MODEL), wo_ref[...]).reshape(bt, tt, D_MODEL)
    x1 = x_ref[...] + ga1_ref[...] * (_rms(m) * gpm_ref[...])
    h2 = (_rms(x1) * gpf_ref[...]) * (1.0 + sc2_ref[...]) + sh2_ref[...]
    h2 = h2.reshape(rows, D_MODEL).astype(BF16)
    f = jnp.zeros((rows, D_MODEL), F32)
    for c in range(D_FF // FF_CHUNK):
        g = _dot(h2, wgu_ref[:, c * FF_CHUNK:(c + 1) * FF_CHUNK])
        u = _dot(h2, wgu_ref[:, D_FF + c * FF_CHUNK:D_FF + (c + 1) * FF_CHUNK])
        a = (g * jax.nn.sigmoid(g) * u).astype(BF16)
        f = f + _dot(a, wd_ref[c * FF_CHUNK:(c + 1) * FF_CHUNK, :])
    f = f.reshape(bt, tt, D_MODEL)
    y_ref[...] = x1 + ga2_ref[...] * (_rms(f) * gpo_ref[...])


def _out_ffn(x, o, ga1, sc2, sh2, ga2, gpm, gpf, gpo, wo, wgu, wd, bt, tt):
    B, T, _ = x.shape
    row = pl.BlockSpec((bt, tt, D_MODEL), lambda b, t: (b, t, 0))
    per_b = pl.BlockSpec((bt, 1, D_MODEL), lambda b, t: (b, 0, 0))
    full = lambda a: pl.BlockSpec(a.shape, lambda b, t: (0,) * a.ndim,
                                  pipeline_mode=pl.Buffered(1))
    return pl.pallas_call(
        _ffn_kernel,
        grid=(B // bt, T // tt),
        in_specs=[row, row, per_b, per_b, per_b, per_b, full(gpm), full(gpf), full(gpo),
                  full(wo), full(wgu), full(wd)],
        out_specs=row,
        out_shape=jax.ShapeDtypeStruct((B, T, D_MODEL), F32),
        name="out_ffn",
        compiler_params=pltpu.CompilerParams(dimension_semantics=("arbitrary", "arbitrary"),
                                             vmem_limit_bytes=VMEM_LIMIT),
    )(x, o, ga1, sc2, sh2, ga2, gpm, gpf, gpo, wo, wgu, wd)


def _split_hi_lo(a):
    hi = a.astype(BF16)
    lo = (a - hi.astype(F32)).astype(BF16)
    return hi, lo


def _prep_in_weights(w_in):
    cuts = [0, 512, 1024, 1536, 2048, 2176, 2304, 2560, 2624, 2628]
    wdq, wdk, wdv, wsq, wsk, wsv, wiq, wik, wiw = [w_in[:, a:b] for a, b in zip(cuts[:-1], cuts[1:])]
    zeros64 = jnp.zeros((D_MODEL, DIFF_DH), F32)
    scale = DIFF_DH ** -0.5
    dq_cols = []
    for h in range(N_DIFF):
        q1 = wdq[:, h * 128:h * 128 + 64] * scale
        q2 = wdq[:, h * 128 + 64:(h + 1) * 128] * scale
        dq_cols += [q1, zeros64, zeros64, q2]
    sq_cols = []
    for h in range(N_DSA):
        q = wsq[:, h * 64:(h + 1) * 64] * scale
        sq_cols += [q, zeros64] if h < DSA_GROUP else [zeros64, q]
    wm = jnp.concatenate(dq_cols + [wdk, wdv] + sq_cols + [wsk, wsv], axis=1).astype(BF16)
    wi = jnp.concatenate([wiq, wik, wiw, jnp.zeros((D_MODEL, IDX_W - 324), F32)], axis=1)
    wih, wil = _split_hi_lo(wi)
    return wm, wih, wil


def _idx_operands(iq, ik):
    qh, ql = _split_hi_lo(iq)
    kh, kl = _split_hi_lo(ik)
    zq = jnp.zeros(qh.shape[:-1] + (IDX_DH,), BF16)
    parts = []
    for h in range(N_IDX):
        s = slice(h * IDX_DH, (h + 1) * IDX_DH)
        parts += [qh[..., s], qh[..., s], ql[..., s], zq]
    zk = jnp.zeros(kh.shape[:-1] + (IDX_DH,), BF16)
    return jnp.concatenate(parts, axis=-1), jnp.concatenate([kh, kl, kh, zk], axis=-1)


def _pad_axis1(a, n):
    return jnp.pad(a, ((0, 0), (0, n - a.shape[1])) + ((0, 0),) * (a.ndim - 2))


def _value_operand(v):
    B, L, H, W = v.shape
    vt = jnp.transpose(v, (0, 2, 3, 1))
    extra = jnp.concatenate([jnp.ones((B, H, 1, L), BF16),
                             jnp.zeros((B, H, V_ROWS - W - 1, L), BF16)], axis=2)
    vt = jnp.concatenate([vt, extra], axis=2).reshape(B, H, V_ROWS, L // KEY_BLOCK, KEY_BLOCK)
    return jnp.transpose(vt, (0, 1, 3, 2, 4))


def _group(x, mods, past, params, bt, tt, tq):
    (g_pre_mix, g_post_mix, g_pre_ffn, g_post_ffn, wm, wih, wil, lam4, g_subln,
     wo, wgu, wd, tab) = params
    B, T, _ = x.shape
    sh1, sc1, ga1, sh2, sc2, ga2 = mods
    (dk32, dv32, sk32, sv32, zi32, dq16, dk16, dv16, sq16, sk16, sv16) = _projection(
        x, sc1, sh1, g_pre_mix, wm, wih, wil, bt, tt)
    ik32 = zi32[..., 256:256 + IDX_DH]
    iw = zi32[..., 256 + IDX_DH:256 + IDX_DH + N_IDX] * (1.0 / 16.0)
    if past is None:
        q_offset = 0
        lp = T
        iqc, ikc = _idx_operands(zi32[..., :256], ik32)
        dk, dv, sk, sv = dk16, dv16, sk16, sv16
    else:
        pk, pv, psk, psv, pik = past
        P = pk.shape[1]
        q_offset = P
        lp = -(-(P + T) // KEY_BLOCK) * KEY_BLOCK
        iqc, ikc = _idx_operands(zi32[..., :256], jnp.concatenate([pik, ik32], axis=1))
        cat = lambda p, n: _pad_axis1(
            jnp.concatenate([p.reshape(B, P, -1).astype(BF16), n], axis=1), lp)
        dk, dv, sk, sv = cat(pk, dk16), cat(pv, dv16), cat(psk, sk16), cat(psv, sv16)
        ikc = _pad_axis1(ikc, lp)
    tpad = -(-T // tq) * tq
    qpad = lambda a: _pad_axis1(a, tpad)
    iw_t = _pad_axis1(jnp.transpose(qpad(iw), (0, 2, 1)), SUBLANE)
    dv_t = _value_operand(dv.reshape(B, lp, N_DIFF, 2 * DIFF_DH))
    sv_t = _value_operand(sv.reshape(B, lp, 1, DSA_KV_W))[:, 0]
    gsub_t = jnp.broadcast_to(g_subln.reshape(-1, 1), (2 * DIFF_DH, tq))
    o = _attention(tab, lam4, gsub_t, qpad(dq16), qpad(sq16), qpad(iqc), iw_t,
                   dk, dv_t, sk, sv_t, ikc, tq, q_offset)[:, :T]
    y = _out_ffn(x, o, ga1, sc2, sh2, ga2, g_post_mix, g_pre_ffn, g_post_ffn, wo, wgu, wd, bt, tt)
    rows = (dk32.reshape(1, B, T, N_DIFF, 2 * DIFF_DH), dv32.reshape(1, B, T, N_DIFF, 2 * DIFF_DH),
            sk32.reshape(1, B, T, N_DSA_KV, DSA_DH), sv32.reshape(1, B, T, N_DSA_KV, DSA_DH),
            ik32.reshape(1, B, T, IDX_DH))
    return y, rows


def kernel(x_prompt, x_sample, cache_diff_k, cache_diff_v, cache_dsa_k, cache_dsa_v, cache_dsa_kidx,
           c_prompt, c_sample, w_ada, b_ada, g_pre_mix, g_post_mix, g_pre_ffn, g_post_ffn, w_in,
           lambda_q1, lambda_k1, lambda_q2, lambda_k2, g_subln, w_out, w_gate_up, w_down,
           rel_bias_table):
    Bp = x_prompt.shape[0]
    c_all = jnp.concatenate([c_prompt, c_sample], axis=0)
    mod = _modulation(c_all, w_ada[0], b_ada[0])
    mods = [mod[:, j * D_MODEL:(j + 1) * D_MODEL][:, None, :] for j in range(6)]
    mods_p = [m[:Bp] for m in mods]
    mods_s = [m[Bp:] for m in mods]

    wm, wih, wil = _prep_in_weights(w_in[0])
    lam4 = jnp.concatenate([lambda_q1, lambda_k1, lambda_q2, lambda_k2], axis=0)
    row = lambda g: g[0].reshape(1, -1)
    params = (row(g_pre_mix), row(g_post_mix), row(g_pre_ffn), row(g_post_ffn), wm, wih, wil,
              lam4, g_subln[0], w_out[0].astype(BF16), w_gate_up[0].astype(BF16),
              w_down[0].astype(BF16), rel_bias_table)

    yp, rp = _group(x_prompt, mods_p, None, params, bt=1, tt=512, tq=256)
    past = (cache_diff_k[0], cache_diff_v[0], cache_dsa_k[0], cache_dsa_v[0], cache_dsa_kidx[0])
    ys, rs = _group(x_sample, mods_s, past, params, bt=8, tt=64, tq=128)
    return (yp, ys) + rp + rs
```

```python
import functools
import math

import jax
import jax.numpy as jnp
from jax import lax
from jax.experimental import pallas as pl
from jax.experimental.pallas import tpu as pltpu

F32 = jnp.float32
BF16 = jnp.bfloat16
I32 = jnp.int32

D_MODEL = 1024
CHUNK = 64
N_DIFF = 4
DIFF_DH = 64
DIFF_W = N_DIFF * 2 * DIFF_DH
N_DSA = 8
N_DSA_KV = 2
DSA_GROUP = N_DSA // N_DSA_KV
DSA_DH = 64
DSA_W = N_DSA * DSA_DH
DSA_KV_W = N_DSA_KV * DSA_DH
N_IDX = 4
IDX_DH = 64
TOPK = 256
N_HEADS = N_DIFF + N_DSA
NUM_BUCKETS = 32
D_FF = 2816
EPS = 1e-6
NEG = -1e30
LOG2E = math.log2(math.e)
LAM_INIT = 0.8 - 0.6 * math.exp(-0.3 * 0)

LANE = 128
SUBLANE = 8
KEY_BLOCK = 256
V_ROWS = 144
VMEM_LIMIT = 56 * 1024 * 1024
INT_MIN = -2 ** 31
FAR_BUCKET = 15
BUCKET_STEPS = (13, 20, 30, 46, 70, 108, 166)
KEY_BYTES = 4
N_BIAS_TILES = 4
N_STAGE = 2


def _rms(x):
    return x * lax.rsqrt(jnp.mean(x * x, axis=-1, keepdims=True) + EPS)


def _dot(a, b):
    return jnp.dot(a, b, preferred_element_type=F32)


def _dot_t(a, b):
    return lax.dot_general(a, b, (((1,), (1,)), ((), ())), preferred_element_type=F32)


def _mod_kernel(c_ref, w_ref, b_ref, o_ref):
    c = c_ref[...]
    s = c * jax.nn.sigmoid(c)
    o_ref[...] = jnp.dot(s, w_ref[...], preferred_element_type=F32,
                         precision=lax.Precision.HIGHEST) + b_ref[...]


def _modulation(c_all, w_ada, b_ada):
    n = c_all.shape[0]
    tn = 1024
    return pl.pallas_call(
        _mod_kernel,
        grid=(6 * D_MODEL // tn,),
        in_specs=[pl.BlockSpec((n, D_MODEL), lambda j: (0, 0)),
                  pl.BlockSpec((D_MODEL, tn), lambda j: (0, j)),
                  pl.BlockSpec((1, tn), lambda j: (0, j))],
        out_specs=pl.BlockSpec((n, tn), lambda j: (0, j)),
        out_shape=jax.ShapeDtypeStruct((n, 6 * D_MODEL), F32),
        name="adaln_mod",
        compiler_params=pltpu.CompilerParams(dimension_semantics=("arbitrary",),
                                             vmem_limit_bytes=VMEM_LIMIT),
    )(c_all, w_ada, b_ada.reshape(1, -1))


_C_DQ = 0
_C_DK = _C_DQ + 2 * DIFF_W
_C_DV = _C_DK + DIFF_W
_C_SQ = _C_DV + DIFF_W
_C_SK = _C_SQ + N_DSA * LANE
_C_SV = _C_SK + DSA_KV_W
_C_END = _C_SV + DSA_KV_W
IDX_W = 384


def _hi_lo(a):
    hi = a.astype(BF16)
    return hi, (a - hi.astype(F32)).astype(BF16)


def _value_rows(v):
    row = lax.broadcasted_iota(I32, (V_ROWS - LANE, KEY_BLOCK), 0)
    extra = jnp.where(row == 0, 1.0, 0.0)
    return jnp.concatenate([v.T, extra], axis=0).astype(BF16)


def _proj_kernel(x_ref, sc_ref, sh_ref, g_ref, wm_ref, wih_ref, wil_ref,
                 dk32, dv32, sk32, sv32, ik32, iw32, dq16, dk16, sq16, sk16, iq16, ik16, dvx, svx,
                 *, transpose_values):
    bt, tt, _ = x_ref.shape
    x = x_ref[...]
    h = (_rms(x) * g_ref[...]) * (1.0 + sc_ref[...]) + sh_ref[...]
    h = h.reshape(bt * tt, D_MODEL)
    hb, hl = _hi_lo(h)
    z = _dot(hb, wm_ref[...])
    wih = wih_ref[...]
    zi = _dot(hb, wih) + _dot(hb, wil_ref[...]) + _dot(hl, wih)

    def put(ref, v):
        ref[...] = v.reshape(bt, tt, v.shape[-1]).astype(ref.dtype)

    zdk = z[:, _C_DK:_C_DV]
    zdv = z[:, _C_DV:_C_SQ]
    zsk = z[:, _C_SK:_C_SV]
    zsv = z[:, _C_SV:_C_END]
    zik = zi[:, 256:256 + IDX_DH]
    put(dk32, zdk)
    put(dv32, zdv)
    put(sk32, zsk)
    put(sv32, zsv)
    put(ik32, zik)
    put(iw32, zi[:, 256:IDX_W])
    put(dq16, z[:, _C_DQ:_C_DK])
    put(dk16, zdk)
    put(sq16, z[:, _C_SQ:_C_SK])
    put(sk16, zsk)

    qh, ql = _hi_lo(zi[:, 0:N_IDX * IDX_DH])
    kh, kl = _hi_lo(zik)
    pad = jnp.zeros_like(kh)
    head = lambda a, n: a[:, n * IDX_DH:(n + 1) * IDX_DH]
    put(iq16, jnp.concatenate([p for n in range(N_IDX)
                               for p in (head(qh, n), head(qh, n), head(ql, n), pad)], axis=1))
    put(ik16, jnp.concatenate([kh, kl, kh, pad], axis=1))

    if transpose_values:
        for kb in range(tt // KEY_BLOCK):
            rows = slice(kb * KEY_BLOCK, (kb + 1) * KEY_BLOCK)
            for n in range(N_DIFF):
                dvx[n, kb] = _value_rows(zdv[rows, n * LANE:(n + 1) * LANE])
            svx[kb] = _value_rows(zsv[rows, :])
    else:
        put(dvx, zdv)
        put(svx, zsv)


def _projection(x, sc, sh, g, wm, wih, wil, bt, tt, transpose_values):
    B, T, _ = x.shape
    grid = (B // bt, T // tt)
    row = lambda w: pl.BlockSpec((bt, tt, w), lambda b, t: (b, t, 0))
    per_b = pl.BlockSpec((bt, 1, D_MODEL), lambda b, t: (b, 0, 0))
    full = lambda a: pl.BlockSpec(a.shape, lambda b, t: (0,) * a.ndim,
                                  pipeline_mode=pl.Buffered(1))
    outs = [(DIFF_W, F32), (DIFF_W, F32), (DSA_KV_W, F32), (DSA_KV_W, F32), (IDX_DH, F32),
            (LANE, F32), (2 * DIFF_W, BF16), (DIFF_W, BF16), (N_DSA * LANE, BF16),
            (DSA_KV_W, BF16), (N_IDX * 256, BF16), (256, BF16)]
    out_specs = [row(w) for w, _ in outs]
    out_shape = [jax.ShapeDtypeStruct((B, T, w), dt) for w, dt in outs]
    if transpose_values:
        assert bt == 1 and tt % KEY_BLOCK == 0
        nb = tt // KEY_BLOCK
        out_specs += [pl.BlockSpec((None, N_DIFF, nb, V_ROWS, KEY_BLOCK), lambda b, t: (b, 0, t, 0, 0)),
                      pl.BlockSpec((None, nb, V_ROWS, KEY_BLOCK), lambda b, t: (b, t, 0, 0))]
        out_shape += [jax.ShapeDtypeStruct((B, N_DIFF, T // KEY_BLOCK, V_ROWS, KEY_BLOCK), BF16),
                      jax.ShapeDtypeStruct((B, T // KEY_BLOCK, V_ROWS, KEY_BLOCK), BF16)]
    else:
        out_specs += [row(DIFF_W), row(DSA_KV_W)]
        out_shape += [jax.ShapeDtypeStruct((B, T, DIFF_W), BF16),
                      jax.ShapeDtypeStruct((B, T, DSA_KV_W), BF16)]
    return pl.pallas_call(
        functools.partial(_proj_kernel, transpose_values=transpose_values),
        grid=grid,
        in_specs=[row(D_MODEL), per_b, per_b, full(g), full(wm), full(wih), full(wil)],
        out_specs=out_specs,
        out_shape=out_shape,
        name="in_proj",
        compiler_params=pltpu.CompilerParams(dimension_semantics=("arbitrary", "arbitrary"),
                                             vmem_limit_bytes=VMEM_LIMIT),
    )(x, sc, sh, g, wm, wih, wil)


def _fold_rows(x, rows, op):
    acc = x[0:rows]
    for r in range(1, x.shape[0] // rows):
        acc = op(acc, x[r * rows:(r + 1) * rows])
    return acc


def _init_bias_tiles(tab_ref, tb_ref):
    k = lax.broadcasted_iota(I32, (LANE, LANE), 0)
    q = lax.broadcasted_iota(I32, (LANE, LANE), 1)
    buckets = []
    for t in range(N_BIAS_TILES - 1):
        rel = k - q - LANE * t
        n = jnp.abs(rel)
        large = jnp.full_like(n, 8)
        for step in BUCKET_STEPS:
            large = large + jnp.where(n >= step, 1, 0)
        buckets.append(jnp.where(rel > 0, NUM_BUCKETS // 2, 0) + jnp.where(n < 8, n, large))

    def per_head(h, carry):
        for t, bucket in enumerate(buckets):
            val = jnp.zeros((LANE, LANE), F32)
            for b in range(NUM_BUCKETS):
                val = jnp.where(bucket == b, tab_ref[b, h] * LOG2E, val)
            tb_ref[h, t] = val
        tb_ref[h, N_BIAS_TILES - 1] = jnp.full((LANE, LANE), tab_ref[FAR_BUCKET, h] * LOG2E, F32)
        return carry

    lax.fori_loop(0, N_HEADS, per_head, 0)


def _attn_kernel(tab_ref, lam_ref, gsub_ref, dq_ref, sq_ref, iq_ref, iw_ref,
                 dk_ref, dv_ref, sk_ref, sv_ref, ik_ref, o_ref,
                 tb_ref, ltri_ref, key_ref, byte_ref, s_ref, acc_ref, ot_ref, *, tq, q_offset):
    tk = KEY_BLOCK
    b = pl.program_id(0)
    i = pl.program_id(1)

    @pl.when((b == 0) & (i == 0))
    def _():
        _init_bias_tiles(tab_ref, tb_ref)
        ltri_ref[...] = jnp.where(lax.broadcasted_iota(I32, (tk, tk), 1)
                                  < lax.broadcasted_iota(I32, (tk, tk), 0), 1.0, 0.0).astype(BF16)

    qpos0 = q_offset + i * tq
    lim_last = CHUNK * ((qpos0 + tq - 1) // CHUNK + 1)
    nblk = (lim_last + tk - 1) // tk
    qq = lax.broadcasted_iota(I32, (1, tq), 1)
    lim_q = CHUNK * ((qpos0 + qq) // CHUNK + 1)
    kk = lax.broadcasted_iota(I32, (tk, tq), 0)

    def key_rows(jb):
        return pl.ds(pl.multiple_of(jb * tk, tk), tk)

    def over_blocks(body, init, unroll=2):
        def run(first, n, carry):
            for u in range(n):
                carry = body(first + u, carry)
            return carry
        carry = lax.fori_loop(0, nblk // unroll, lambda jp, c: run(unroll * jp, unroll, c), init)
        n = unroll // 2
        while n >= 1:
            first = (nblk // (2 * n)) * (2 * n)
            carry = lax.fori_loop(0, (nblk // n) % 2, lambda _, c, first=first, n=n: run(first, n, c),
                                  carry)
            n //= 2
        return carry

    def bias_tile(head, jb):
        rows = []
        for a in range(tk // LANE):
            parts = []
            for c in range(tq // LANE):
                t = jnp.clip((qpos0 + c * LANE - (jb * tk + a * LANE)) // LANE, 0, N_BIAS_TILES - 1)
                parts.append(tb_ref[head, t])
            rows.append(jnp.concatenate(parts, axis=1))
        return jnp.concatenate(rows, axis=0)

    iw = iw_ref[...]

    def score_block(jb, carry):
        kc = ik_ref[key_rows(jb), :]
        sc = jnp.zeros((tk, tq), F32)
        for h in range(N_IDX):
            d = _dot_t(kc, iq_ref[:, h * 256:(h + 1) * 256])
            sc = sc + jnp.maximum(d, 0.0) * iw[h:h + 1]
        sc = jnp.where(kk + jb * tk < lim_q, sc, NEG)
        bits = pltpu.bitcast(sc, I32)
        key = bits ^ ((bits >> 31) & 0x7FFFFFFF)
        key_ref[jb] = key
        for lvl in range(KEY_BYTES):
            byte = ((key >> (8 * lvl)) & 0xFF) if lvl < KEY_BYTES - 1 else (key >> 24) + 128
            byte_ref[lvl, jb] = byte.astype(F32).astype(BF16)
        return carry

    over_blocks(score_block, 0)

    one16 = jnp.ones((), BF16)
    zero16 = jnp.zeros((), BF16)

    def count_digits(lvl, cand):
        def body(jb, acc):
            hit = jnp.where(byte_ref[lvl, jb] >= cand, one16, zero16)
            return acc + _fold_rows(hit, 2 * SUBLANE, jnp.add).astype(F32)
        acc = over_blocks(body, jnp.zeros((2 * SUBLANE, tq), F32))
        return jnp.sum(acc, axis=0, keepdims=True)

    want = jnp.full((1, tq), float(TOPK), F32)
    thr = jnp.zeros((1, tq), I32)
    for lvl in reversed(range(KEY_BYTES)):
        def digit_bit(it, state, lvl=lvl, want=want):
            digit, above = state
            cand = digit + jnp.left_shift(jnp.int32(1), 7 - it).astype(F32)
            cnt = count_digits(lvl, cand.astype(BF16))
            ok = cnt >= want
            return jnp.where(ok, cand, digit), jnp.where(ok, above, cnt)

        digit, above = lax.fori_loop(0, 8, digit_bit,
                                     (jnp.zeros((1, tq), F32), jnp.zeros((1, tq), F32)))
        want = want - above
        digit_i = digit.astype(I32)
        thr = thr | ((digit_i - 128) << 24 if lvl == KEY_BYTES - 1 else digit_i << (8 * lvl))
        if lvl > 0:
            def narrow(jb, carry, lvl=lvl, digit16=digit.astype(BF16)):
                byte_ref[lvl - 1, jb] = jnp.where(byte_ref[lvl, jb] == digit16,
                                                  byte_ref[lvl - 1, jb], -one16)
                return carry
            over_blocks(narrow, 0)

    def mask_block(jb, before):
        kb = key_ref[jb]
        eq = jnp.where(kb == thr, 1.0, 0.0)
        rank = _dot(ltri_ref[...], eq.astype(BF16)) + before
        keep = jnp.where(rank < want, eq, 0.0)
        sel = jnp.where(kb > thr, 0.0, (keep - 1.0) * (-NEG))
        sel = jnp.where(kk + jb * tk < lim_q, sel, NEG)
        key_ref[jb] = pltpu.bitcast(sel, I32)
        return before + jnp.sum(_fold_rows(eq, SUBLANE, jnp.add), axis=0, keepdims=True)

    over_blocks(mask_block, jnp.zeros((1, tq), F32))

    lam = lam_ref[...]
    lam_full = (jnp.exp(jnp.sum(lam[0:1] * lam[1:2], axis=1, keepdims=True))
                - jnp.exp(jnp.sum(lam[2:3] * lam[3:4], axis=1, keepdims=True)) + LAM_INIT)

    def dsa_group(pair):
        heads = [N_STAGE * pair + n for n in range(N_STAGE)]

        def scores(jb):
            kb = sk_ref[key_rows(jb), :]
            sel = pltpu.bitcast(key_ref[jb], F32)
            return [_dot_t(kb, sq_ref[:, h * LANE:(h + 1) * LANE]) + bias_tile(N_DIFF + h, jb) + sel
                    for h in heads]

        def finish():
            for n, h in enumerate(heads):
                g = h // DSA_GROUP
                a = acc_ref[n]
                out = a[g * DSA_DH:(g + 1) * DSA_DH] / a[2 * DSA_DH:2 * DSA_DH + 1]
                ot_ref[DIFF_W + h * DSA_DH:DIFF_W + (h + 1) * DSA_DH, :] = out

        return scores, (lambda n, jb: sv_ref[jb]), finish

    def diff_group(h):
        def scores(jb):
            kb = dk_ref[key_rows(jb), h * LANE:(h + 1) * LANE]
            bias = bias_tile(h, jb) + jnp.where(kk + jb * tk < lim_q, 0.0, NEG)
            return [_dot_t(kb, dq_ref[:, (2 * h + part) * LANE:(2 * h + part + 1) * LANE]) + bias
                    for part in range(2)]

        def finish():
            a1 = acc_ref[0]
            a2 = acc_ref[1]
            od = (a1[0:LANE] / a1[LANE:LANE + 1]
                  - lam_full * (a2[0:LANE] / a2[LANE:LANE + 1]))
            od = od * lax.rsqrt(jnp.mean(od * od, axis=0, keepdims=True) + EPS)
            ot_ref[h * LANE:(h + 1) * LANE, :] = (od * gsub_ref[...]) * (1.0 - LAM_INIT)

        return scores, (lambda n, jb: dv_ref[h, jb]), finish

    groups = ([dsa_group(p) for p in range(N_DSA // N_STAGE)]
              + [diff_group(h) for h in range(N_DIFF)])
    m_prev = None
    for k in range(len(groups) + 1):
        produce = groups[k] if k < len(groups) else None
        consume = groups[k - 1] if k > 0 else None

        def step(jb, mx, k=k, produce=produce, consume=consume, m_prev=m_prev):
            if consume is not None:
                for n in range(N_STAGE):
                    p = jnp.exp2(s_ref[(k - 1) % 2, n, jb] - m_prev[n]).astype(BF16)
                    acc_ref[n] += _dot(consume[1](n, jb), p)
            if produce is None:
                return mx
            out = []
            for n, s in enumerate(produce[0](jb)):
                s_ref[k % 2, n, jb] = s
                out.append(jnp.maximum(mx[n], _fold_rows(s, SUBLANE, jnp.maximum)))
            return tuple(out)

        if consume is not None:
            for n in range(N_STAGE):
                acc_ref[n] = jnp.zeros((V_ROWS, tq), F32)
        mx = over_blocks(step, tuple(jnp.full((SUBLANE, tq), -3e38, F32) for _ in range(N_STAGE)),
                         unroll=4)
        if consume is not None:
            consume[2]()
        m_prev = [jnp.max(x, axis=0, keepdims=True) for x in mx]

    o_ref[...] = ot_ref[...].T.astype(o_ref.dtype)


def _attention(tab, lam4, gsub_t, dq, sq, iq, iw_t, dk, dv_t, sk, sv_t, ik, tq, q_offset):
    B, T, _ = dq.shape
    Lp = dk.shape[1]
    nblk_max = Lp // KEY_BLOCK
    qspec = lambda w: pl.BlockSpec((None, tq, w), lambda b, i: (b, i, 0))
    once = pl.Buffered(1)
    kspec = lambda w: pl.BlockSpec((None, Lp, w), lambda b, i: (b, 0, 0), pipeline_mode=once)
    small = lambda a: pl.BlockSpec(a.shape, lambda b, i: (0,) * a.ndim)
    kern = functools.partial(_attn_kernel, tq=tq, q_offset=q_offset)
    return pl.pallas_call(
        kern,
        grid=(B, T // tq),
        in_specs=[pl.BlockSpec(memory_space=pltpu.SMEM), small(lam4), small(gsub_t),
                  qspec(2 * DIFF_W), qspec(N_DSA * LANE), qspec(N_IDX * 256),
                  pl.BlockSpec((None, SUBLANE, tq), lambda b, i: (b, 0, i)),
                  kspec(DIFF_W),
                  pl.BlockSpec((None, N_DIFF, nblk_max, V_ROWS, KEY_BLOCK),
                               lambda b, i: (b, 0, 0, 0, 0), pipeline_mode=once),
                  kspec(DSA_KV_W),
                  pl.BlockSpec((None, nblk_max, V_ROWS, KEY_BLOCK),
                               lambda b, i: (b, 0, 0, 0), pipeline_mode=once),
                  kspec(256)],
        out_specs=pl.BlockSpec((None, tq, D_MODEL), lambda b, i: (b, i, 0)),
        out_shape=jax.ShapeDtypeStruct((B, T, D_MODEL), BF16),
        name="attn",
        scratch_shapes=[pltpu.VMEM((N_HEADS, N_BIAS_TILES, LANE, LANE), F32),
                        pltpu.VMEM((KEY_BLOCK, KEY_BLOCK), BF16),
                        pltpu.VMEM((nblk_max, KEY_BLOCK, tq), I32),
                        pltpu.VMEM((KEY_BYTES, nblk_max, KEY_BLOCK, tq), BF16),
                        pltpu.VMEM((2, N_STAGE, nblk_max, KEY_BLOCK, tq), F32),
                        pltpu.VMEM((N_STAGE, V_ROWS, tq), F32),
                        pltpu.VMEM((D_MODEL, tq), F32)],
        compiler_params=pltpu.CompilerParams(dimension_semantics=("arbitrary", "arbitrary"),
                                             vmem_limit_bytes=VMEM_LIMIT),
    )(tab, lam4, gsub_t, dq, sq, iq, iw_t, dk, dv_t, sk, sv_t, ik)


FF_CHUNK = D_FF // 2


def _ffn_kernel(x_ref, o_ref, ga1_ref, sc2_ref, sh2_ref, ga2_ref, gpm_ref, gpf_ref, gpo_ref,
                wo_ref, wgu_ref, wd_ref, y_ref):
    bt, tt, _ = x_ref.shape
    rows = bt * tt
    m = _dot(o_ref[...].reshape(rows, D_MODEL), wo_ref[...]).reshape(bt, tt, D_MODEL)
    x1 = x_ref[...] + ga1_ref[...] * (_rms(m) * gpm_ref[...])
    h2 = (_rms(x1) * gpf_ref[...]) * (1.0 + sc2_ref[...]) + sh2_ref[...]
    h2 = h2.reshape(rows, D_MODEL).astype(BF16)
    f = jnp.zeros((rows, D_MODEL), F32)
    for c in range(D_FF // FF_CHUNK):
        g = _dot(h2, wgu_ref[:, c * FF_CHUNK:(c + 1) * FF_CHUNK])
        u = _dot(h2, wgu_ref[:, D_FF + c * FF_CHUNK:D_FF + (c + 1) * FF_CHUNK])
        a = (g * jax.nn.sigmoid(g) * u).astype(BF16)
        f = f + _dot(a, wd_ref[c * FF_CHUNK:(c + 1) * FF_CHUNK, :])
    f = f.reshape(bt, tt, D_MODEL)
    y_ref[...] = x1 + ga2_ref[...] * (_rms(f) * gpo_ref[...])


def _out_ffn(x, o, ga1, sc2, sh2, ga2, gpm, gpf, gpo, wo, wgu, wd, bt, tt):
    B, T, _ = x.shape
    row = pl.BlockSpec((bt, tt, D_MODEL), lambda b, t: (b, t, 0))
    per_b = pl.BlockSpec((bt, 1, D_MODEL), lambda b, t: (b, 0, 0))
    full = lambda a: pl.BlockSpec(a.shape, lambda b, t: (0,) * a.ndim,
                                  pipeline_mode=pl.Buffered(1))
    return pl.pallas_call(
        _ffn_kernel,
        grid=(B // bt, T // tt),
        in_specs=[row, row, per_b, per_b, per_b, per_b, full(gpm), full(gpf), full(gpo),
                  full(wo), full(wgu), full(wd)],
        out_specs=row,
        out_shape=jax.ShapeDtypeStruct((B, T, D_MODEL), F32),
        name="out_ffn",
        compiler_params=pltpu.CompilerParams(dimension_semantics=("arbitrary", "arbitrary"),
                                             vmem_limit_bytes=VMEM_LIMIT),
    )(x, o, ga1, sc2, sh2, ga2, gpm, gpf, gpo, wo, wgu, wd)


def _split_hi_lo(a):
    hi = a.astype(BF16)
    lo = (a - hi.astype(F32)).astype(BF16)
    return hi, lo


def _prep_in_weights(w_in):
    cuts = [0, 512, 1024, 1536, 2048, 2176, 2304, 2560, 2624, 2628]
    wdq, wdk, wdv, wsq, wsk, wsv, wiq, wik, wiw = [w_in[:, a:b] for a, b in zip(cuts[:-1], cuts[1:])]
    zeros64 = jnp.zeros((D_MODEL, DIFF_DH), F32)
    scale = DIFF_DH ** -0.5 * LOG2E
    dq_cols = []
    for h in range(N_DIFF):
        q1 = wdq[:, h * 128:h * 128 + 64] * scale
        q2 = wdq[:, h * 128 + 64:(h + 1) * 128] * scale
        dq_cols += [q1, zeros64, zeros64, q2]
    sq_cols = []
    for h in range(N_DSA):
        q = wsq[:, h * 64:(h + 1) * 64] * scale
        sq_cols += [q, zeros64] if h < DSA_GROUP else [zeros64, q]
    wm = jnp.concatenate(dq_cols + [wdk, wdv] + sq_cols + [wsk, wsv], axis=1).astype(BF16)
    wi = jnp.concatenate([wiq, wik, wiw, jnp.zeros((D_MODEL, IDX_W - 324), F32)], axis=1)
    wih, wil = _split_hi_lo(wi)
    return wm, wih, wil


def _idx_key_operand(ik):
    kh, kl = _split_hi_lo(ik)
    return jnp.concatenate([kh, kl, kh, jnp.zeros_like(kh)], axis=-1)


def _pad_axis1(a, n):
    return jnp.pad(a, ((0, 0), (0, n - a.shape[1])) + ((0, 0),) * (a.ndim - 2))


def _value_operand(v):
    B, L, H, W = v.shape
    vt = jnp.transpose(v, (0, 2, 3, 1))
    extra = jnp.concatenate([jnp.ones((B, H, 1, L), BF16),
                             jnp.zeros((B, H, V_ROWS - W - 1, L), BF16)], axis=2)
    vt = jnp.concatenate([vt, extra], axis=2).reshape(B, H, V_ROWS, L // KEY_BLOCK, KEY_BLOCK)
    return jnp.transpose(vt, (0, 1, 3, 2, 4))


def _group(x, mods, past, params, bt, tt, tq):
    (g_pre_mix, g_post_mix, g_pre_ffn, g_post_ffn, wm, wih, wil, lam4, g_subln,
     wo, wgu, wd, tab) = params
    B, T, _ = x.shape
    sh1, sc1, ga1, sh2, sc2, ga2 = mods
    (dk32, dv32, sk32, sv32, ik32, iw32, dq16, dk16, sq16, sk16, iq16, ik16, dvx, svx) = _projection(
        x, sc1, sh1, g_pre_mix, wm, wih, wil, bt, tt, transpose_values=past is None)
    iw = iw32[..., IDX_DH:IDX_DH + N_IDX] * (1.0 / 16.0)
    if past is None:
        q_offset = 0
        dk, dv_t, sk, sv_t, ikc = dk16, dvx, sk16, svx, ik16
    else:
        pk, pv, psk, psv, pik = past
        P = pk.shape[1]
        q_offset = P
        lp = -(-(P + T) // KEY_BLOCK) * KEY_BLOCK
        cat = lambda p, n: _pad_axis1(
            jnp.concatenate([p.reshape(B, P, -1).astype(BF16), n], axis=1), lp)
        dk, dv, sk, sv = cat(pk, dk16), cat(pv, dvx), cat(psk, sk16), cat(psv, svx)
        ikc = _pad_axis1(jnp.concatenate([_idx_key_operand(pik), ik16], axis=1), lp)
        dv_t = _value_operand(dv.reshape(B, lp, N_DIFF, 2 * DIFF_DH))
        sv_t = _value_operand(sv.reshape(B, lp, 1, DSA_KV_W))[:, 0]
    tpad = -(-T // tq) * tq
    qpad = lambda a: _pad_axis1(a, tpad)
    iw_t = _pad_axis1(jnp.transpose(qpad(iw), (0, 2, 1)), SUBLANE)
    gsub_t = jnp.broadcast_to(g_subln.reshape(-1, 1), (2 * DIFF_DH, tq))
    o = _attention(tab, lam4, gsub_t, qpad(dq16), qpad(sq16), qpad(iq16), iw_t,
                   dk, dv_t, sk, sv_t, ikc, tq, q_offset)[:, :T]
    y = _out_ffn(x, o, ga1, sc2, sh2, ga2, g_post_mix, g_pre_ffn, g_post_ffn, wo, wgu, wd, bt, tt)
    rows = (dk32.reshape(1, B, T, N_DIFF, 2 * DIFF_DH), dv32.reshape(1, B, T, N_DIFF, 2 * DIFF_DH),
            sk32.reshape(1, B, T, N_DSA_KV, DSA_DH), sv32.reshape(1, B, T, N_DSA_KV, DSA_DH),
            ik32.reshape(1, B, T, IDX_DH))
    return y, rows


def kernel(x_prompt, x_sample, cache_diff_k, cache_diff_v, cache_dsa_k, cache_dsa_v, cache_dsa_kidx,
           c_prompt, c_sample, w_ada, b_ada, g_pre_mix, g_post_mix, g_pre_ffn, g_post_ffn, w_in,
           lambda_q1, lambda_k1, lambda_q2, lambda_k2, g_subln, w_out, w_gate_up, w_down,
           rel_bias_table):
    Bp = x_prompt.shape[0]
    c_all = jnp.concatenate([c_prompt, c_sample], axis=0)
    mod = _modulation(c_all, w_ada[0], b_ada[0])
    mods = [mod[:, j * D_MODEL:(j + 1) * D_MODEL][:, None, :] for j in range(6)]
    mods_p = [m[:Bp] for m in mods]
    mods_s = [m[Bp:] for m in mods]

    wm, wih, wil = _prep_in_weights(w_in[0])
    lam4 = jnp.concatenate([lambda_q1, lambda_k1, lambda_q2, lambda_k2], axis=0)
    row = lambda g: g[0].reshape(1, -1)
    params = (row(g_pre_mix), row(g_post_mix), row(g_pre_ffn), row(g_post_ffn), wm, wih, wil,
              lam4, g_subln[0], w_out[0].astype(BF16), w_gate_up[0].astype(BF16),
              w_down[0].astype(BF16), rel_bias_table)

    yp, rp = _group(x_prompt, mods_p, None, params, bt=1, tt=512, tq=256)
    past = (cache_diff_k[0], cache_diff_v[0], cache_dsa_k[0], cache_dsa_v[0], cache_dsa_kidx[0])
    ys, rs = _group(x_sample, mods_s, past, params, bt=8, tt=64, tq=128)
    return (yp, ys) + rp + rs
```

```python
import functools
import math

import jax
import jax.numpy as jnp
from jax import lax
from jax.experimental import pallas as pl
from jax.experimental.pallas import tpu as pltpu

F32 = jnp.float32
BF16 = jnp.bfloat16
I32 = jnp.int32

D_MODEL = 1024
CHUNK = 64
N_DIFF = 4
DIFF_DH = 64
DIFF_W = N_DIFF * 2 * DIFF_DH
N_DSA = 8
N_DSA_KV = 2
DSA_GROUP = N_DSA // N_DSA_KV
DSA_DH = 64
DSA_W = N_DSA * DSA_DH
DSA_KV_W = N_DSA_KV * DSA_DH
N_IDX = 4
IDX_DH = 64
TOPK = 256
N_HEADS = N_DIFF + N_DSA
NUM_BUCKETS = 32
D_FF = 2816
EPS = 1e-6
NEG = -1e30
LOG2E = math.log2(math.e)
LAM_INIT = 0.8 - 0.6 * math.exp(-0.3 * 0)

LANE = 128
SUBLANE = 8
KEY_BLOCK = 256
V_ROWS = 144
VMEM_LIMIT = 56 * 1024 * 1024
INT_MIN = -2 ** 31
FAR_BUCKET = 15
BUCKET_STEPS = (13, 20, 30, 46, 70, 108, 166)
KEY_BYTES = 4
N_BIAS_TILES = 4
N_STAGE = 2


def _rms(x):
    return x * lax.rsqrt(jnp.mean(x * x, axis=-1, keepdims=True) + EPS)


def _dot(a, b):
    return jnp.dot(a, b, preferred_element_type=F32)


def _dot_t(a, b):
    return lax.dot_general(a, b, (((1,), (1,)), ((), ())), preferred_element_type=F32)


def _mod_kernel(c_ref, w_ref, b_ref, o_ref):
    c = c_ref[...]
    s = c * jax.nn.sigmoid(c)
    o_ref[...] = jnp.dot(s, w_ref[...], preferred_element_type=F32,
                         precision=lax.Precision.HIGHEST) + b_ref[...]


def _modulation(c_all, w_ada, b_ada):
    n = c_all.shape[0]
    tn = 1024
    return pl.pallas_call(
        _mod_kernel,
        grid=(6 * D_MODEL // tn,),
        in_specs=[pl.BlockSpec((n, D_MODEL), lambda j: (0, 0)),
                  pl.BlockSpec((D_MODEL, tn), lambda j: (0, j)),
                  pl.BlockSpec((1, tn), lambda j: (0, j))],
        out_specs=pl.BlockSpec((n, tn), lambda j: (0, j)),
        out_shape=jax.ShapeDtypeStruct((n, 6 * D_MODEL), F32),
        name="adaln_mod",
        compiler_params=pltpu.CompilerParams(dimension_semantics=("arbitrary",),
                                             vmem_limit_bytes=VMEM_LIMIT),
    )(c_all, w_ada, b_ada.reshape(1, -1))


_C_DQ = 0
_C_DK = _C_DQ + 2 * DIFF_W
_C_DV = _C_DK + DIFF_W
_C_SQ = _C_DV + DIFF_W
_C_SK = _C_SQ + N_DSA * LANE
_C_SV = _C_SK + DSA_KV_W
_C_END = _C_SV + DSA_KV_W
IDX_W = 384


def _hi_lo(a):
    hi = a.astype(BF16)
    return hi, (a - hi.astype(F32)).astype(BF16)


def _value_rows(v):
    row = lax.broadcasted_iota(I32, (V_ROWS - LANE, KEY_BLOCK), 0)
    extra = jnp.where(row == 0, 1.0, 0.0)
    return jnp.concatenate([v.T, extra], axis=0).astype(BF16)


def _proj_kernel(x_ref, sc_ref, sh_ref, g_ref, wm_ref, wih_ref, wil_ref,
                 dk32, dv32, sk32, sv32, ik32, iw32, dq16, dk16, sq16, sk16, iq16, ik16, dvx, svx,
                 *, transpose_values):
    bt, tt, _ = x_ref.shape
    x = x_ref[...]
    h = (_rms(x) * g_ref[...]) * (1.0 + sc_ref[...]) + sh_ref[...]
    h = h.reshape(bt * tt, D_MODEL)
    hb, hl = _hi_lo(h)
    z = _dot(hb, wm_ref[...])
    wih = wih_ref[...]
    zi = _dot(hb, wih) + _dot(hb, wil_ref[...]) + _dot(hl, wih)

    def put(ref, v):
        ref[...] = v.reshape(bt, tt, v.shape[-1]).astype(ref.dtype)

    zdk = z[:, _C_DK:_C_DV]
    zdv = z[:, _C_DV:_C_SQ]
    zsk = z[:, _C_SK:_C_SV]
    zsv = z[:, _C_SV:_C_END]
    zik = zi[:, 256:256 + IDX_DH]
    for n in range(N_DIFF):
        dk32[:, :, n, :] = zdk[:, n * LANE:(n + 1) * LANE].reshape(bt, tt, LANE)
        dv32[:, :, n, :] = zdv[:, n * LANE:(n + 1) * LANE].reshape(bt, tt, LANE)
    for g in range(N_DSA_KV):
        sk32[:, :, g, :] = zsk[:, g * DSA_DH:(g + 1) * DSA_DH].reshape(bt, tt, DSA_DH)
        sv32[:, :, g, :] = zsv[:, g * DSA_DH:(g + 1) * DSA_DH].reshape(bt, tt, DSA_DH)
    put(ik32, zik)
    put(iw32, zi[:, 256:IDX_W])
    put(dq16, z[:, _C_DQ:_C_DK])
    put(dk16, zdk)
    put(sq16, z[:, _C_SQ:_C_SK])
    put(sk16, zsk)

    qh, ql = _hi_lo(zi[:, 0:N_IDX * IDX_DH])
    kh, kl = _hi_lo(zik)
    pad = jnp.zeros_like(kh)
    head = lambda a, n: a[:, n * IDX_DH:(n + 1) * IDX_DH]
    put(iq16, jnp.concatenate([p for n in range(N_IDX)
                               for p in (head(qh, n), head(qh, n), head(ql, n), pad)], axis=1))
    put(ik16, jnp.concatenate([kh, kl, kh, pad], axis=1))

    if transpose_values:
        for kb in range(tt // KEY_BLOCK):
            rows = slice(kb * KEY_BLOCK, (kb + 1) * KEY_BLOCK)
            for n in range(N_DIFF):
                dvx[n, kb] = _value_rows(zdv[rows, n * LANE:(n + 1) * LANE])
            svx[kb] = _value_rows(zsv[rows, :])
    else:
        put(dvx, zdv)
        put(svx, zsv)


def _projection(x, sc, sh, g, wm, wih, wil, bt, tt, transpose_values):
    B, T, _ = x.shape
    grid = (B // bt, T // tt)
    row = lambda w: pl.BlockSpec((bt, tt, w), lambda b, t: (b, t, 0))
    per_b = pl.BlockSpec((bt, 1, D_MODEL), lambda b, t: (b, 0, 0))
    full = lambda a: pl.BlockSpec(a.shape, lambda b, t: (0,) * a.ndim,
                                  pipeline_mode=pl.Buffered(1))
    outs = [(DIFF_W, F32), (DIFF_W, F32), (DSA_KV_W, F32), (DSA_KV_W, F32), (IDX_DH, F32),
            (LANE, F32), (2 * DIFF_W, BF16), (DIFF_W, BF16), (N_DSA * LANE, BF16),
            (DSA_KV_W, BF16), (N_IDX * 256, BF16), (256, BF16)]
    out_specs = [row(w) for w, _ in outs]
    out_shape = [jax.ShapeDtypeStruct((B, T, w), dt) for w, dt in outs]
    for j, (heads, width) in enumerate(2 * [(N_DIFF, 2 * DIFF_DH)] + 2 * [(N_DSA_KV, DSA_DH)]):
        out_specs[j] = pl.BlockSpec((bt, tt, heads, width), lambda b, t: (b, t, 0, 0))
        out_shape[j] = jax.ShapeDtypeStruct((B, T, heads, width), F32)
    if transpose_values:
        assert bt == 1 and tt % KEY_BLOCK == 0
        nb = tt // KEY_BLOCK
        out_specs += [pl.BlockSpec((None, N_DIFF, nb, V_ROWS, KEY_BLOCK), lambda b, t: (b, 0, t, 0, 0)),
                      pl.BlockSpec((None, nb, V_ROWS, KEY_BLOCK), lambda b, t: (b, t, 0, 0))]
        out_shape += [jax.ShapeDtypeStruct((B, N_DIFF, T // KEY_BLOCK, V_ROWS, KEY_BLOCK), BF16),
                      jax.ShapeDtypeStruct((B, T // KEY_BLOCK, V_ROWS, KEY_BLOCK), BF16)]
    else:
        out_specs += [row(DIFF_W), row(DSA_KV_W)]
        out_shape += [jax.ShapeDtypeStruct((B, T, DIFF_W), BF16),
                      jax.ShapeDtypeStruct((B, T, DSA_KV_W), BF16)]
    return pl.pallas_call(
        functools.partial(_proj_kernel, transpose_values=transpose_values),
        grid=grid,
        in_specs=[row(D_MODEL), per_b, per_b, full(g), full(wm), full(wih), full(wil)],
        out_specs=out_specs,
        out_shape=out_shape,
        name="in_proj",
        compiler_params=pltpu.CompilerParams(dimension_semantics=("arbitrary", "arbitrary"),
                                             vmem_limit_bytes=VMEM_LIMIT),
    )(x, sc, sh, g, wm, wih, wil)


def _fold_rows(x, rows, op):
    acc = x[0:rows]
    for r in range(1, x.shape[0] // rows):
        acc = op(acc, x[r * rows:(r + 1) * rows])
    return acc


def _init_bias_tiles(tab_ref, tb_ref):
    k = lax.broadcasted_iota(I32, (LANE, LANE), 0)
    q = lax.broadcasted_iota(I32, (LANE, LANE), 1)
    buckets = []
    for t in range(N_BIAS_TILES - 1):
        rel = k - q - LANE * t
        n = jnp.abs(rel)
        large = jnp.full_like(n, 8)
        for step in BUCKET_STEPS:
            large = large + jnp.where(n >= step, 1, 0)
        buckets.append(jnp.where(rel > 0, NUM_BUCKETS // 2, 0) + jnp.where(n < 8, n, large))

    def per_head(h, carry):
        for t, bucket in enumerate(buckets):
            val = jnp.zeros((LANE, LANE), F32)
            for b in range(NUM_BUCKETS):
                val = jnp.where(bucket == b, tab_ref[b, h] * LOG2E, val)
            tb_ref[h, t] = val
        tb_ref[h, N_BIAS_TILES - 1] = jnp.full((LANE, LANE), tab_ref[FAR_BUCKET, h] * LOG2E, F32)
        return carry

    lax.fori_loop(0, N_HEADS, per_head, 0)


def _attn_kernel(tab_ref, lam_ref, gsub_ref, dq_ref, sq_ref, iq_ref, iw_ref,
                 dk_ref, dv_ref, sk_ref, sv_ref, ik_ref, o_ref,
                 tb_ref, ltri_ref, key_ref, byte_ref, s_ref, acc_ref, ot_ref, *, tq, q_offset):
    tk = KEY_BLOCK
    b = pl.program_id(0)
    i = pl.program_id(1)

    @pl.when((b == 0) & (i == 0))
    def _():
        _init_bias_tiles(tab_ref, tb_ref)
        ltri_ref[...] = jnp.where(lax.broadcasted_iota(I32, (tk, tk), 1)
                                  < lax.broadcasted_iota(I32, (tk, tk), 0), 1.0, 0.0).astype(BF16)

    qpos0 = q_offset + i * tq
    lim_last = CHUNK * ((qpos0 + tq - 1) // CHUNK + 1)
    nblk = (lim_last + tk - 1) // tk
    qq = lax.broadcasted_iota(I32, (1, tq), 1)
    lim_q = CHUNK * ((qpos0 + qq) // CHUNK + 1)
    kk = lax.broadcasted_iota(I32, (tk, tq), 0)

    def key_rows(jb):
        return pl.ds(pl.multiple_of(jb * tk, tk), tk)

    def over_blocks(body, init, unroll=2):
        def run(first, n, carry):
            for u in range(n):
                carry = body(first + u, carry)
            return carry
        carry = lax.fori_loop(0, nblk // unroll, lambda jp, c: run(unroll * jp, unroll, c), init)
        n = unroll // 2
        while n >= 1:
            first = (nblk // (2 * n)) * (2 * n)
            carry = lax.fori_loop(0, (nblk // n) % 2, lambda _, c, first=first, n=n: run(first, n, c),
                                  carry)
            n //= 2
        return carry

    def bias_tile(head, jb):
        rows = []
        for a in range(tk // LANE):
            parts = []
            for c in range(tq // LANE):
                t = jnp.clip((qpos0 + c * LANE - (jb * tk + a * LANE)) // LANE, 0, N_BIAS_TILES - 1)
                parts.append(tb_ref[head, t])
            rows.append(jnp.concatenate(parts, axis=1))
        return jnp.concatenate(rows, axis=0)

    iw = iw_ref[...]

    def score_block(jb, carry):
        kc = ik_ref[key_rows(jb), :]
        sc = jnp.zeros((tk, tq), F32)
        for h in range(N_IDX):
            d = _dot_t(kc, iq_ref[:, h * 256:(h + 1) * 256])
            sc = sc + jnp.maximum(d, 0.0) * iw[h:h + 1]
        sc = jnp.where(kk + jb * tk < lim_q, sc, NEG)
        bits = pltpu.bitcast(sc, I32)
        key = bits ^ ((bits >> 31) & 0x7FFFFFFF)
        key_ref[jb] = key
        for lvl in range(KEY_BYTES):
            byte = ((key >> (8 * lvl)) & 0xFF) if lvl < KEY_BYTES - 1 else (key >> 24) + 128
            byte_ref[lvl, jb] = byte.astype(F32).astype(BF16)
        return carry

    over_blocks(score_block, 0)

    one16 = jnp.ones((), BF16)
    zero16 = jnp.zeros((), BF16)

    def count_digits(lvl, cand):
        def body(jb, acc):
            hit = jnp.where(byte_ref[lvl, jb] >= cand, one16, zero16)
            return acc + _fold_rows(hit, 2 * SUBLANE, jnp.add).astype(F32)
        acc = over_blocks(body, jnp.zeros((2 * SUBLANE, tq), F32))
        return jnp.sum(acc, axis=0, keepdims=True)

    want = jnp.full((1, tq), float(TOPK), F32)
    thr = jnp.zeros((1, tq), I32)
    for lvl in reversed(range(KEY_BYTES)):
        def digit_bit(it, state, lvl=lvl, want=want):
            digit, above = state
            cand = digit + jnp.left_shift(jnp.int32(1), 7 - it).astype(F32)
            cnt = count_digits(lvl, cand.astype(BF16))
            ok = cnt >= want
            return jnp.where(ok, cand, digit), jnp.where(ok, above, cnt)

        digit, above = lax.fori_loop(0, 8, digit_bit,
                                     (jnp.zeros((1, tq), F32), jnp.zeros((1, tq), F32)))
        want = want - above
        digit_i = digit.astype(I32)
        thr = thr | ((digit_i - 128) << 24 if lvl == KEY_BYTES - 1 else digit_i << (8 * lvl))
        if lvl > 0:
            def narrow(jb, carry, lvl=lvl, digit16=digit.astype(BF16)):
                byte_ref[lvl - 1, jb] = jnp.where(byte_ref[lvl, jb] == digit16,
                                                  byte_ref[lvl - 1, jb], -one16)
                return carry
            over_blocks(narrow, 0)

    def mask_block(jb, before):
        kb = key_ref[jb]
        eq = jnp.where(kb == thr, 1.0, 0.0)
        rank = _dot(ltri_ref[...], eq.astype(BF16)) + before
        keep = jnp.where(rank < want, eq, 0.0)
        sel = jnp.where(kb > thr, 0.0, (keep - 1.0) * (-NEG))
        sel = jnp.where(kk + jb * tk < lim_q, sel, NEG)
        key_ref[jb] = pltpu.bitcast(sel, I32)
        return before + jnp.sum(_fold_rows(eq, SUBLANE, jnp.add), axis=0, keepdims=True)

    over_blocks(mask_block, jnp.zeros((1, tq), F32))

    lam = lam_ref[...]
    lam_full = (jnp.exp(jnp.sum(lam[0:1] * lam[1:2], axis=1, keepdims=True))
                - jnp.exp(jnp.sum(lam[2:3] * lam[3:4], axis=1, keepdims=True)) + LAM_INIT)

    def dsa_group(pair):
        heads = [N_STAGE * pair + n for n in range(N_STAGE)]

        def scores(jb):
            kb = sk_ref[key_rows(jb), :]
            sel = pltpu.bitcast(key_ref[jb], F32)
            return [_dot_t(kb, sq_ref[:, h * LANE:(h + 1) * LANE]) + bias_tile(N_DIFF + h, jb) + sel
                    for h in heads]

        def finish():
            for n, h in enumerate(heads):
                g = h // DSA_GROUP
                a = acc_ref[n]
                out = a[g * DSA_DH:(g + 1) * DSA_DH] / a[2 * DSA_DH:2 * DSA_DH + 1]
                ot_ref[DIFF_W + h * DSA_DH:DIFF_W + (h + 1) * DSA_DH, :] = out

        return scores, (lambda n, jb: sv_ref[jb]), finish

    def diff_group(h):
        def scores(jb):
            kb = dk_ref[key_rows(jb), h * LANE:(h + 1) * LANE]
            bias = bias_tile(h, jb) + jnp.where(kk + jb * tk < lim_q, 0.0, NEG)
            return [_dot_t(kb, dq_ref[:, (2 * h + part) * LANE:(2 * h + part + 1) * LANE]) + bias
                    for part in range(2)]

        def finish():
            a1 = acc_ref[0]
            a2 = acc_ref[1]
            od = (a1[0:LANE] / a1[LANE:LANE + 1]
                  - lam_full * (a2[0:LANE] / a2[LANE:LANE + 1]))
            od = od * lax.rsqrt(jnp.mean(od * od, axis=0, keepdims=True) + EPS)
            ot_ref[h * LANE:(h + 1) * LANE, :] = (od * gsub_ref[...]) * (1.0 - LAM_INIT)

        return scores, (lambda n, jb: dv_ref[h, jb]), finish

    groups = ([dsa_group(p) for p in range(N_DSA // N_STAGE)]
              + [diff_group(h) for h in range(N_DIFF)])
    m_prev = None
    for k in range(len(groups) + 1):
        produce = groups[k] if k < len(groups) else None
        consume = groups[k - 1] if k > 0 else None

        def step(jb, mx, k=k, produce=produce, consume=consume, m_prev=m_prev):
            if consume is not None:
                for n in range(N_STAGE):
                    p = jnp.exp2(s_ref[(k - 1) % 2, n, jb] - m_prev[n]).astype(BF16)
                    acc_ref[n] += _dot(consume[1](n, jb), p)
            if produce is None:
                return mx
            out = []
            for n, s in enumerate(produce[0](jb)):
                s_ref[k % 2, n, jb] = s
                out.append(jnp.maximum(mx[n], _fold_rows(s, SUBLANE, jnp.maximum)))
            return tuple(out)

        if consume is not None:
            for n in range(N_STAGE):
                acc_ref[n] = jnp.zeros((V_ROWS, tq), F32)
        mx = over_blocks(step, tuple(jnp.full((SUBLANE, tq), -3e38, F32) for _ in range(N_STAGE)),
                         unroll=4)
        if consume is not None:
            consume[2]()
        m_prev = [jnp.max(x, axis=0, keepdims=True) for x in mx]

    o_ref[...] = ot_ref[...].T.astype(o_ref.dtype)


def _attention(tab, lam4, gsub_t, dq, sq, iq, iw_t, dk, dv_t, sk, sv_t, ik, tq, q_offset):
    B, T, _ = dq.shape
    Lp = dk.shape[1]
    nblk_max = Lp // KEY_BLOCK
    qspec = lambda w: pl.BlockSpec((None, tq, w), lambda b, i: (b, i, 0))
    once = pl.Buffered(1)
    kspec = lambda w: pl.BlockSpec((None, Lp, w), lambda b, i: (b, 0, 0), pipeline_mode=once)
    small = lambda a: pl.BlockSpec(a.shape, lambda b, i: (0,) * a.ndim)
    kern = functools.partial(_attn_kernel, tq=tq, q_offset=q_offset)
    return pl.pallas_call(
        kern,
        grid=(B, T // tq),
        in_specs=[pl.BlockSpec(memory_space=pltpu.SMEM), small(lam4), small(gsub_t),
                  qspec(2 * DIFF_W), qspec(N_DSA * LANE), qspec(N_IDX * 256),
                  pl.BlockSpec((None, SUBLANE, tq), lambda b, i: (b, 0, i)),
                  kspec(DIFF_W),
                  pl.BlockSpec((None, N_DIFF, nblk_max, V_ROWS, KEY_BLOCK),
                               lambda b, i: (b, 0, 0, 0, 0), pipeline_mode=once),
                  kspec(DSA_KV_W),
                  pl.BlockSpec((None, nblk_max, V_ROWS, KEY_BLOCK),
                               lambda b, i: (b, 0, 0, 0), pipeline_mode=once),
                  kspec(256)],
        out_specs=pl.BlockSpec((None, tq, D_MODEL), lambda b, i: (b, i, 0)),
        out_shape=jax.ShapeDtypeStruct((B, T, D_MODEL), BF16),
        name="attn",
        scratch_shapes=[pltpu.VMEM((N_HEADS, N_BIAS_TILES, LANE, LANE), F32),
                        pltpu.VMEM((KEY_BLOCK, KEY_BLOCK), BF16),
                        pltpu.VMEM((nblk_max, KEY_BLOCK, tq), I32),
                        pltpu.VMEM((KEY_BYTES, nblk_max, KEY_BLOCK, tq), BF16),
                        pltpu.VMEM((2, N_STAGE, nblk_max, KEY_BLOCK, tq), F32),
                        pltpu.VMEM((N_STAGE, V_ROWS, tq), F32),
                        pltpu.VMEM((D_MODEL, tq), F32)],
        compiler_params=pltpu.CompilerParams(dimension_semantics=("arbitrary", "arbitrary"),
                                             vmem_limit_bytes=VMEM_LIMIT),
    )(tab, lam4, gsub_t, dq, sq, iq, iw_t, dk, dv_t, sk, sv_t, ik)


FF_CHUNK = D_FF // 2


def _ffn_kernel(x_ref, o_ref, ga1_ref, sc2_ref, sh2_ref, ga2_ref, gpm_ref, gpf_ref, gpo_ref,
                wo_ref, wgu_ref, wd_ref, y_ref):
    bt, tt, _ = x_ref.shape
    rows = bt * tt
    m = _dot(o_ref[...].reshape(rows, D_MODEL), wo_ref[...]).reshape(bt, tt, D_MODEL)
    x1 = x_ref[...] + ga1_ref[...] * (_rms(m) * gpm_ref[...])
    h2 = (_rms(x1) * gpf_ref[...]) * (1.0 + sc2_ref[...]) + sh2_ref[...]
    h2 = h2.reshape(rows, D_MODEL).astype(BF16)
    f = jnp.zeros((rows, D_MODEL), F32)
    for c in range(D_FF // FF_CHUNK):
        g = _dot(h2, wgu_ref[:, c * FF_CHUNK:(c + 1) * FF_CHUNK])
        u = _dot(h2, wgu_ref[:, D_FF + c * FF_CHUNK:D_FF + (c + 1) * FF_CHUNK])
        a = (g * jax.nn.sigmoid(g) * u).astype(BF16)
        f = f + _dot(a, wd_ref[c * FF_CHUNK:(c + 1) * FF_CHUNK, :])
    f = f.reshape(bt, tt, D_MODEL)
    y_ref[...] = x1 + ga2_ref[...] * (_rms(f) * gpo_ref[...])


def _out_ffn(x, o, ga1, sc2, sh2, ga2, gpm, gpf, gpo, wo, wgu, wd, bt, tt):
    B, T, _ = x.shape
    row = pl.BlockSpec((bt, tt, D_MODEL), lambda b, t: (b, t, 0))
    per_b = pl.BlockSpec((bt, 1, D_MODEL), lambda b, t: (b, 0, 0))
    full = lambda a: pl.BlockSpec(a.shape, lambda b, t: (0,) * a.ndim,
                                  pipeline_mode=pl.Buffered(1))
    return pl.pallas_call(
        _ffn_kernel,
        grid=(B // bt, T // tt),
        in_specs=[row, row, per_b, per_b, per_b, per_b, full(gpm), full(gpf), full(gpo),
                  full(wo), full(wgu), full(wd)],
        out_specs=row,
        out_shape=jax.ShapeDtypeStruct((B, T, D_MODEL), F32),
        name="out_ffn",
        compiler_params=pltpu.CompilerParams(dimension_semantics=("arbitrary", "arbitrary"),
                                             vmem_limit_bytes=VMEM_LIMIT),
    )(x, o, ga1, sc2, sh2, ga2, gpm, gpf, gpo, wo, wgu, wd)


def _split_hi_lo(a):
    hi = a.astype(BF16)
    lo = (a - hi.astype(F32)).astype(BF16)
    return hi, lo


def _prep_in_weights(w_in):
    cuts = [0, 512, 1024, 1536, 2048, 2176, 2304, 2560, 2624, 2628]
    wdq, wdk, wdv, wsq, wsk, wsv, wiq, wik, wiw = [w_in[:, a:b] for a, b in zip(cuts[:-1], cuts[1:])]
    zeros64 = jnp.zeros((D_MODEL, DIFF_DH), F32)
    scale = DIFF_DH ** -0.5 * LOG2E
    dq_cols = []
    for h in range(N_DIFF):
        q1 = wdq[:, h * 128:h * 128 + 64] * scale
        q2 = wdq[:, h * 128 + 64:(h + 1) * 128] * scale
        dq_cols += [q1, zeros64, zeros64, q2]
    sq_cols = []
    for h in range(N_DSA):
        q = wsq[:, h * 64:(h + 1) * 64] * scale
        sq_cols += [q, zeros64] if h < DSA_GROUP else [zeros64, q]
    wm = jnp.concatenate(dq_cols + [wdk, wdv] + sq_cols + [wsk, wsv], axis=1).astype(BF16)
    wi = jnp.concatenate([wiq, wik, wiw, jnp.zeros((D_MODEL, IDX_W - 324), F32)], axis=1)
    wih, wil = _split_hi_lo(wi)
    return wm, wih, wil


def _idx_key_operand(ik):
    kh, kl = _split_hi_lo(ik)
    return jnp.concatenate([kh, kl, kh, jnp.zeros_like(kh)], axis=-1)


def _pad_axis1(a, n):
    return jnp.pad(a, ((0, 0), (0, n - a.shape[1])) + ((0, 0),) * (a.ndim - 2))


def _value_operand(v):
    B, L, H, W = v.shape
    vt = jnp.transpose(v, (0, 2, 3, 1))
    extra = jnp.concatenate([jnp.ones((B, H, 1, L), BF16),
                             jnp.zeros((B, H, V_ROWS - W - 1, L), BF16)], axis=2)
    vt = jnp.concatenate([vt, extra], axis=2).reshape(B, H, V_ROWS, L // KEY_BLOCK, KEY_BLOCK)
    return jnp.transpose(vt, (0, 1, 3, 2, 4))


def _group(x, mods, past, params, bt, tt, tq):
    (g_pre_mix, g_post_mix, g_pre_ffn, g_post_ffn, wm, wih, wil, lam4, g_subln,
     wo, wgu, wd, tab) = params
    B, T, _ = x.shape
    sh1, sc1, ga1, sh2, sc2, ga2 = mods
    (dk32, dv32, sk32, sv32, ik32, iw32, dq16, dk16, sq16, sk16, iq16, ik16, dvx, svx) = _projection(
        x, sc1, sh1, g_pre_mix, wm, wih, wil, bt, tt, transpose_values=past is None)
    iw = iw32[..., IDX_DH:IDX_DH + N_IDX] * (1.0 / 16.0)
    if past is None:
        q_offset = 0
        dk, dv_t, sk, sv_t, ikc = dk16, dvx, sk16, svx, ik16
    else:
        pk, pv, psk, psv, pik = past
        P = pk.shape[1]
        q_offset = P
        lp = -(-(P + T) // KEY_BLOCK) * KEY_BLOCK
        cat = lambda p, n: _pad_axis1(
            jnp.concatenate([p.reshape(B, P, -1).astype(BF16), n], axis=1), lp)
        dk, dv, sk, sv = cat(pk, dk16), cat(pv, dvx), cat(psk, sk16), cat(psv, svx)
        ikc = _pad_axis1(jnp.concatenate([_idx_key_operand(pik), ik16], axis=1), lp)
        dv_t = _value_operand(dv.reshape(B, lp, N_DIFF, 2 * DIFF_DH))
        sv_t = _value_operand(sv.reshape(B, lp, 1, DSA_KV_W))[:, 0]
    tpad = -(-T // tq) * tq
    qpad = lambda a: _pad_axis1(a, tpad)
    iw_t = _pad_axis1(jnp.transpose(qpad(iw), (0, 2, 1)), SUBLANE)
    gsub_t = jnp.broadcast_to(g_subln.reshape(-1, 1), (2 * DIFF_DH, tq))
    o = _attention(tab, lam4, gsub_t, qpad(dq16), qpad(sq16), qpad(iq16), iw_t,
                   dk, dv_t, sk, sv_t, ikc, tq, q_offset)[:, :T]
    y = _out_ffn(x, o, ga1, sc2, sh2, ga2, g_post_mix, g_pre_ffn, g_post_ffn, wo, wgu, wd, bt, tt)
    rows = (dk32[None], dv32[None],
            sk32[None], sv32[None],
            ik32.reshape(1, B, T, IDX_DH))
    return y, rows


def kernel(x_prompt, x_sample, cache_diff_k, cache_diff_v, cache_dsa_k, cache_dsa_v, cache_dsa_kidx,
           c_prompt, c_sample, w_ada, b_ada, g_pre_mix, g_post_mix, g_pre_ffn, g_post_ffn, w_in,
           lambda_q1, lambda_k1, lambda_q2, lambda_k2, g_subln, w_out, w_gate_up, w_down,
           rel_bias_table):
    Bp = x_prompt.shape[0]
    c_all = jnp.concatenate([c_prompt, c_sample], axis=0)
    mod = _modulation(c_all, w_ada[0], b_ada[0])
    mods = [mod[:, j * D_MODEL:(j + 1) * D_MODEL][:, None, :] for j in range(6)]
    mods_p = [m[:Bp] for m in mods]
    mods_s = [m[Bp:] for m in mods]

    wm, wih, wil = _prep_in_weights(w_in[0])
    lam4 = jnp.concatenate([lambda_q1, lambda_k1, lambda_q2, lambda_k2], axis=0)
    row = lambda g: g[0].reshape(1, -1)
    params = (row(g_pre_mix), row(g_post_mix), row(g_pre_ffn), row(g_post_ffn), wm, wih, wil,
              lam4, g_subln[0], w_out[0].astype(BF16), w_gate_up[0].astype(BF16),
              w_down[0].astype(BF16), rel_bias_table)

    yp, rp = _group(x_prompt, mods_p, None, params, bt=1, tt=512, tq=256)
    past = (cache_diff_k[0], cache_diff_v[0], cache_dsa_k[0], cache_dsa_v[0], cache_dsa_kidx[0])
    ys, rs = _group(x_sample, mods_s, past, params, bt=8, tt=64, tq=128)
    return (yp, ys) + rp + rs
```

```python
import functools
import math

import jax
import jax.numpy as jnp
from jax import lax
from jax.experimental import pallas as pl
from jax.experimental.pallas import tpu as pltpu

F32 = jnp.float32
BF16 = jnp.bfloat16
I32 = jnp.int32

D_MODEL = 1024
CHUNK = 64
N_DIFF = 4
DIFF_DH = 64
DIFF_W = N_DIFF * 2 * DIFF_DH
N_DSA = 8
N_DSA_KV = 2
DSA_GROUP = N_DSA // N_DSA_KV
DSA_DH = 64
DSA_W = N_DSA * DSA_DH
DSA_KV_W = N_DSA_KV * DSA_DH
N_IDX = 4
IDX_DH = 64
TOPK = 256
N_HEADS = N_DIFF + N_DSA
NUM_BUCKETS = 32
D_FF = 2816
EPS = 1e-6
NEG = -1e30
LOG2E = math.log2(math.e)
LAM_INIT = 0.8 - 0.6 * math.exp(-0.3 * 0)

LANE = 128
SUBLANE = 8
KEY_BLOCK = 256
V_PAD = 16
V_ROWS = 2 * DIFF_DH + V_PAD
VMEM_LIMIT = 56 * 1024 * 1024
FAR_BUCKET = 15
BUCKET_STEPS = (13, 20, 30, 46, 70, 108, 166)
KEY_BYTES = 4
N_BIAS_TILES = 4
N_STAGE = 2


def _rms(x):
    return x * lax.rsqrt(jnp.mean(x * x, axis=-1, keepdims=True) + EPS)


def _dot(a, b):
    return jnp.dot(a, b, preferred_element_type=F32)


def _dot_t(a, b):
    return lax.dot_general(a, b, (((1,), (1,)), ((), ())), preferred_element_type=F32)


def _mod_kernel(c_ref, w_ref, b_ref, o_ref):
    c = c_ref[...]
    s = c * jax.nn.sigmoid(c)
    o_ref[...] = jnp.dot(s, w_ref[...], preferred_element_type=F32,
                         precision=lax.Precision.HIGHEST) + b_ref[...]


def _modulation(c_all, w_ada, b_ada):
    n = c_all.shape[0]
    tn = 1024
    return pl.pallas_call(
        _mod_kernel,
        grid=(6 * D_MODEL // tn,),
        in_specs=[pl.BlockSpec((n, D_MODEL), lambda j: (0, 0)),
                  pl.BlockSpec((D_MODEL, tn), lambda j: (0, j)),
                  pl.BlockSpec((1, tn), lambda j: (0, j))],
        out_specs=pl.BlockSpec((n, tn), lambda j: (0, j)),
        out_shape=jax.ShapeDtypeStruct((n, 6 * D_MODEL), F32),
        name="adaln_mod",
        compiler_params=pltpu.CompilerParams(dimension_semantics=("arbitrary",),
                                             vmem_limit_bytes=VMEM_LIMIT),
    )(c_all, w_ada, b_ada.reshape(1, -1))


_C_DQ = 0
_C_DK = _C_DQ + DIFF_W
_C_DV = _C_DK + DIFF_W
_C_SQ = _C_DV + DIFF_W
_C_SK = _C_SQ + DSA_W
_C_SV = _C_SK + DSA_KV_W
_C_END = _C_SV + DSA_KV_W
IDX_W = 384


def _hi_lo(a):
    hi = a.astype(BF16)
    return hi, (a - hi.astype(F32)).astype(BF16)


def _split3(a):
    a1 = a.astype(BF16)
    r = a - a1.astype(F32)
    a2 = r.astype(BF16)
    return a1, a2, (r - a2.astype(F32)).astype(BF16)


def _value_rows(v):
    row = lax.broadcasted_iota(I32, (V_PAD, KEY_BLOCK), 0)
    extra = jnp.where(row == 0, 1.0, 0.0)
    return jnp.concatenate([v.T, extra], axis=0).astype(BF16)


def _proj_kernel(x_ref, sc_ref, sh_ref, g_ref, wm_ref, wih_ref, wil_ref, wi3_ref,
                 dk32, dv32, sk32, sv32, ik32, iw32, dq16, dk16, sq16, sk16, iq16, ik16, dvx, svx,
                 *, transpose_values):
    bt, tt, _ = x_ref.shape
    x = x_ref[...]
    h = (_rms(x) * g_ref[...]) * (1.0 + sc_ref[...]) + sh_ref[...]
    h = h.reshape(bt * tt, D_MODEL)
    hb, hl, h3 = _split3(h)
    z = _dot(hb, wm_ref[...])
    wih = wih_ref[...]
    wil = wil_ref[...]
    zi = (_dot(hb, wih) + _dot(hb, wil) + _dot(hl, wih)
          + _dot(hb, wi3_ref[...]) + _dot(h3, wih) + _dot(hl, wil))

    def put(ref, v):
        ref[...] = v.reshape(bt, tt, v.shape[-1]).astype(ref.dtype)

    zdk = z[:, _C_DK:_C_DV]
    zdv = z[:, _C_DV:_C_SQ]
    zsk = z[:, _C_SK:_C_SV]
    zsv = z[:, _C_SV:_C_END]
    zik = zi[:, 256:256 + IDX_DH]
    for n in range(N_DIFF):
        dk32[:, :, n, :] = zdk[:, n * LANE:(n + 1) * LANE].reshape(bt, tt, LANE)
        dv32[:, :, n, :] = zdv[:, n * LANE:(n + 1) * LANE].reshape(bt, tt, LANE)
    put(sk32, zsk)
    put(sv32, zsv)
    put(ik32, zik)
    put(iw32, zi[:, 256:IDX_W])
    low = lax.broadcasted_iota(I32, (bt * tt, LANE), 1) < DIFF_DH
    halves = lambda v: (jnp.where(low, v, 0.0), jnp.where(low, 0.0, v))
    dq_pairs = [halves(z[:, _C_DQ + n * LANE:_C_DQ + (n + 1) * LANE]) for n in range(N_DIFF)]
    sq_pairs = [halves(z[:, _C_SQ + j * LANE:_C_SQ + (j + 1) * LANE]) for j in range(DSA_GROUP)]
    put(dq16, jnp.concatenate([part for pair in dq_pairs for part in pair], axis=1))
    put(dk16, zdk)
    put(sq16, jnp.concatenate([pair[0] for pair in sq_pairs] + [pair[1] for pair in sq_pairs], axis=1))
    put(sk16, zsk)

    q1, q2, q3 = _split3(zi[:, 0:N_IDX * IDX_DH])
    k1, k2, k3 = _split3(zik)
    head = lambda a, n: a[:, n * IDX_DH:(n + 1) * IDX_DH]
    put(iq16, jnp.concatenate([head(p, n) for n in range(N_IDX)
                               for p in (q1, q1, q2, q1, q3, q2)], axis=1))
    put(ik16, jnp.concatenate([k1, k2, k1, k3, k1, k2], axis=1))

    if transpose_values:
        for kb in range(tt // KEY_BLOCK):
            rows = slice(kb * KEY_BLOCK, (kb + 1) * KEY_BLOCK)
            for n in range(N_DIFF):
                dvx[n, kb] = _value_rows(zdv[rows, n * LANE:(n + 1) * LANE])
            for g in range(N_DSA_KV):
                svx[g, kb] = _value_rows(zsv[rows, g * DSA_DH:(g + 1) * DSA_DH])
    else:
        put(dvx, zdv)
        put(svx, zsv)


def _projection(x, sc, sh, g, wm, wih, wil, wi3, bt, tt, transpose_values):
    B, T, _ = x.shape
    grid = (B // bt, T // tt)
    row = lambda w: pl.BlockSpec((bt, tt, w), lambda b, t: (b, t, 0))
    per_b = pl.BlockSpec((bt, 1, D_MODEL), lambda b, t: (b, 0, 0))
    full = lambda a: pl.BlockSpec(a.shape, lambda b, t: (0,) * a.ndim,
                                  pipeline_mode=pl.Buffered(1))
    outs = [(DIFF_W, F32), (DIFF_W, F32), (DSA_KV_W, F32), (DSA_KV_W, F32), (IDX_DH, F32),
            (LANE, F32), (2 * DIFF_W, BF16), (DIFF_W, BF16), (N_DSA * LANE, BF16),
            (DSA_KV_W, BF16), (N_IDX * 384, BF16), (384, BF16)]
    out_specs = [row(w) for w, _ in outs]
    out_shape = [jax.ShapeDtypeStruct((B, T, w), dt) for w, dt in outs]
    for j in range(2):
        out_specs[j] = pl.BlockSpec((bt, tt, N_DIFF, 2 * DIFF_DH), lambda b, t: (b, t, 0, 0))
        out_shape[j] = jax.ShapeDtypeStruct((B, T, N_DIFF, 2 * DIFF_DH), F32)
    if transpose_values:
        assert bt == 1 and tt % KEY_BLOCK == 0
        nb = tt // KEY_BLOCK
        out_specs += [pl.BlockSpec((None, N_DIFF, nb, V_ROWS, KEY_BLOCK), lambda b, t: (b, 0, t, 0, 0)),
                      pl.BlockSpec((None, N_DSA_KV, nb, DSA_DH + V_PAD, KEY_BLOCK),
                                   lambda b, t: (b, 0, t, 0, 0))]
        out_shape += [jax.ShapeDtypeStruct((B, N_DIFF, T // KEY_BLOCK, V_ROWS, KEY_BLOCK), BF16),
                      jax.ShapeDtypeStruct((B, N_DSA_KV, T // KEY_BLOCK, DSA_DH + V_PAD, KEY_BLOCK), BF16)]
    else:
        out_specs += [row(DIFF_W), row(DSA_KV_W)]
        out_shape += [jax.ShapeDtypeStruct((B, T, DIFF_W), BF16),
                      jax.ShapeDtypeStruct((B, T, DSA_KV_W), BF16)]
    return pl.pallas_call(
        functools.partial(_proj_kernel, transpose_values=transpose_values),
        grid=grid,
        in_specs=[row(D_MODEL), per_b, per_b, full(g), full(wm), full(wih), full(wil), full(wi3)],
        out_specs=out_specs,
        out_shape=out_shape,
        name="in_proj",
        compiler_params=pltpu.CompilerParams(dimension_semantics=("arbitrary", "arbitrary"),
                                             vmem_limit_bytes=VMEM_LIMIT),
    )(x, sc, sh, g, wm, wih, wil, wi3)


def _fold_rows(x, rows, op):
    acc = x[0:rows]
    for r in range(1, x.shape[0] // rows):
        acc = op(acc, x[r * rows:(r + 1) * rows])
    return acc


def _init_bias_tiles(tab_ref, tb_ref):
    k = lax.broadcasted_iota(I32, (LANE, LANE), 0)
    q = lax.broadcasted_iota(I32, (LANE, LANE), 1)
    buckets = []
    for t in range(N_BIAS_TILES - 1):
        rel = k - q - LANE * t
        n = jnp.abs(rel)
        large = jnp.full_like(n, 8)
        for step in BUCKET_STEPS:
            large = large + jnp.where(n >= step, 1, 0)
        buckets.append(jnp.where(rel > 0, NUM_BUCKETS // 2, 0) + jnp.where(n < 8, n, large))

    def per_head(h, carry):
        for t, bucket in enumerate(buckets):
            val = jnp.zeros((LANE, LANE), F32)
            for b in range(NUM_BUCKETS):
                val = jnp.where(bucket == b, tab_ref[b, h] * LOG2E, val)
            tb_ref[h, t] = val
        tb_ref[h, N_BIAS_TILES - 1] = jnp.full((LANE, LANE), tab_ref[FAR_BUCKET, h] * LOG2E, F32)
        return carry

    lax.fori_loop(0, N_HEADS, per_head, 0)


def _attn_kernel(tab_ref, lam_ref, gsub_ref, dq_ref, sq_ref, iq_ref, iw_ref,
                 dk_ref, dv_ref, sk_ref, sv_ref, ik_ref, o_ref,
                 tb_ref, ltri_ref, key_ref, byte_ref, s_ref, acc_ref, ot_ref, *, tq, q_offset):
    tk = KEY_BLOCK
    b = pl.program_id(0)
    i = pl.program_id(1)

    @pl.when((b == 0) & (i == 0))
    def _():
        _init_bias_tiles(tab_ref, tb_ref)
        ltri_ref[...] = jnp.where(lax.broadcasted_iota(I32, (tk, tk), 1)
                                  < lax.broadcasted_iota(I32, (tk, tk), 0), 1.0, 0.0).astype(BF16)

    qpos0 = q_offset + i * tq
    lim_last = CHUNK * ((qpos0 + tq - 1) // CHUNK + 1)
    nblk = (lim_last + tk - 1) // tk
    qq = lax.broadcasted_iota(I32, (1, tq), 1)
    lim_q = CHUNK * ((qpos0 + qq) // CHUNK + 1)
    kk = lax.broadcasted_iota(I32, (tk, tq), 0)

    def key_rows(jb):
        return pl.ds(pl.multiple_of(jb * tk, tk), tk)

    def over_blocks(body, init, unroll=2):
        def run(first, n, carry):
            for u in range(n):
                carry = body(first + u, carry)
            return carry
        carry = lax.fori_loop(0, nblk // unroll, lambda jp, c: run(unroll * jp, unroll, c), init)
        n = unroll // 2
        while n >= 1:
            first = (nblk // (2 * n)) * (2 * n)
            carry = lax.fori_loop(0, (nblk // n) % 2, lambda _, c, first=first, n=n: run(first, n, c),
                                  carry)
            n //= 2
        return carry

    def bias_tile(head, jb):
        rows = []
        for a in range(tk // LANE):
            parts = []
            for c in range(tq // LANE):
                t = jnp.clip((qpos0 + c * LANE - (jb * tk + a * LANE)) // LANE, 0, N_BIAS_TILES - 1)
                parts.append(tb_ref[head, t])
            rows.append(jnp.concatenate(parts, axis=1))
        return jnp.concatenate(rows, axis=0)

    iw = iw_ref[...]

    def score_block(jb, carry):
        kc = ik_ref[key_rows(jb), :]
        sc = jnp.zeros((tk, tq), F32)
        for h in range(N_IDX):
            d = _dot_t(kc, iq_ref[:, h * 384:(h + 1) * 384])
            sc = sc + jnp.maximum(d, 0.0) * iw[h:h + 1]
        sc = jnp.where(kk + jb * tk < lim_q, sc, NEG)
        bits = pltpu.bitcast(sc, I32)
        key = bits ^ ((bits >> 31) & 0x7FFFFFFF)
        key_ref[jb] = key
        for lvl in range(KEY_BYTES):
            byte = ((key >> (8 * lvl)) & 0xFF) if lvl < KEY_BYTES - 1 else (key >> 24) + 128
            byte_ref[lvl, jb] = byte.astype(F32).astype(BF16)
        return carry

    over_blocks(score_block, 0)

    one16 = jnp.ones((), BF16)
    zero16 = jnp.zeros((), BF16)

    def count_digits(lvl, cand):
        def body(jb, acc):
            hit = jnp.where(byte_ref[lvl, jb] >= cand, one16, zero16)
            return acc + _fold_rows(hit, 2 * SUBLANE, jnp.add).astype(F32)
        acc = over_blocks(body, jnp.zeros((2 * SUBLANE, tq), F32))
        return jnp.sum(acc, axis=0, keepdims=True)

    want = jnp.full((1, tq), float(TOPK), F32)
    thr = jnp.zeros((1, tq), I32)
    for lvl in reversed(range(KEY_BYTES)):
        def digit_bit(it, state, lvl=lvl, want=want):
            digit, above = state
            cand = digit + jnp.left_shift(jnp.int32(1), 7 - it).astype(F32)
            cnt = count_digits(lvl, cand.astype(BF16))
            ok = cnt >= want
            return jnp.where(ok, cand, digit), jnp.where(ok, above, cnt)

        digit, above = lax.fori_loop(0, 8, digit_bit,
                                     (jnp.zeros((1, tq), F32), jnp.zeros((1, tq), F32)))
        want = want - above
        digit_i = digit.astype(I32)
        thr = thr | ((digit_i - 128) << 24 if lvl == KEY_BYTES - 1 else digit_i << (8 * lvl))
        if lvl > 0:
            def narrow(jb, carry, lvl=lvl, digit16=digit.astype(BF16)):
                byte_ref[lvl - 1, jb] = jnp.where(byte_ref[lvl, jb] == digit16,
                                                  byte_ref[lvl - 1, jb], -one16)
                return carry
            over_blocks(narrow, 0)

    def mask_block(jb, before):
        kb = key_ref[jb]
        eq = jnp.where(kb == thr, 1.0, 0.0)
        rank = _dot(ltri_ref[...], eq.astype(BF16)) + before
        keep = jnp.where(rank < want, eq, 0.0)
        sel = jnp.where(kb > thr, 0.0, (keep - 1.0) * (-NEG))
        sel = jnp.where(kk + jb * tk < lim_q, sel, NEG)
        key_ref[jb] = pltpu.bitcast(sel, I32)
        return before + jnp.sum(_fold_rows(eq, SUBLANE, jnp.add), axis=0, keepdims=True)

    over_blocks(mask_block, jnp.zeros((1, tq), F32))

    lam = lam_ref[...]
    lam_full = (jnp.exp(jnp.sum(lam[0:1] * lam[1:2], axis=1, keepdims=True))
                - jnp.exp(jnp.sum(lam[2:3] * lam[3:4], axis=1, keepdims=True)) + LAM_INIT)

    def dsa_group(pair):
        heads = [N_STAGE * pair + n for n in range(N_STAGE)]

        def scores(jb):
            kb = sk_ref[key_rows(jb), :]
            sel = pltpu.bitcast(key_ref[jb], F32)
            return [_dot_t(kb, sq_ref[:, h * LANE:(h + 1) * LANE]) + bias_tile(N_DIFF + h, jb) + sel
                    for h in heads]

        def finish():
            for n, h in enumerate(heads):
                a = acc_ref[n]
                out = a[0:DSA_DH] / a[DSA_DH:DSA_DH + 1]
                ot_ref[DIFF_W + h * DSA_DH:DIFF_W + (h + 1) * DSA_DH, :] = out

        return scores, (lambda n, jb: sv_ref[heads[n] // DSA_GROUP, jb]), finish

    def diff_group(h):
        def scores(jb):
            kb = dk_ref[key_rows(jb), h * LANE:(h + 1) * LANE]
            bias = bias_tile(h, jb) + jnp.where(kk + jb * tk < lim_q, 0.0, NEG)
            return [_dot_t(kb, dq_ref[:, (2 * h + part) * LANE:(2 * h + part + 1) * LANE]) + bias
                    for part in range(2)]

        def finish():
            a1 = acc_ref[0]
            a2 = acc_ref[1]
            od = (a1[0:LANE] / a1[LANE:LANE + 1]
                  - lam_full * (a2[0:LANE] / a2[LANE:LANE + 1]))
            od = od * lax.rsqrt(jnp.mean(od * od, axis=0, keepdims=True) + EPS)
            ot_ref[h * LANE:(h + 1) * LANE, :] = (od * gsub_ref[...]) * (1.0 - LAM_INIT)

        return scores, (lambda n, jb: dv_ref[h, jb]), finish

    groups = ([dsa_group(p) for p in range(N_DSA // N_STAGE)]
              + [diff_group(h) for h in range(N_DIFF)])
    m_prev = None
    for k in range(len(groups) + 1):
        produce = groups[k] if k < len(groups) else None
        consume = groups[k - 1] if k > 0 else None

        def step(jb, mx, k=k, produce=produce, consume=consume, m_prev=m_prev):
            if consume is not None:
                for n in range(N_STAGE):
                    p = jnp.exp2(s_ref[(k - 1) % 2, n, jb] - m_prev[n]).astype(BF16)
                    v = consume[1](n, jb)
                    acc_ref[n, 0:v.shape[0]] += _dot(v, p)
            if produce is None:
                return mx
            out = []
            for n, s in enumerate(produce[0](jb)):
                s_ref[k % 2, n, jb] = s
                out.append(jnp.maximum(mx[n], _fold_rows(s, SUBLANE, jnp.maximum)))
            return tuple(out)

        if consume is not None:
            for n in range(N_STAGE):
                acc_ref[n] = jnp.zeros((V_ROWS, tq), F32)
        mx = over_blocks(step, tuple(jnp.full((SUBLANE, tq), -3e38, F32) for _ in range(N_STAGE)),
                         unroll=4)
        if consume is not None:
            consume[2]()
        m_prev = [jnp.max(x, axis=0, keepdims=True) for x in mx]

    o_ref[...] = ot_ref[...].T.astype(o_ref.dtype)


def _attention(tab, lam4, gsub_t, dq, sq, iq, iw_t, dk, dv_t, sk, sv_t, ik, tq, q_offset):
    B, T, _ = dq.shape
    Lp = dk.shape[1]
    nblk_max = Lp // KEY_BLOCK
    qspec = lambda w: pl.BlockSpec((None, tq, w), lambda b, i: (b, i, 0))
    once = pl.Buffered(1)
    kspec = lambda w: pl.BlockSpec((None, Lp, w), lambda b, i: (b, 0, 0), pipeline_mode=once)
    small = lambda a: pl.BlockSpec(a.shape, lambda b, i: (0,) * a.ndim)
    kern = functools.partial(_attn_kernel, tq=tq, q_offset=q_offset)
    return pl.pallas_call(
        kern,
        grid=(B, T // tq),
        in_specs=[pl.BlockSpec(memory_space=pltpu.SMEM), small(lam4), small(gsub_t),
                  qspec(2 * DIFF_W), qspec(N_DSA * LANE), qspec(N_IDX * 384),
                  pl.BlockSpec((None, SUBLANE, tq), lambda b, i: (b, 0, i)),
                  kspec(DIFF_W),
                  pl.BlockSpec((None, N_DIFF, nblk_max, V_ROWS, KEY_BLOCK),
                               lambda b, i: (b, 0, 0, 0, 0), pipeline_mode=once),
                  kspec(DSA_KV_W),
                  pl.BlockSpec((None, N_DSA_KV, nblk_max, DSA_DH + V_PAD, KEY_BLOCK),
                               lambda b, i: (b, 0, 0, 0, 0), pipeline_mode=once),
                  kspec(384)],
        out_specs=pl.BlockSpec((None, tq, D_MODEL), lambda b, i: (b, i, 0)),
        out_shape=jax.ShapeDtypeStruct((B, T, D_MODEL), BF16),
        name="attn",
        scratch_shapes=[pltpu.VMEM((N_HEADS, N_BIAS_TILES, LANE, LANE), F32),
                        pltpu.VMEM((KEY_BLOCK, KEY_BLOCK), BF16),
                        pltpu.VMEM((nblk_max, KEY_BLOCK, tq), I32),
                        pltpu.VMEM((KEY_BYTES, nblk_max, KEY_BLOCK, tq), BF16),
                        pltpu.VMEM((2, N_STAGE, nblk_max, KEY_BLOCK, tq), F32),
                        pltpu.VMEM((N_STAGE, V_ROWS, tq), F32),
                        pltpu.VMEM((D_MODEL, tq), F32)],
        compiler_params=pltpu.CompilerParams(dimension_semantics=("arbitrary", "arbitrary"),
                                             vmem_limit_bytes=VMEM_LIMIT),
    )(tab, lam4, gsub_t, dq, sq, iq, iw_t, dk, dv_t, sk, sv_t, ik)


FF_CUTS = (0, 1536, D_FF)


def _ffn_kernel(x_ref, o_ref, ga1_ref, sc2_ref, sh2_ref, ga2_ref, gpm_ref, gpf_ref, gpo_ref,
                wo_ref, wgu_ref, wd_ref, y_ref):
    bt, tt, _ = x_ref.shape
    rows = bt * tt
    m = _dot(o_ref[...].reshape(rows, D_MODEL), wo_ref[...]).reshape(bt, tt, D_MODEL)
    x1 = x_ref[...] + ga1_ref[...] * (_rms(m) * gpm_ref[...])
    h2 = (_rms(x1) * gpf_ref[...]) * (1.0 + sc2_ref[...]) + sh2_ref[...]
    h2 = h2.reshape(rows, D_MODEL).astype(BF16)
    f = jnp.zeros((rows, D_MODEL), F32)
    for lo, hi in zip(FF_CUTS[:-1], FF_CUTS[1:]):
        g = _dot(h2, wgu_ref[:, lo:hi])
        u = _dot(h2, wgu_ref[:, D_FF + lo:D_FF + hi])
        a = (g * jax.nn.sigmoid(g) * u).astype(BF16)
        f = f + _dot(a, wd_ref[lo:hi, :])
    f = f.reshape(bt, tt, D_MODEL)
    y_ref[...] = x1 + ga2_ref[...] * (_rms(f) * gpo_ref[...])


def _out_ffn(x, o, ga1, sc2, sh2, ga2, gpm, gpf, gpo, wo, wgu, wd, bt, tt):
    B, T, _ = x.shape
    row = pl.BlockSpec((bt, tt, D_MODEL), lambda b, t: (b, t, 0))
    per_b = pl.BlockSpec((bt, 1, D_MODEL), lambda b, t: (b, 0, 0))
    full = lambda a: pl.BlockSpec(a.shape, lambda b, t: (0,) * a.ndim,
                                  pipeline_mode=pl.Buffered(1))
    return pl.pallas_call(
        _ffn_kernel,
        grid=(B // bt, T // tt),
        in_specs=[row, row, per_b, per_b, per_b, per_b, full(gpm), full(gpf), full(gpo),
                  full(wo), full(wgu), full(wd)],
        out_specs=row,
        out_shape=jax.ShapeDtypeStruct((B, T, D_MODEL), F32),
        name="out_ffn",
        compiler_params=pltpu.CompilerParams(dimension_semantics=("arbitrary", "arbitrary"),
                                             vmem_limit_bytes=VMEM_LIMIT),
    )(x, o, ga1, sc2, sh2, ga2, gpm, gpf, gpo, wo, wgu, wd)


def _split_hi_lo(a):
    hi = a.astype(BF16)
    lo = (a - hi.astype(F32)).astype(BF16)
    return hi, lo


def _prep_in_weights(w_in):
    cuts = [0, 512, 1024, 1536, 2048, 2176, 2304, 2560, 2624, 2628]
    wdq, wdk, wdv, wsq, wsk, wsv, wiq, wik, wiw = [w_in[:, a:b] for a, b in zip(cuts[:-1], cuts[1:])]
    scale = DIFF_DH ** -0.5 * LOG2E
    sq_head = lambda h: wsq[:, h * DSA_DH:(h + 1) * DSA_DH]
    sq_cols = [sq_head(j + g * DSA_GROUP) for j in range(DSA_GROUP) for g in range(N_DSA_KV)]
    wm = jnp.concatenate([wdq * scale, wdk, wdv] + [c * scale for c in sq_cols] + [wsk, wsv],
                         axis=1).astype(BF16)
    wi = jnp.concatenate([wiq, wik, wiw, jnp.zeros((D_MODEL, IDX_W - 324), F32)], axis=1)
    wih, wil = _split_hi_lo(wi)
    wi3 = (wi - wih.astype(F32) - wil.astype(F32)).astype(BF16)
    return wm, wih, wil, wi3


def _idx_key_operand(ik):
    kh, kl = _split_hi_lo(ik)
    k3 = (ik - kh.astype(F32) - kl.astype(F32)).astype(BF16)
    return jnp.concatenate([kh, kl, kh, k3, kh, kl], axis=-1)


def _pad_axis1(a, n):
    return jnp.pad(a, ((0, 0), (0, n - a.shape[1])) + ((0, 0),) * (a.ndim - 2))


def _value_operand(v):
    B, L, H, W = v.shape
    vt = jnp.transpose(v, (0, 2, 3, 1))
    extra = jnp.concatenate([jnp.ones((B, H, 1, L), BF16),
                             jnp.zeros((B, H, V_PAD - 1, L), BF16)], axis=2)
    vt = jnp.concatenate([vt, extra], axis=2).reshape(B, H, W + V_PAD, L // KEY_BLOCK, KEY_BLOCK)
    return jnp.transpose(vt, (0, 1, 3, 2, 4))


def _group(x, mods, past, params, bt, tt, tq):
    (g_pre_mix, g_post_mix, g_pre_ffn, g_post_ffn, wm, wih, wil, wi3, lam4, g_subln,
     wo, wgu, wd, tab) = params
    B, T, _ = x.shape
    sh1, sc1, ga1, sh2, sc2, ga2 = mods
    (dk32, dv32, sk32, sv32, ik32, iw32, dq16, dk16, sq16, sk16, iq16, ik16, dvx, svx) = _projection(
        x, sc1, sh1, g_pre_mix, wm, wih, wil, wi3, bt, tt, transpose_values=past is None)
    iw = iw32[..., IDX_DH:IDX_DH + N_IDX] * (1.0 / 16.0)
    if past is None:
        q_offset = 0
        dk, dv_t, sk, sv_t, ikc = dk16, dvx, sk16, svx, ik16
    else:
        pk, pv, psk, psv, pik = past
        P = pk.shape[1]
        q_offset = P
        lp = -(-(P + T) // KEY_BLOCK) * KEY_BLOCK
        cat = lambda p, n: _pad_axis1(
            jnp.concatenate([p.reshape(B, P, -1).astype(BF16), n], axis=1), lp)
        dk, dv, sk, sv = cat(pk, dk16), cat(pv, dvx), cat(psk, sk16), cat(psv, svx)
        ikc = _pad_axis1(jnp.concatenate([_idx_key_operand(pik), ik16], axis=1), lp)
        dv_t = _value_operand(dv.reshape(B, lp, N_DIFF, 2 * DIFF_DH))
        sv_t = _value_operand(sv.reshape(B, lp, N_DSA_KV, DSA_DH))
    tpad = -(-T // tq) * tq
    qpad = lambda a: _pad_axis1(a, tpad)
    iw_t = _pad_axis1(jnp.transpose(qpad(iw), (0, 2, 1)), SUBLANE)
    gsub_t = jnp.broadcast_to(g_subln.reshape(-1, 1), (2 * DIFF_DH, tq))
    o = _attention(tab, lam4, gsub_t, qpad(dq16), qpad(sq16), qpad(iq16), iw_t,
                   dk, dv_t, sk, sv_t, ikc, tq, q_offset)[:, :T]
    y = _out_ffn(x, o, ga1, sc2, sh2, ga2, g_post_mix, g_pre_ffn, g_post_ffn, wo, wgu, wd, bt, tt)
    rows = (dk32[None], dv32[None],
            sk32.reshape(1, B, T, N_DSA_KV, DSA_DH), sv32.reshape(1, B, T, N_DSA_KV, DSA_DH),
            ik32.reshape(1, B, T, IDX_DH))
    return y, rows


def kernel(x_prompt, x_sample, cache_diff_k, cache_diff_v, cache_dsa_k, cache_dsa_v, cache_dsa_kidx,
           c_prompt, c_sample, w_ada, b_ada, g_pre_mix, g_post_mix, g_pre_ffn, g_post_ffn, w_in,
           lambda_q1, lambda_k1, lambda_q2, lambda_k2, g_subln, w_out, w_gate_up, w_down,
           rel_bias_table):
    Bp = x_prompt.shape[0]
    c_all = jnp.concatenate([c_prompt, c_sample], axis=0)
    mod = _modulation(c_all, w_ada[0], b_ada[0])
    mods = [mod[:, j * D_MODEL:(j + 1) * D_MODEL][:, None, :] for j in range(6)]
    mods_p = [m[:Bp] for m in mods]
    mods_s = [m[Bp:] for m in mods]

    wm, wih, wil, wi3 = _prep_in_weights(w_in[0])
    lam4 = jnp.concatenate([lambda_q1, lambda_k1, lambda_q2, lambda_k2], axis=0)
    row = lambda g: g[0].reshape(1, -1)
    params = (row(g_pre_mix), row(g_post_mix), row(g_pre_ffn), row(g_post_ffn), wm, wih, wil, wi3,
              lam4, g_subln[0], w_out[0].astype(BF16), w_gate_up[0].astype(BF16),
              w_down[0].astype(BF16), rel_bias_table)

    yp, rp = _group(x_prompt, mods_p, None, params, bt=1, tt=512, tq=256)
    past = (cache_diff_k[0], cache_diff_v[0], cache_dsa_k[0], cache_dsa_v[0], cache_dsa_kidx[0])
    ys, rs = _group(x_sample, mods_s, past, params, bt=8, tt=64, tq=128)
    return (yp, ys) + rp + rs
```

```python
import functools
import math

import jax
import jax.numpy as jnp
from jax import lax
from jax.experimental import pallas as pl
from jax.experimental.pallas import tpu as pltpu

F32 = jnp.float32
BF16 = jnp.bfloat16
I32 = jnp.int32

D_MODEL = 1024
CHUNK = 64
N_DIFF = 4
DIFF_DH = 64
DIFF_W = N_DIFF * 2 * DIFF_DH
N_DSA = 8
N_DSA_KV = 2
DSA_GROUP = N_DSA // N_DSA_KV
DSA_DH = 64
DSA_W = N_DSA * DSA_DH
DSA_KV_W = N_DSA_KV * DSA_DH
N_IDX = 4
IDX_DH = 64
TOPK = 256
N_HEADS = N_DIFF + N_DSA
NUM_BUCKETS = 32
D_FF = 2816
EPS = 1e-6
NEG = -1e30
LOG2E = math.log2(math.e)
LAM_INIT = 0.8 - 0.6 * math.exp(-0.3 * 0)

LANE = 128
SUBLANE = 8
KEY_BLOCK = 256
V_PAD = 16
V_ROWS = 2 * DIFF_DH + V_PAD
VMEM_LIMIT = 56 * 1024 * 1024
FAR_BUCKET = 15
BUCKET_STEPS = (13, 20, 30, 46, 70, 108, 166)
KEY_BYTES = 4
N_BIAS_TILES = 4
N_STAGE = 2


def _rms(x):
    return x * lax.rsqrt(jnp.mean(x * x, axis=-1, keepdims=True) + EPS)


def _dot(a, b):
    return jnp.dot(a, b, preferred_element_type=F32)


def _dot_t(a, b):
    return lax.dot_general(a, b, (((1,), (1,)), ((), ())), preferred_element_type=F32)


def _mod_kernel(c_ref, w_ref, b_ref, o_ref):
    c = c_ref[...]
    s = c * jax.nn.sigmoid(c)
    o_ref[...] = jnp.dot(s, w_ref[...], preferred_element_type=F32,
                         precision=lax.Precision.HIGHEST) + b_ref[...]


def _modulation(c_all, w_ada, b_ada):
    n = c_all.shape[0]
    tn = 1024
    return pl.pallas_call(
        _mod_kernel,
        grid=(6 * D_MODEL // tn,),
        in_specs=[pl.BlockSpec((n, D_MODEL), lambda j: (0, 0)),
                  pl.BlockSpec((D_MODEL, tn), lambda j: (0, j)),
                  pl.BlockSpec((1, tn), lambda j: (0, j))],
        out_specs=pl.BlockSpec((n, tn), lambda j: (0, j)),
        out_shape=jax.ShapeDtypeStruct((n, 6 * D_MODEL), F32),
        name="adaln_mod",
        compiler_params=pltpu.CompilerParams(dimension_semantics=("arbitrary",),
                                             vmem_limit_bytes=VMEM_LIMIT),
    )(c_all, w_ada, b_ada.reshape(1, -1))


_C_DQ = 0
_C_DK = _C_DQ + DIFF_W
_C_DV = _C_DK + DIFF_W
_C_SQ = _C_DV + DIFF_W
_C_SK = _C_SQ + DSA_W
_C_SV = _C_SK + DSA_KV_W
_C_END = _C_SV + DSA_KV_W
IDX_W = 384


def _hi_lo(a):
    hi = a.astype(BF16)
    return hi, (a - hi.astype(F32)).astype(BF16)


def _split3(a):
    a1 = a.astype(BF16)
    r = a - a1.astype(F32)
    a2 = r.astype(BF16)
    return a1, a2, (r - a2.astype(F32)).astype(BF16)


def _value_rows(v):
    row = lax.broadcasted_iota(I32, (V_PAD, KEY_BLOCK), 0)
    extra = jnp.where(row == 0, 1.0, 0.0)
    return jnp.concatenate([v.T, extra], axis=0).astype(BF16)


def _proj_kernel(x_ref, sc_ref, sh_ref, g_ref, wm_ref, wih_ref, wil_ref, wi3_ref,
                 dk32, dv32, sk32, sv32, ik32, iw32, dq16, dk16, sq16, sk16, iq16, ik16, dvx, svx,
                 *, transpose_values):
    bt, tt, _ = x_ref.shape
    x = x_ref[...]
    h = (_rms(x) * g_ref[...]) * (1.0 + sc_ref[...]) + sh_ref[...]
    h = h.reshape(bt * tt, D_MODEL)
    hb, hl, h3 = _split3(h)
    z = _dot(hb, wm_ref[...])
    wih = wih_ref[...]
    wil = wil_ref[...]
    zi = (_dot(hb, wih) + _dot(hb, wil) + _dot(hl, wih)
          + _dot(hb, wi3_ref[...]) + _dot(h3, wih) + _dot(hl, wil))

    def put(ref, v):
        ref[...] = v.reshape(bt, tt, v.shape[-1]).astype(ref.dtype)

    zdk = z[:, _C_DK:_C_DV]
    zdv = z[:, _C_DV:_C_SQ]
    zsk = z[:, _C_SK:_C_SV]
    zsv = z[:, _C_SV:_C_END]
    zik = zi[:, 256:256 + IDX_DH]
    for n in range(N_DIFF):
        dk32[:, :, n, :] = zdk[:, n * LANE:(n + 1) * LANE].reshape(bt, tt, LANE)
        dv32[:, :, n, :] = zdv[:, n * LANE:(n + 1) * LANE].reshape(bt, tt, LANE)
    put(sk32, zsk)
    put(sv32, zsv)
    put(ik32, zik)
    put(iw32, zi[:, 256:IDX_W])
    low = lax.broadcasted_iota(I32, (bt * tt, LANE), 1) < DIFF_DH
    halves = lambda v: (jnp.where(low, v, 0.0), jnp.where(low, 0.0, v))
    dq_pairs = [halves(z[:, _C_DQ + n * LANE:_C_DQ + (n + 1) * LANE]) for n in range(N_DIFF)]
    sq_pairs = [halves(z[:, _C_SQ + j * LANE:_C_SQ + (j + 1) * LANE]) for j in range(DSA_GROUP)]
    put(dq16, jnp.concatenate([part for pair in dq_pairs for part in pair], axis=1))
    put(dk16, zdk)
    put(sq16, jnp.concatenate([pair[0] for pair in sq_pairs] + [pair[1] for pair in sq_pairs], axis=1))
    put(sk16, zsk)

    q1, q2, q3 = _split3(zi[:, 0:N_IDX * IDX_DH])
    k1, k2, k3 = _split3(zik)
    head = lambda a, n: a[:, n * IDX_DH:(n + 1) * IDX_DH]
    put(iq16, jnp.concatenate([head(p, n) for n in range(N_IDX)
                               for p in (q1, q1, q2, q1, q3, q2)], axis=1))
    put(ik16, jnp.concatenate([k1, k2, k1, k3, k1, k2], axis=1))

    if transpose_values:
        for kb in range(tt // KEY_BLOCK):
            rows = slice(kb * KEY_BLOCK, (kb + 1) * KEY_BLOCK)
            for n in range(N_DIFF):
                dvx[n, kb] = _value_rows(zdv[rows, n * LANE:(n + 1) * LANE])
            for g in range(N_DSA_KV):
                svx[g, kb] = _value_rows(zsv[rows, g * DSA_DH:(g + 1) * DSA_DH])
    else:
        put(dvx, zdv)
        put(svx, zsv)


def _projection(x, sc, sh, g, wm, wih, wil, wi3, bt, tt, transpose_values):
    B, T, _ = x.shape
    grid = (B // bt, T // tt)
    row = lambda w: pl.BlockSpec((bt, tt, w), lambda b, t: (b, t, 0))
    per_b = pl.BlockSpec((bt, 1, D_MODEL), lambda b, t: (b, 0, 0))
    full = lambda a: pl.BlockSpec(a.shape, lambda b, t: (0,) * a.ndim,
                                  pipeline_mode=pl.Buffered(1))
    outs = [(DIFF_W, F32), (DIFF_W, F32), (DSA_KV_W, F32), (DSA_KV_W, F32), (IDX_DH, F32),
            (LANE, F32), (2 * DIFF_W, BF16), (DIFF_W, BF16), (N_DSA * LANE, BF16),
            (DSA_KV_W, BF16), (N_IDX * 384, BF16), (384, BF16)]
    out_specs = [row(w) for w, _ in outs]
    out_shape = [jax.ShapeDtypeStruct((B, T, w), dt) for w, dt in outs]
    for j in range(2):
        out_specs[j] = pl.BlockSpec((bt, tt, N_DIFF, 2 * DIFF_DH), lambda b, t: (b, t, 0, 0))
        out_shape[j] = jax.ShapeDtypeStruct((B, T, N_DIFF, 2 * DIFF_DH), F32)
    if transpose_values:
        assert bt == 1 and tt % KEY_BLOCK == 0
        nb = tt // KEY_BLOCK
        out_specs += [pl.BlockSpec((None, N_DIFF, nb, V_ROWS, KEY_BLOCK), lambda b, t: (b, 0, t, 0, 0)),
                      pl.BlockSpec((None, N_DSA_KV, nb, DSA_DH + V_PAD, KEY_BLOCK),
                                   lambda b, t: (b, 0, t, 0, 0))]
        out_shape += [jax.ShapeDtypeStruct((B, N_DIFF, T // KEY_BLOCK, V_ROWS, KEY_BLOCK), BF16),
                      jax.ShapeDtypeStruct((B, N_DSA_KV, T // KEY_BLOCK, DSA_DH + V_PAD, KEY_BLOCK), BF16)]
    else:
        out_specs += [row(DIFF_W), row(DSA_KV_W)]
        out_shape += [jax.ShapeDtypeStruct((B, T, DIFF_W), BF16),
                      jax.ShapeDtypeStruct((B, T, DSA_KV_W), BF16)]
    return pl.pallas_call(
        functools.partial(_proj_kernel, transpose_values=transpose_values),
        grid=grid,
        in_specs=[row(D_MODEL), per_b, per_b, full(g), full(wm), full(wih), full(wil), full(wi3)],
        out_specs=out_specs,
        out_shape=out_shape,
        name="in_proj",
        compiler_params=pltpu.CompilerParams(dimension_semantics=("arbitrary", "arbitrary"),
                                             vmem_limit_bytes=VMEM_LIMIT),
    )(x, sc, sh, g, wm, wih, wil, wi3)


def _fold_rows(x, rows, op):
    acc = x[0:rows]
    for r in range(1, x.shape[0] // rows):
        acc = op(acc, x[r * rows:(r + 1) * rows])
    return acc


def _init_bias_tiles(tab_ref, tb_ref):
    k = lax.broadcasted_iota(I32, (LANE, LANE), 0)
    q = lax.broadcasted_iota(I32, (LANE, LANE), 1)
    buckets = []
    for t in range(N_BIAS_TILES - 1):
        rel = k - q - LANE * t
        n = jnp.abs(rel)
        large = jnp.full_like(n, 8)
        for step in BUCKET_STEPS:
            large = large + jnp.where(n >= step, 1, 0)
        buckets.append(jnp.where(rel > 0, NUM_BUCKETS // 2, 0) + jnp.where(n < 8, n, large))

    def per_head(h, carry):
        for t, bucket in enumerate(buckets):
            val = jnp.zeros((LANE, LANE), F32)
            for b in range(NUM_BUCKETS):
                val = jnp.where(bucket == b, tab_ref[b, h] * LOG2E, val)
            tb_ref[h, t] = val
        tb_ref[h, N_BIAS_TILES - 1] = jnp.full((LANE, LANE), tab_ref[FAR_BUCKET, h] * LOG2E, F32)
        return carry

    lax.fori_loop(0, N_HEADS, per_head, 0)


def _attn_kernel(tab_ref, lam_ref, gsub_ref, dq_ref, sq_ref, iq_ref, iw_ref,
                 dk_ref, dv_ref, sk_ref, sv_ref, ik_ref, o_ref,
                 tb_ref, ltri_ref, key_ref, byte_ref, s_ref, acc_ref, ot_ref, *, tq, q_offset):
    tk = KEY_BLOCK
    b = pl.program_id(0)
    i = pl.program_id(1)

    @pl.when((b == 0) & (i == 0))
    def _():
        _init_bias_tiles(tab_ref, tb_ref)
        ltri_ref[...] = jnp.where(lax.broadcasted_iota(I32, (tk, tk), 1)
                                  < lax.broadcasted_iota(I32, (tk, tk), 0), 1.0, 0.0).astype(BF16)

    qpos0 = q_offset + i * tq
    lim_last = CHUNK * ((qpos0 + tq - 1) // CHUNK + 1)
    nblk = (lim_last + tk - 1) // tk
    qq = lax.broadcasted_iota(I32, (1, tq), 1)
    lim_q = CHUNK * ((qpos0 + qq) // CHUNK + 1)
    kk = lax.broadcasted_iota(I32, (tk, tq), 0)

    def key_rows(jb):
        return pl.ds(pl.multiple_of(jb * tk, tk), tk)

    def over_blocks(body, init, unroll=2):
        def run(first, n, carry):
            for u in range(n):
                carry = body(first + u, carry)
            return carry
        carry = lax.fori_loop(0, nblk // unroll, lambda jp, c: run(unroll * jp, unroll, c), init)
        n = unroll // 2
        while n >= 1:
            first = (nblk // (2 * n)) * (2 * n)
            carry = lax.fori_loop(0, (nblk // n) % 2, lambda _, c, first=first, n=n: run(first, n, c),
                                  carry)
            n //= 2
        return carry

    def bias_tile(head, jb):
        rows = []
        for a in range(tk // LANE):
            parts = []
            for c in range(tq // LANE):
                t = jnp.clip((qpos0 + c * LANE - (jb * tk + a * LANE)) // LANE, 0, N_BIAS_TILES - 1)
                parts.append(tb_ref[head, t])
            rows.append(jnp.concatenate(parts, axis=1))
        return jnp.concatenate(rows, axis=0)

    iw = iw_ref[...]

    def score_block(jb, carry):
        kc = ik_ref[key_rows(jb), :]
        sc = jnp.zeros((tk, tq), F32)
        for h in range(N_IDX):
            d = _dot_t(kc, iq_ref[:, h * 384:(h + 1) * 384])
            sc = sc + jnp.maximum(d, 0.0) * iw[h:h + 1]
        sc = jnp.where(kk + jb * tk < lim_q, sc, NEG)
        bits = pltpu.bitcast(sc, I32)
        key = bits ^ ((bits >> 31) & 0x7FFFFFFF)
        key_ref[jb] = key
        for lvl in range(KEY_BYTES):
            byte = ((key >> (8 * lvl)) & 0xFF) if lvl < KEY_BYTES - 1 else (key >> 24) + 128
            byte_ref[lvl, jb] = byte.astype(F32).astype(BF16)
        return carry

    over_blocks(score_block, 0, unroll=4)

    one16 = jnp.ones((), BF16)
    zero16 = jnp.zeros((), BF16)

    def count_digits(lvl, cand):
        def body(jb, acc):
            hit = jnp.where(byte_ref[lvl, jb] >= cand, one16, zero16)
            return acc + _fold_rows(hit, 2 * SUBLANE, jnp.add).astype(F32)
        acc = over_blocks(body, jnp.zeros((2 * SUBLANE, tq), F32))
        return jnp.sum(acc, axis=0, keepdims=True)

    want = jnp.full((1, tq), float(TOPK), F32)
    thr = jnp.zeros((1, tq), I32)
    for lvl in reversed(range(KEY_BYTES)):
        def digit_bit(it, state, lvl=lvl, want=want):
            digit, above = state
            cand = digit + jnp.left_shift(jnp.int32(1), 7 - it).astype(F32)
            cnt = count_digits(lvl, cand.astype(BF16))
            ok = cnt >= want
            return jnp.where(ok, cand, digit), jnp.where(ok, above, cnt)

        digit, above = lax.fori_loop(0, 8, digit_bit,
                                     (jnp.zeros((1, tq), F32), jnp.zeros((1, tq), F32)))
        want = want - above
        digit_i = digit.astype(I32)
        thr = thr | ((digit_i - 128) << 24 if lvl == KEY_BYTES - 1 else digit_i << (8 * lvl))
        if lvl > 0:
            def narrow(jb, carry, lvl=lvl, digit16=digit.astype(BF16)):
                byte_ref[lvl - 1, jb] = jnp.where(byte_ref[lvl, jb] == digit16,
                                                  byte_ref[lvl - 1, jb], -one16)
                return carry
            over_blocks(narrow, 0)

    def mask_block(jb, before):
        kb = key_ref[jb]
        eq = jnp.where(kb == thr, 1.0, 0.0)
        rank = _dot(ltri_ref[...], eq.astype(BF16)) + before
        keep = jnp.where(rank < want, eq, 0.0)
        sel = jnp.where(kb > thr, 0.0, (keep - 1.0) * (-NEG))
        sel = jnp.where(kk + jb * tk < lim_q, sel, NEG)
        key_ref[jb] = pltpu.bitcast(sel, I32)
        return before + jnp.sum(_fold_rows(eq, SUBLANE, jnp.add), axis=0, keepdims=True)

    over_blocks(mask_block, jnp.zeros((1, tq), F32))

    lam = lam_ref[...]
    lam_full = (jnp.exp(jnp.sum(lam[0:1] * lam[1:2], axis=1, keepdims=True))
                - jnp.exp(jnp.sum(lam[2:3] * lam[3:4], axis=1, keepdims=True)) + LAM_INIT)

    def dsa_group(pair):
        heads = [N_STAGE * pair + n for n in range(N_STAGE)]

        def scores(jb):
            kb = sk_ref[key_rows(jb), :]
            sel = pltpu.bitcast(key_ref[jb], F32)
            return [_dot_t(kb, sq_ref[:, h * LANE:(h + 1) * LANE]) + bias_tile(N_DIFF + h, jb) + sel
                    for h in heads]

        def finish():
            for n, h in enumerate(heads):
                a = acc_ref[n]
                out = a[0:DSA_DH] / a[DSA_DH:DSA_DH + 1]
                ot_ref[DIFF_W + h * DSA_DH:DIFF_W + (h + 1) * DSA_DH, :] = out

        return scores, (lambda n, jb: sv_ref[heads[n] // DSA_GROUP, jb]), finish

    def diff_group(h):
        def scores(jb):
            kb = dk_ref[key_rows(jb), h * LANE:(h + 1) * LANE]
            bias = bias_tile(h, jb) + jnp.where(kk + jb * tk < lim_q, 0.0, NEG)
            return [_dot_t(kb, dq_ref[:, (2 * h + part) * LANE:(2 * h + part + 1) * LANE]) + bias
                    for part in range(2)]

        def finish():
            a1 = acc_ref[0]
            a2 = acc_ref[1]
            od = (a1[0:LANE] / a1[LANE:LANE + 1]
                  - lam_full * (a2[0:LANE] / a2[LANE:LANE + 1]))
            od = od * lax.rsqrt(jnp.mean(od * od, axis=0, keepdims=True) + EPS)
            ot_ref[h * LANE:(h + 1) * LANE, :] = (od * gsub_ref[...]) * (1.0 - LAM_INIT)

        return scores, (lambda n, jb: dv_ref[h, jb]), finish

    groups = ([dsa_group(p) for p in range(N_DSA // N_STAGE)]
              + [diff_group(h) for h in range(N_DIFF)])
    m_prev = None
    for k in range(len(groups) + 1):
        produce = groups[k] if k < len(groups) else None
        consume = groups[k - 1] if k > 0 else None

        def step(jb, mx, k=k, produce=produce, consume=consume, m_prev=m_prev):
            if consume is not None:
                for n in range(N_STAGE):
                    p = jnp.exp2(s_ref[(k - 1) % 2, n, jb] - m_prev[n]).astype(BF16)
                    v = consume[1](n, jb)
                    acc_ref[n, 0:v.shape[0]] += _dot(v, p)
            if produce is None:
                return mx
            out = []
            for n, s in enumerate(produce[0](jb)):
                s_ref[k % 2, n, jb] = s
                out.append(jnp.maximum(mx[n], _fold_rows(s, SUBLANE, jnp.maximum)))
            return tuple(out)

        if consume is not None:
            for n in range(N_STAGE):
                acc_ref[n] = jnp.zeros((V_ROWS, tq), F32)
        mx = over_blocks(step, tuple(jnp.full((SUBLANE, tq), -3e38, F32) for _ in range(N_STAGE)),
                         unroll=8)
        if consume is not None:
            consume[2]()
        m_prev = [jnp.max(x, axis=0, keepdims=True) for x in mx]

    o_ref[...] = ot_ref[...].T.astype(o_ref.dtype)


def _attention(tab, lam4, gsub_t, dq, sq, iq, iw_t, dk, dv_t, sk, sv_t, ik, tq, q_offset):
    B, T, _ = dq.shape
    Lp = dk.shape[1]
    nblk_max = Lp // KEY_BLOCK
    qspec = lambda w: pl.BlockSpec((None, tq, w), lambda b, i: (b, i, 0))
    once = pl.Buffered(1)
    kspec = lambda w: pl.BlockSpec((None, Lp, w), lambda b, i: (b, 0, 0), pipeline_mode=once)
    small = lambda a: pl.BlockSpec(a.shape, lambda b, i: (0,) * a.ndim)
    kern = functools.partial(_attn_kernel, tq=tq, q_offset=q_offset)
    return pl.pallas_call(
        kern,
        grid=(B, T // tq),
        in_specs=[pl.BlockSpec(memory_space=pltpu.SMEM), small(lam4), small(gsub_t),
                  qspec(2 * DIFF_W), qspec(N_DSA * LANE), qspec(N_IDX * 384),
                  pl.BlockSpec((None, SUBLANE, tq), lambda b, i: (b, 0, i)),
                  kspec(DIFF_W),
                  pl.BlockSpec((None, N_DIFF, nblk_max, V_ROWS, KEY_BLOCK),
                               lambda b, i: (b, 0, 0, 0, 0), pipeline_mode=once),
                  kspec(DSA_KV_W),
                  pl.BlockSpec((None, N_DSA_KV, nblk_max, DSA_DH + V_PAD, KEY_BLOCK),
                               lambda b, i: (b, 0, 0, 0, 0), pipeline_mode=once),
                  kspec(384)],
        out_specs=pl.BlockSpec((None, tq, D_MODEL), lambda b, i: (b, i, 0)),
        out_shape=jax.ShapeDtypeStruct((B, T, D_MODEL), BF16),
        name="attn",
        scratch_shapes=[pltpu.VMEM((N_HEADS, N_BIAS_TILES, LANE, LANE), F32),
                        pltpu.VMEM((KEY_BLOCK, KEY_BLOCK), BF16),
                        pltpu.VMEM((nblk_max, KEY_BLOCK, tq), I32),
                        pltpu.VMEM((KEY_BYTES, nblk_max, KEY_BLOCK, tq), BF16),
                        pltpu.VMEM((2, N_STAGE, nblk_max, KEY_BLOCK, tq), F32),
                        pltpu.VMEM((N_STAGE, V_ROWS, tq), F32),
                        pltpu.VMEM((D_MODEL, tq), F32)],
        compiler_params=pltpu.CompilerParams(dimension_semantics=("arbitrary", "arbitrary"),
                                             vmem_limit_bytes=VMEM_LIMIT),
    )(tab, lam4, gsub_t, dq, sq, iq, iw_t, dk, dv_t, sk, sv_t, ik)


FF_CUTS = (0, 1536, D_FF)


def _ffn_kernel(x_ref, o_ref, ga1_ref, sc2_ref, sh2_ref, ga2_ref, gpm_ref, gpf_ref, gpo_ref,
                wo_ref, wgu_ref, wd_ref, y_ref):
    bt, tt, _ = x_ref.shape
    rows = bt * tt
    m = _dot(o_ref[...].reshape(rows, D_MODEL), wo_ref[...]).reshape(bt, tt, D_MODEL)
    x1 = x_ref[...] + ga1_ref[...] * (_rms(m) * gpm_ref[...])
    h2 = (_rms(x1) * gpf_ref[...]) * (1.0 + sc2_ref[...]) + sh2_ref[...]
    h2 = h2.reshape(rows, D_MODEL).astype(BF16)
    f = jnp.zeros((rows, D_MODEL), F32)
    for lo, hi in zip(FF_CUTS[:-1], FF_CUTS[1:]):
        g = _dot(h2, wgu_ref[:, lo:hi])
        u = _dot(h2, wgu_ref[:, D_FF + lo:D_FF + hi])
        a = (g * jax.nn.sigmoid(g) * u).astype(BF16)
        f = f + _dot(a, wd_ref[lo:hi, :])
    f = f.reshape(bt, tt, D_MODEL)
    y_ref[...] = x1 + ga2_ref[...] * (_rms(f) * gpo_ref[...])


def _out_ffn(x, o, ga1, sc2, sh2, ga2, gpm, gpf, gpo, wo, wgu, wd, bt, tt):
    B, T, _ = x.shape
    row = pl.BlockSpec((bt, tt, D_MODEL), lambda b, t: (b, t, 0))
    per_b = pl.BlockSpec((bt, 1, D_MODEL), lambda b, t: (b, 0, 0))
    full = lambda a: pl.BlockSpec(a.shape, lambda b, t: (0,) * a.ndim,
                                  pipeline_mode=pl.Buffered(1))
    return pl.pallas_call(
        _ffn_kernel,
        grid=(B // bt, T // tt),
        in_specs=[row, row, per_b, per_b, per_b, per_b, full(gpm), full(gpf), full(gpo),
                  full(wo), full(wgu), full(wd)],
        out_specs=row,
        out_shape=jax.ShapeDtypeStruct((B, T, D_MODEL), F32),
        name="out_ffn",
        compiler_params=pltpu.CompilerParams(dimension_semantics=("arbitrary", "arbitrary"),
                                             vmem_limit_bytes=VMEM_LIMIT),
    )(x, o, ga1, sc2, sh2, ga2, gpm, gpf, gpo, wo, wgu, wd)


def _split_hi_lo(a):
    hi = a.astype(BF16)
    lo = (a - hi.astype(F32)).astype(BF16)
    return hi, lo


def _prep_in_weights(w_in):
    cuts = [0, 512, 1024, 1536, 2048, 2176, 2304, 2560, 2624, 2628]
    wdq, wdk, wdv, wsq, wsk, wsv, wiq, wik, wiw = [w_in[:, a:b] for a, b in zip(cuts[:-1], cuts[1:])]
    scale = DIFF_DH ** -0.5 * LOG2E
    sq_head = lambda h: wsq[:, h * DSA_DH:(h + 1) * DSA_DH]
    sq_cols = [sq_head(j + g * DSA_GROUP) for j in range(DSA_GROUP) for g in range(N_DSA_KV)]
    wm = jnp.concatenate([wdq * scale, wdk, wdv] + [c * scale for c in sq_cols] + [wsk, wsv],
                         axis=1).astype(BF16)
    wi = jnp.concatenate([wiq, wik, wiw, jnp.zeros((D_MODEL, IDX_W - 324), F32)], axis=1)
    wih, wil = _split_hi_lo(wi)
    wi3 = (wi - wih.astype(F32) - wil.astype(F32)).astype(BF16)
    return wm, wih, wil, wi3


def _idx_key_operand(ik):
    kh, kl = _split_hi_lo(ik)
    k3 = (ik - kh.astype(F32) - kl.astype(F32)).astype(BF16)
    return jnp.concatenate([kh, kl, kh, k3, kh, kl], axis=-1)


def _pad_axis1(a, n):
    return jnp.pad(a, ((0, 0), (0, n - a.shape[1])) + ((0, 0),) * (a.ndim - 2))


def _value_operand(v):
    B, L, H, W = v.shape
    vt = jnp.transpose(v, (0, 2, 3, 1))
    extra = jnp.concatenate([jnp.ones((B, H, 1, L), BF16),
                             jnp.zeros((B, H, V_PAD - 1, L), BF16)], axis=2)
    vt = jnp.concatenate([vt, extra], axis=2).reshape(B, H, W + V_PAD, L // KEY_BLOCK, KEY_BLOCK)
    return jnp.transpose(vt, (0, 1, 3, 2, 4))


def _group(x, mods, past, params, bt, tt, tq):
    (g_pre_mix, g_post_mix, g_pre_ffn, g_post_ffn, wm, wih, wil, wi3, lam4, g_subln,
     wo, wgu, wd, tab) = params
    B, T, _ = x.shape
    sh1, sc1, ga1, sh2, sc2, ga2 = mods
    (dk32, dv32, sk32, sv32, ik32, iw32, dq16, dk16, sq16, sk16, iq16, ik16, dvx, svx) = _projection(
        x, sc1, sh1, g_pre_mix, wm, wih, wil, wi3, bt, tt, transpose_values=past is None)
    iw = iw32[..., IDX_DH:IDX_DH + N_IDX] * (1.0 / 16.0)
    if past is None:
        q_offset = 0
        dk, dv_t, sk, sv_t, ikc = dk16, dvx, sk16, svx, ik16
    else:
        pk, pv, psk, psv, pik = past
        P = pk.shape[1]
        q_offset = P
        lp = -(-(P + T) // KEY_BLOCK) * KEY_BLOCK
        cat = lambda p, n: _pad_axis1(
            jnp.concatenate([p.reshape(B, P, -1).astype(BF16), n], axis=1), lp)
        dk, dv, sk, sv = cat(pk, dk16), cat(pv, dvx), cat(psk, sk16), cat(psv, svx)
        ikc = _pad_axis1(jnp.concatenate([_idx_key_operand(pik), ik16], axis=1), lp)
        dv_t = _value_operand(dv.reshape(B, lp, N_DIFF, 2 * DIFF_DH))
        sv_t = _value_operand(sv.reshape(B, lp, N_DSA_KV, DSA_DH))
    tpad = -(-T // tq) * tq
    qpad = lambda a: _pad_axis1(a, tpad)
    iw_t = _pad_axis1(jnp.transpose(qpad(iw), (0, 2, 1)), SUBLANE)
    gsub_t = jnp.broadcast_to(g_subln.reshape(-1, 1), (2 * DIFF_DH, tq))
    o = _attention(tab, lam4, gsub_t, qpad(dq16), qpad(sq16), qpad(iq16), iw_t,
                   dk, dv_t, sk, sv_t, ikc, tq, q_offset)[:, :T]
    y = _out_ffn(x, o, ga1, sc2, sh2, ga2, g_post_mix, g_pre_ffn, g_post_ffn, wo, wgu, wd, bt, tt)
    rows = (dk32[None], dv32[None],
            sk32.reshape(1, B, T, N_DSA_KV, DSA_DH), sv32.reshape(1, B, T, N_DSA_KV, DSA_DH),
            ik32.reshape(1, B, T, IDX_DH))
    return y, rows


def kernel(x_prompt, x_sample, cache_diff_k, cache_diff_v, cache_dsa_k, cache_dsa_v, cache_dsa_kidx,
           c_prompt, c_sample, w_ada, b_ada, g_pre_mix, g_post_mix, g_pre_ffn, g_post_ffn, w_in,
           lambda_q1, lambda_k1, lambda_q2, lambda_k2, g_subln, w_out, w_gate_up, w_down,
           rel_bias_table):
    Bp = x_prompt.shape[0]
    c_all = jnp.concatenate([c_prompt, c_sample], axis=0)
    mod = _modulation(c_all, w_ada[0], b_ada[0])
    mods = [mod[:, j * D_MODEL:(j + 1) * D_MODEL][:, None, :] for j in range(6)]
    mods_p = [m[:Bp] for m in mods]
    mods_s = [m[Bp:] for m in mods]

    wm, wih, wil, wi3 = _prep_in_weights(w_in[0])
    lam4 = jnp.concatenate([lambda_q1, lambda_k1, lambda_q2, lambda_k2], axis=0)
    row = lambda g: g[0].reshape(1, -1)
    params = (row(g_pre_mix), row(g_post_mix), row(g_pre_ffn), row(g_post_ffn), wm, wih, wil, wi3,
              lam4, g_subln[0], w_out[0].astype(BF16), w_gate_up[0].astype(BF16),
              w_down[0].astype(BF16), rel_bias_table)

    yp, rp = _group(x_prompt, mods_p, None, params, bt=1, tt=512, tq=256)
    past = (cache_diff_k[0], cache_diff_v[0], cache_dsa_k[0], cache_dsa_v[0], cache_dsa_kidx[0])
    ys, rs = _group(x_sample, mods_s, past, params, bt=8, tt=64, tq=128)
    return (yp, ys) + rp + rs
```

```python
import functools
import math

import jax
import jax.numpy as jnp
from jax import lax
from jax.experimental import pallas as pl
from jax.experimental.pallas import tpu as pltpu

F32 = jnp.float32
BF16 = jnp.bfloat16
I32 = jnp.int32

D_MODEL = 1024
CHUNK = 64
N_DIFF = 4
DIFF_DH = 64
DIFF_W = N_DIFF * 2 * DIFF_DH
N_DSA = 8
N_DSA_KV = 2
DSA_GROUP = N_DSA // N_DSA_KV
DSA_DH = 64
DSA_W = N_DSA * DSA_DH
DSA_KV_W = N_DSA_KV * DSA_DH
N_IDX = 4
IDX_DH = 64
TOPK = 256
N_HEADS = N_DIFF + N_DSA
NUM_BUCKETS = 32
D_FF = 2816
EPS = 1e-6
NEG = -1e30
LOG2E = math.log2(math.e)
LAM_INIT = 0.8 - 0.6 * math.exp(-0.3 * 0)

LANE = 128
SUBLANE = 8
KEY_BLOCK = 256
V_PAD = 16
V_ROWS = 2 * DIFF_DH + V_PAD
VMEM_LIMIT = 56 * 1024 * 1024
FAR_BUCKET = 15
BUCKET_STEPS = (13, 20, 30, 46, 70, 108, 166)
KEY_BYTES = 4
N_BIAS_TILES = 4
N_STAGE = 2


def _rms(x):
    return x * lax.rsqrt(jnp.mean(x * x, axis=-1, keepdims=True) + EPS)


def _dot(a, b):
    return jnp.dot(a, b, preferred_element_type=F32)


def _dot_t(a, b):
    return lax.dot_general(a, b, (((1,), (1,)), ((), ())), preferred_element_type=F32)


def _mod_kernel(c_ref, w_ref, b_ref, o_ref):
    c = c_ref[...]
    s = c * jax.nn.sigmoid(c)
    o_ref[...] = jnp.dot(s, w_ref[...], preferred_element_type=F32,
                         precision=lax.Precision.HIGHEST) + b_ref[...]


def _modulation(c_all, w_ada, b_ada):
    n = c_all.shape[0]
    tn = 1024
    return pl.pallas_call(
        _mod_kernel,
        grid=(6 * D_MODEL // tn,),
        in_specs=[pl.BlockSpec((n, D_MODEL), lambda j: (0, 0)),
                  pl.BlockSpec((D_MODEL, tn), lambda j: (0, j)),
                  pl.BlockSpec((1, tn), lambda j: (0, j))],
        out_specs=pl.BlockSpec((n, tn), lambda j: (0, j)),
        out_shape=jax.ShapeDtypeStruct((n, 6 * D_MODEL), F32),
        name="adaln_mod",
        compiler_params=pltpu.CompilerParams(dimension_semantics=("arbitrary",),
                                             vmem_limit_bytes=VMEM_LIMIT),
    )(c_all, w_ada, b_ada.reshape(1, -1))


_C_DQ = 0
_C_DK = _C_DQ + DIFF_W
_C_DV = _C_DK + DIFF_W
_C_SQ = _C_DV + DIFF_W
_C_SK = _C_SQ + DSA_W
_C_SV = _C_SK + DSA_KV_W
_C_END = _C_SV + DSA_KV_W
IDX_W = 384


def _hi_lo(a):
    hi = a.astype(BF16)
    return hi, (a - hi.astype(F32)).astype(BF16)


def _split3(a):
    a1 = a.astype(BF16)
    r = a - a1.astype(F32)
    a2 = r.astype(BF16)
    return a1, a2, (r - a2.astype(F32)).astype(BF16)


def _value_rows(v):
    row = lax.broadcasted_iota(I32, (V_PAD, KEY_BLOCK), 0)
    extra = jnp.where(row == 0, 1.0, 0.0)
    return jnp.concatenate([v.T, extra], axis=0).astype(BF16)


def _proj_kernel(x_ref, sc_ref, sh_ref, g_ref, wm_ref, wih_ref, wil_ref, wi3_ref,
                 dk32, dv32, sk32, sv32, ik32, iw32, dq16, dk16, sq16, sk16, iq16, ik16, dvx, svx,
                 *, transpose_values):
    bt, tt, _ = x_ref.shape
    x = x_ref[...]
    h = (_rms(x) * g_ref[...]) * (1.0 + sc_ref[...]) + sh_ref[...]
    h = h.reshape(bt * tt, D_MODEL)
    hb, hl, h3 = _split3(h)
    z = _dot(hb, wm_ref[...])
    wih = wih_ref[...]
    wil = wil_ref[...]
    zi = (_dot(hb, wih) + _dot(hb, wil) + _dot(hl, wih)
          + _dot(hb, wi3_ref[...]) + _dot(h3, wih) + _dot(hl, wil))

    def put(ref, v):
        ref[...] = v.reshape(bt, tt, v.shape[-1]).astype(ref.dtype)

    zdk = z[:, _C_DK:_C_DV]
    zdv = z[:, _C_DV:_C_SQ]
    zsk = z[:, _C_SK:_C_SV]
    zsv = z[:, _C_SV:_C_END]
    zik = zi[:, 256:256 + IDX_DH]
    for n in range(N_DIFF):
        dk32[:, :, n, :] = zdk[:, n * LANE:(n + 1) * LANE].reshape(bt, tt, LANE)
        dv32[:, :, n, :] = zdv[:, n * LANE:(n + 1) * LANE].reshape(bt, tt, LANE)
    put(sk32, zsk)
    put(sv32, zsv)
    put(ik32, zik)
    put(iw32, zi[:, 256:IDX_W])
    low = lax.broadcasted_iota(I32, (bt * tt, LANE), 1) < DIFF_DH
    halves = lambda v: (jnp.where(low, v, 0.0), jnp.where(low, 0.0, v))
    dq_pairs = [halves(z[:, _C_DQ + n * LANE:_C_DQ + (n + 1) * LANE]) for n in range(N_DIFF)]
    sq_pairs = [halves(z[:, _C_SQ + j * LANE:_C_SQ + (j + 1) * LANE]) for j in range(DSA_GROUP)]
    put(dq16, jnp.concatenate([part for pair in dq_pairs for part in pair], axis=1))
    put(dk16, zdk)
    put(sq16, jnp.concatenate([pair[0] for pair in sq_pairs] + [pair[1] for pair in sq_pairs], axis=1))
    put(sk16, zsk)

    q1, q2, q3 = _split3(zi[:, 0:N_IDX * IDX_DH])
    k1, k2, k3 = _split3(zik)
    head = lambda a, n: a[:, n * IDX_DH:(n + 1) * IDX_DH]
    put(iq16, jnp.concatenate([head(p, n) for n in range(N_IDX)
                               for p in (q1, q1, q2, q1, q3, q2)], axis=1))
    put(ik16, jnp.concatenate([k1, k2, k1, k3, k1, k2], axis=1))

    if transpose_values:
        for kb in range(tt // KEY_BLOCK):
            rows = slice(kb * KEY_BLOCK, (kb + 1) * KEY_BLOCK)
            for n in range(N_DIFF):
                dvx[n, kb] = _value_rows(zdv[rows, n * LANE:(n + 1) * LANE])
            for g in range(N_DSA_KV):
                svx[g, kb] = _value_rows(zsv[rows, g * DSA_DH:(g + 1) * DSA_DH])
    else:
        put(dvx, zdv)
        put(svx, zsv)


def _projection(x, sc, sh, g, wm, wih, wil, wi3, bt, tt, transpose_values):
    B, T, _ = x.shape
    grid = (B // bt, T // tt)
    row = lambda w: pl.BlockSpec((bt, tt, w), lambda b, t: (b, t, 0))
    per_b = pl.BlockSpec((bt, 1, D_MODEL), lambda b, t: (b, 0, 0))
    full = lambda a: pl.BlockSpec(a.shape, lambda b, t: (0,) * a.ndim,
                                  pipeline_mode=pl.Buffered(1))
    outs = [(DIFF_W, F32), (DIFF_W, F32), (DSA_KV_W, F32), (DSA_KV_W, F32), (IDX_DH, F32),
            (LANE, F32), (2 * DIFF_W, BF16), (DIFF_W, BF16), (N_DSA * LANE, BF16),
            (DSA_KV_W, BF16), (N_IDX * 384, BF16), (384, BF16)]
    out_specs = [row(w) for w, _ in outs]
    out_shape = [jax.ShapeDtypeStruct((B, T, w), dt) for w, dt in outs]
    for j in range(2):
        out_specs[j] = pl.BlockSpec((bt, tt, N_DIFF, 2 * DIFF_DH), lambda b, t: (b, t, 0, 0))
        out_shape[j] = jax.ShapeDtypeStruct((B, T, N_DIFF, 2 * DIFF_DH), F32)
    if transpose_values:
        assert bt == 1 and tt % KEY_BLOCK == 0
        nb = tt // KEY_BLOCK
        out_specs += [pl.BlockSpec((None, N_DIFF, nb, V_ROWS, KEY_BLOCK), lambda b, t: (b, 0, t, 0, 0)),
                      pl.BlockSpec((None, N_DSA_KV, nb, DSA_DH + V_PAD, KEY_BLOCK),
                                   lambda b, t: (b, 0, t, 0, 0))]
        out_shape += [jax.ShapeDtypeStruct((B, N_DIFF, T // KEY_BLOCK, V_ROWS, KEY_BLOCK), BF16),
                      jax.ShapeDtypeStruct((B, N_DSA_KV, T // KEY_BLOCK, DSA_DH + V_PAD, KEY_BLOCK), BF16)]
    else:
        out_specs += [row(DIFF_W), row(DSA_KV_W)]
        out_shape += [jax.ShapeDtypeStruct((B, T, DIFF_W), BF16),
                      jax.ShapeDtypeStruct((B, T, DSA_KV_W), BF16)]
    return pl.pallas_call(
        functools.partial(_proj_kernel, transpose_values=transpose_values),
        grid=grid,
        in_specs=[row(D_MODEL), per_b, per_b, full(g), full(wm), full(wih), full(wil), full(wi3)],
        out_specs=out_specs,
        out_shape=out_shape,
        name="in_proj",
        compiler_params=pltpu.CompilerParams(dimension_semantics=("arbitrary", "arbitrary"),
                                             vmem_limit_bytes=VMEM_LIMIT),
    )(x, sc, sh, g, wm, wih, wil, wi3)


def _fold_rows(x, rows, op):
    acc = x[0:rows]
    for r in range(1, x.shape[0] // rows):
        acc = op(acc, x[r * rows:(r + 1) * rows])
    return acc


def _init_bias_tiles(tab_ref, tb_ref):
    k = lax.broadcasted_iota(I32, (LANE, LANE), 0)
    q = lax.broadcasted_iota(I32, (LANE, LANE), 1)
    buckets = []
    for t in range(N_BIAS_TILES - 1):
        rel = k - q - LANE * t
        n = jnp.abs(rel)
        large = jnp.full_like(n, 8)
        for step in BUCKET_STEPS:
            large = large + jnp.where(n >= step, 1, 0)
        buckets.append(jnp.where(rel > 0, NUM_BUCKETS // 2, 0) + jnp.where(n < 8, n, large))

    def per_head(h, carry):
        for t, bucket in enumerate(buckets):
            val = jnp.zeros((LANE, LANE), F32)
            for b in range(NUM_BUCKETS):
                val = jnp.where(bucket == b, tab_ref[b, h] * LOG2E, val)
            tb_ref[h, t] = val
        tb_ref[h, N_BIAS_TILES - 1] = jnp.full((LANE, LANE), tab_ref[FAR_BUCKET, h] * LOG2E, F32)
        return carry

    lax.fori_loop(0, N_HEADS, per_head, 0)


def _attn_kernel(tab_ref, lam_ref, gsub_ref, dq_ref, sq_ref, iq_ref, iw_ref,
                 dk_ref, dv_ref, sk_ref, sv_ref, ik_ref, o_ref,
                 tb_ref, ltri_ref, key_ref, byte_ref, s_ref, acc_ref, ot_ref, *, tq, q_offset):
    tk = KEY_BLOCK
    b = pl.program_id(0)
    i = pl.program_id(1)

    @pl.when((b == 0) & (i == 0))
    def _():
        _init_bias_tiles(tab_ref, tb_ref)
        ltri_ref[...] = jnp.where(lax.broadcasted_iota(I32, (tk, tk), 1)
                                  < lax.broadcasted_iota(I32, (tk, tk), 0), 1.0, 0.0).astype(BF16)

    qpos0 = q_offset + i * tq
    lim_last = CHUNK * ((qpos0 + tq - 1) // CHUNK + 1)
    nblk = (lim_last + tk - 1) // tk
    qq = lax.broadcasted_iota(I32, (1, tq), 1)
    lim_q = CHUNK * ((qpos0 + qq) // CHUNK + 1)
    kk = lax.broadcasted_iota(I32, (tk, tq), 0)

    def key_rows(jb):
        return pl.ds(pl.multiple_of(jb * tk, tk), tk)

    def over_blocks(body, init, unroll=4):
        def run(first, n, carry):
            for u in range(n):
                carry = body(first + u, carry)
            return carry
        carry = lax.fori_loop(0, nblk // unroll, lambda jp, c: run(unroll * jp, unroll, c), init)
        n = unroll // 2
        while n >= 1:
            first = (nblk // (2 * n)) * (2 * n)
            carry = lax.fori_loop(0, (nblk // n) % 2, lambda _, c, first=first, n=n: run(first, n, c),
                                  carry)
            n //= 2
        return carry

    def bias_tile(head, jb):
        rows = []
        for a in range(tk // LANE):
            parts = []
            for c in range(tq // LANE):
                t = jnp.clip((qpos0 + c * LANE - (jb * tk + a * LANE)) // LANE, 0, N_BIAS_TILES - 1)
                parts.append(tb_ref[head, t])
            rows.append(jnp.concatenate(parts, axis=1))
        return jnp.concatenate(rows, axis=0)

    iw = iw_ref[...]

    def score_block(jb, carry):
        kc = ik_ref[key_rows(jb), :]
        sc = jnp.zeros((tk, tq), F32)
        for h in range(N_IDX):
            d = _dot_t(kc, iq_ref[:, h * 384:(h + 1) * 384])
            sc = sc + jnp.maximum(d, 0.0) * iw[h:h + 1]
        sc = jnp.where(kk + jb * tk < lim_q, sc, NEG)
        bits = pltpu.bitcast(sc, I32)
        key = bits ^ ((bits >> 31) & 0x7FFFFFFF)
        key_ref[jb] = key
        for lvl in range(KEY_BYTES):
            byte = ((key >> (8 * lvl)) & 0xFF) if lvl < KEY_BYTES - 1 else (key >> 24) + 128
            byte_ref[lvl, jb] = byte.astype(F32).astype(BF16)
        return carry

    over_blocks(score_block, 0, unroll=4)

    one16 = jnp.ones((), BF16)
    zero16 = jnp.zeros((), BF16)

    def count_digits(lvl, cand):
        def body(jb, acc):
            hit = jnp.where(byte_ref[lvl, jb] >= cand, one16, zero16)
            return acc + _fold_rows(hit, 2 * SUBLANE, jnp.add).astype(F32)
        acc = over_blocks(body, jnp.zeros((2 * SUBLANE, tq), F32))
        return jnp.sum(acc, axis=0, keepdims=True)

    want = jnp.full((1, tq), float(TOPK), F32)
    thr = jnp.zeros((1, tq), I32)
    for lvl in reversed(range(KEY_BYTES)):
        def digit_bit(it, state, lvl=lvl, want=want):
            digit, above = state
            cand = digit + jnp.left_shift(jnp.int32(1), 7 - it).astype(F32)
            cnt = count_digits(lvl, cand.astype(BF16))
            ok = cnt >= want
            return jnp.where(ok, cand, digit), jnp.where(ok, above, cnt)

        digit, above = lax.fori_loop(0, 8, digit_bit,
                                     (jnp.zeros((1, tq), F32), jnp.zeros((1, tq), F32)))
        want = want - above
        digit_i = digit.astype(I32)
        thr = thr | ((digit_i - 128) << 24 if lvl == KEY_BYTES - 1 else digit_i << (8 * lvl))
        if lvl > 0:
            def narrow(jb, carry, lvl=lvl, digit16=digit.astype(BF16)):
                byte_ref[lvl - 1, jb] = jnp.where(byte_ref[lvl, jb] == digit16,
                                                  byte_ref[lvl - 1, jb], -one16)
                return carry
            over_blocks(narrow, 0)

    def mask_block(jb, before):
        kb = key_ref[jb]
        eq = jnp.where(kb == thr, 1.0, 0.0)
        rank = _dot(ltri_ref[...], eq.astype(BF16)) + before
        keep = jnp.where(rank < want, eq, 0.0)
        sel = jnp.where(kb > thr, 0.0, (keep - 1.0) * (-NEG))
        sel = jnp.where(kk + jb * tk < lim_q, sel, NEG)
        key_ref[jb] = pltpu.bitcast(sel, I32)
        return before + jnp.sum(_fold_rows(eq, SUBLANE, jnp.add), axis=0, keepdims=True)

    over_blocks(mask_block, jnp.zeros((1, tq), F32))

    lam = lam_ref[...]
    lam_full = (jnp.exp(jnp.sum(lam[0:1] * lam[1:2], axis=1, keepdims=True))
                - jnp.exp(jnp.sum(lam[2:3] * lam[3:4], axis=1, keepdims=True)) + LAM_INIT)

    def dsa_group(pair):
        heads = [N_STAGE * pair + n for n in range(N_STAGE)]

        def scores(jb):
            kb = sk_ref[key_rows(jb), :]
            sel = pltpu.bitcast(key_ref[jb], F32)
            return [_dot_t(kb, sq_ref[:, h * LANE:(h + 1) * LANE]) + bias_tile(N_DIFF + h, jb) + sel
                    for h in heads]

        def finish():
            for n, h in enumerate(heads):
                a = acc_ref[n]
                out = a[0:DSA_DH] / a[DSA_DH:DSA_DH + 1]
                ot_ref[DIFF_W + h * DSA_DH:DIFF_W + (h + 1) * DSA_DH, :] = out

        return scores, (lambda n, jb: sv_ref[heads[n] // DSA_GROUP, jb]), finish

    def diff_group(h):
        def scores(jb):
            kb = dk_ref[key_rows(jb), h * LANE:(h + 1) * LANE]
            bias = bias_tile(h, jb) + jnp.where(kk + jb * tk < lim_q, 0.0, NEG)
            return [_dot_t(kb, dq_ref[:, (2 * h + part) * LANE:(2 * h + part + 1) * LANE]) + bias
                    for part in range(2)]

        def finish():
            a1 = acc_ref[0]
            a2 = acc_ref[1]
            od = (a1[0:LANE] / a1[LANE:LANE + 1]
                  - lam_full * (a2[0:LANE] / a2[LANE:LANE + 1]))
            od = od * lax.rsqrt(jnp.mean(od * od, axis=0, keepdims=True) + EPS)
            ot_ref[h * LANE:(h + 1) * LANE, :] = (od * gsub_ref[...]) * (1.0 - LAM_INIT)

        return scores, (lambda n, jb: dv_ref[h, jb]), finish

    groups = ([dsa_group(p) for p in range(N_DSA // N_STAGE)]
              + [diff_group(h) for h in range(N_DIFF)])
    m_prev = None
    for k in range(len(groups) + 1):
        produce = groups[k] if k < len(groups) else None
        consume = groups[k - 1] if k > 0 else None

        def step(jb, mx, k=k, produce=produce, consume=consume, m_prev=m_prev):
            if consume is not None:
                for n in range(N_STAGE):
                    p = jnp.exp2(s_ref[(k - 1) % 2, n, jb] - m_prev[n]).astype(BF16)
                    v = consume[1](n, jb)
                    acc_ref[n, 0:v.shape[0]] += _dot(v, p)
            if produce is None:
                return mx
            out = []
            for n, s in enumerate(produce[0](jb)):
                s_ref[k % 2, n, jb] = s
                out.append(jnp.maximum(mx[n], _fold_rows(s, SUBLANE, jnp.maximum)))
            return tuple(out)

        if consume is not None:
            for n in range(N_STAGE):
                acc_ref[n] = jnp.zeros((V_ROWS, tq), F32)
        mx = over_blocks(step, tuple(jnp.full((SUBLANE, tq), -3e38, F32) for _ in range(N_STAGE)),
                         unroll=8)
        if consume is not None:
            consume[2]()
        m_prev = [jnp.max(x, axis=0, keepdims=True) for x in mx]

    o_ref[...] = ot_ref[...].T.astype(o_ref.dtype)


def _attention(tab, lam4, gsub_t, dq, sq, iq, iw_t, dk, dv_t, sk, sv_t, ik, tq, q_offset):
    B, T, _ = dq.shape
    Lp = dk.shape[1]
    nblk_max = Lp // KEY_BLOCK
    qspec = lambda w: pl.BlockSpec((None, tq, w), lambda b, i: (b, i, 0))
    once = pl.Buffered(1)
    kspec = lambda w: pl.BlockSpec((None, Lp, w), lambda b, i: (b, 0, 0), pipeline_mode=once)
    small = lambda a: pl.BlockSpec(a.shape, lambda b, i: (0,) * a.ndim)
    kern = functools.partial(_attn_kernel, tq=tq, q_offset=q_offset)
    return pl.pallas_call(
        kern,
        grid=(B, T // tq),
        in_specs=[pl.BlockSpec(memory_space=pltpu.SMEM), small(lam4), small(gsub_t),
                  qspec(2 * DIFF_W), qspec(N_DSA * LANE), qspec(N_IDX * 384),
                  pl.BlockSpec((None, SUBLANE, tq), lambda b, i: (b, 0, i)),
                  kspec(DIFF_W),
                  pl.BlockSpec((None, N_DIFF, nblk_max, V_ROWS, KEY_BLOCK),
                               lambda b, i: (b, 0, 0, 0, 0), pipeline_mode=once),
                  kspec(DSA_KV_W),
                  pl.BlockSpec((None, N_DSA_KV, nblk_max, DSA_DH + V_PAD, KEY_BLOCK),
                               lambda b, i: (b, 0, 0, 0, 0), pipeline_mode=once),
                  kspec(384)],
        out_specs=pl.BlockSpec((None, tq, D_MODEL), lambda b, i: (b, i, 0)),
        out_shape=jax.ShapeDtypeStruct((B, T, D_MODEL), BF16),
        name="attn",
        scratch_shapes=[pltpu.VMEM((N_HEADS, N_BIAS_TILES, LANE, LANE), F32),
                        pltpu.VMEM((KEY_BLOCK, KEY_BLOCK), BF16),
                        pltpu.VMEM((nblk_max, KEY_BLOCK, tq), I32),
                        pltpu.VMEM((KEY_BYTES, nblk_max, KEY_BLOCK, tq), BF16),
                        pltpu.VMEM((2, N_STAGE, nblk_max, KEY_BLOCK, tq), F32),
                        pltpu.VMEM((N_STAGE, V_ROWS, tq), F32),
                        pltpu.VMEM((D_MODEL, tq), F32)],
        compiler_params=pltpu.CompilerParams(dimension_semantics=("arbitrary", "arbitrary"),
                                             vmem_limit_bytes=VMEM_LIMIT),
    )(tab, lam4, gsub_t, dq, sq, iq, iw_t, dk, dv_t, sk, sv_t, ik)


FF_CUTS = (0, 1536, D_FF)


def _ffn_kernel(x_ref, o_ref, ga1_ref, sc2_ref, sh2_ref, ga2_ref, gpm_ref, gpf_ref, gpo_ref,
                wo_ref, wgu_ref, wd_ref, y_ref):
    bt, tt, _ = x_ref.shape
    rows = bt * tt
    m = _dot(o_ref[...].reshape(rows, D_MODEL), wo_ref[...]).reshape(bt, tt, D_MODEL)
    x1 = x_ref[...] + ga1_ref[...] * (_rms(m) * gpm_ref[...])
    h2 = (_rms(x1) * gpf_ref[...]) * (1.0 + sc2_ref[...]) + sh2_ref[...]
    h2 = h2.reshape(rows, D_MODEL).astype(BF16)
    f = jnp.zeros((rows, D_MODEL), F32)
    for lo, hi in zip(FF_CUTS[:-1], FF_CUTS[1:]):
        g = _dot(h2, wgu_ref[:, lo:hi])
        u = _dot(h2, wgu_ref[:, D_FF + lo:D_FF + hi])
        a = (g * jax.nn.sigmoid(g) * u).astype(BF16)
        f = f + _dot(a, wd_ref[lo:hi, :])
    f = f.reshape(bt, tt, D_MODEL)
    y_ref[...] = x1 + ga2_ref[...] * (_rms(f) * gpo_ref[...])


def _out_ffn(x, o, ga1, sc2, sh2, ga2, gpm, gpf, gpo, wo, wgu, wd, bt, tt):
    B, T, _ = x.shape
    row = pl.BlockSpec((bt, tt, D_MODEL), lambda b, t: (b, t, 0))
    per_b = pl.BlockSpec((bt, 1, D_MODEL), lambda b, t: (b, 0, 0))
    full = lambda a: pl.BlockSpec(a.shape, lambda b, t: (0,) * a.ndim,
                                  pipeline_mode=pl.Buffered(1))
    return pl.pallas_call(
        _ffn_kernel,
        grid=(B // bt, T // tt),
        in_specs=[row, row, per_b, per_b, per_b, per_b, full(gpm), full(gpf), full(gpo),
                  full(wo), full(wgu), full(wd)],
        out_specs=row,
        out_shape=jax.ShapeDtypeStruct((B, T, D_MODEL), F32),
        name="out_ffn",
        compiler_params=pltpu.CompilerParams(dimension_semantics=("arbitrary", "arbitrary"),
                                             vmem_limit_bytes=VMEM_LIMIT),
    )(x, o, ga1, sc2, sh2, ga2, gpm, gpf, gpo, wo, wgu, wd)


def _split_hi_lo(a):
    hi = a.astype(BF16)
    lo = (a - hi.astype(F32)).astype(BF16)
    return hi, lo


def _prep_in_weights(w_in):
    cuts = [0, 512, 1024, 1536, 2048, 2176, 2304, 2560, 2624, 2628]
    wdq, wdk, wdv, wsq, wsk, wsv, wiq, wik, wiw = [w_in[:, a:b] for a, b in zip(cuts[:-1], cuts[1:])]
    scale = DIFF_DH ** -0.5 * LOG2E
    sq_head = lambda h: wsq[:, h * DSA_DH:(h + 1) * DSA_DH]
    sq_cols = [sq_head(j + g * DSA_GROUP) for j in range(DSA_GROUP) for g in range(N_DSA_KV)]
    wm = jnp.concatenate([wdq * scale, wdk, wdv] + [c * scale for c in sq_cols] + [wsk, wsv],
                         axis=1).astype(BF16)
    wi = jnp.concatenate([wiq, wik, wiw, jnp.zeros((D_MODEL, IDX_W - 324), F32)], axis=1)
    wih, wil = _split_hi_lo(wi)
    wi3 = (wi - wih.astype(F32) - wil.astype(F32)).astype(BF16)
    return wm, wih, wil, wi3


def _idx_key_operand(ik):
    kh, kl = _split_hi_lo(ik)
    k3 = (ik - kh.astype(F32) - kl.astype(F32)).astype(BF16)
    return jnp.concatenate([kh, kl, kh, k3, kh, kl], axis=-1)


def _pad_axis1(a, n):
    return jnp.pad(a, ((0, 0), (0, n - a.shape[1])) + ((0, 0),) * (a.ndim - 2))


def _value_operand(v):
    B, L, H, W = v.shape
    vt = jnp.transpose(v, (0, 2, 3, 1))
    extra = jnp.concatenate([jnp.ones((B, H, 1, L), BF16),
                             jnp.zeros((B, H, V_PAD - 1, L), BF16)], axis=2)
    vt = jnp.concatenate([vt, extra], axis=2).reshape(B, H, W + V_PAD, L // KEY_BLOCK, KEY_BLOCK)
    return jnp.transpose(vt, (0, 1, 3, 2, 4))


def _group(x, mods, past, params, bt, tt, tq):
    (g_pre_mix, g_post_mix, g_pre_ffn, g_post_ffn, wm, wih, wil, wi3, lam4, g_subln,
     wo, wgu, wd, tab) = params
    B, T, _ = x.shape
    sh1, sc1, ga1, sh2, sc2, ga2 = mods
    (dk32, dv32, sk32, sv32, ik32, iw32, dq16, dk16, sq16, sk16, iq16, ik16, dvx, svx) = _projection(
        x, sc1, sh1, g_pre_mix, wm, wih, wil, wi3, bt, tt, transpose_values=past is None)
    iw = iw32[..., IDX_DH:IDX_DH + N_IDX] * (1.0 / 16.0)
    if past is None:
        q_offset = 0
        dk, dv_t, sk, sv_t, ikc = dk16, dvx, sk16, svx, ik16
    else:
        pk, pv, psk, psv, pik = past
        P = pk.shape[1]
        q_offset = P
        lp = -(-(P + T) // KEY_BLOCK) * KEY_BLOCK
        cat = lambda p, n: _pad_axis1(
            jnp.concatenate([p.reshape(B, P, -1).astype(BF16), n], axis=1), lp)
        dk, dv, sk, sv = cat(pk, dk16), cat(pv, dvx), cat(psk, sk16), cat(psv, svx)
        ikc = _pad_axis1(jnp.concatenate([_idx_key_operand(pik), ik16], axis=1), lp)
        dv_t = _value_operand(dv.reshape(B, lp, N_DIFF, 2 * DIFF_DH))
        sv_t = _value_operand(sv.reshape(B, lp, N_DSA_KV, DSA_DH))
    tpad = -(-T // tq) * tq
    qpad = lambda a: _pad_axis1(a, tpad)
    iw_t = _pad_axis1(jnp.transpose(qpad(iw), (0, 2, 1)), SUBLANE)
    gsub_t = jnp.broadcast_to(g_subln.reshape(-1, 1), (2 * DIFF_DH, tq))
    o = _attention(tab, lam4, gsub_t, qpad(dq16), qpad(sq16), qpad(iq16), iw_t,
                   dk, dv_t, sk, sv_t, ikc, tq, q_offset)[:, :T]
    y = _out_ffn(x, o, ga1, sc2, sh2, ga2, g_post_mix, g_pre_ffn, g_post_ffn, wo, wgu, wd, bt, tt)
    rows = (dk32[None], dv32[None],
            sk32.reshape(1, B, T, N_DSA_KV, DSA_DH), sv32.reshape(1, B, T, N_DSA_KV, DSA_DH),
            ik32.reshape(1, B, T, IDX_DH))
    return y, rows


def kernel(x_prompt, x_sample, cache_diff_k, cache_diff_v, cache_dsa_k, cache_dsa_v, cache_dsa_kidx,
           c_prompt, c_sample, w_ada, b_ada, g_pre_mix, g_post_mix, g_pre_ffn, g_post_ffn, w_in,
           lambda_q1, lambda_k1, lambda_q2, lambda_k2, g_subln, w_out, w_gate_up, w_down,
           rel_bias_table):
    Bp = x_prompt.shape[0]
    c_all = jnp.concatenate([c_prompt, c_sample], axis=0)
    mod = _modulation(c_all, w_ada[0], b_ada[0])
    mods = [mod[:, j * D_MODEL:(j + 1) * D_MODEL][:, None, :] for j in range(6)]
    mods_p = [m[:Bp] for m in mods]
    mods_s = [m[Bp:] for m in mods]

    wm, wih, wil, wi3 = _prep_in_weights(w_in[0])
    lam4 = jnp.concatenate([lambda_q1, lambda_k1, lambda_q2, lambda_k2], axis=0)
    row = lambda g: g[0].reshape(1, -1)
    params = (row(g_pre_mix), row(g_post_mix), row(g_pre_ffn), row(g_post_ffn), wm, wih, wil, wi3,
              lam4, g_subln[0], w_out[0].astype(BF16), w_gate_up[0].astype(BF16),
              w_down[0].astype(BF16), rel_bias_table)

    yp, rp = _group(x_prompt, mods_p, None, params, bt=1, tt=512, tq=256)
    past = (cache_diff_k[0], cache_diff_v[0], cache_dsa_k[0], cache_dsa_v[0], cache_dsa_kidx[0])
    ys, rs = _group(x_sample, mods_s, past, params, bt=8, tt=64, tq=128)
    return (yp, ys) + rp + rs
```

```python
import functools
import math

import jax
import jax.numpy as jnp
from jax import lax
from jax.experimental import pallas as pl
from jax.experimental.pallas import tpu as pltpu

F32 = jnp.float32
BF16 = jnp.bfloat16
I32 = jnp.int32

D_MODEL = 1024
CHUNK = 64
N_DIFF = 4
DIFF_DH = 64
DIFF_W = N_DIFF * 2 * DIFF_DH
N_DSA = 8
N_DSA_KV = 2
DSA_GROUP = N_DSA // N_DSA_KV
DSA_DH = 64
DSA_W = N_DSA * DSA_DH
DSA_KV_W = N_DSA_KV * DSA_DH
N_IDX = 4
IDX_DH = 64
TOPK = 256
N_HEADS = N_DIFF + N_DSA
NUM_BUCKETS = 32
D_FF = 2816
EPS = 1e-6
NEG = -1e30
LOG2E = math.log2(math.e)
LAM_INIT = 0.8 - 0.6 * math.exp(-0.3 * 0)

LANE = 128
SUBLANE = 8
KEY_BLOCK = 256
V_PAD = 16
V_ROWS = 2 * DIFF_DH + V_PAD
VMEM_LIMIT = 56 * 1024 * 1024
FAR_BUCKET = 15
BUCKET_STEPS = (13, 20, 30, 46, 70, 108, 166)
KEY_BYTES = 4
N_BIAS_TILES = 4
N_STAGE = 2


def _rms(x):
    return x * lax.rsqrt(jnp.mean(x * x, axis=-1, keepdims=True) + EPS)


def _dot(a, b):
    return jnp.dot(a, b, preferred_element_type=F32)


def _dot_t(a, b):
    return lax.dot_general(a, b, (((1,), (1,)), ((), ())), preferred_element_type=F32)


def _mod_kernel(c_ref, w_ref, b_ref, o_ref):
    c = c_ref[...]
    s = c * jax.nn.sigmoid(c)
    o_ref[...] = jnp.dot(s, w_ref[...], preferred_element_type=F32,
                         precision=lax.Precision.HIGHEST) + b_ref[...]


def _modulation(c_all, w_ada, b_ada):
    n = c_all.shape[0]
    tn = 1024
    return pl.pallas_call(
        _mod_kernel,
        grid=(6 * D_MODEL // tn,),
        in_specs=[pl.BlockSpec((n, D_MODEL), lambda j: (0, 0)),
                  pl.BlockSpec((D_MODEL, tn), lambda j: (0, j)),
                  pl.BlockSpec((1, tn), lambda j: (0, j))],
        out_specs=pl.BlockSpec((n, tn), lambda j: (0, j)),
        out_shape=jax.ShapeDtypeStruct((n, 6 * D_MODEL), F32),
        name="adaln_mod",
        compiler_params=pltpu.CompilerParams(dimension_semantics=("arbitrary",),
                                             vmem_limit_bytes=VMEM_LIMIT),
    )(c_all, w_ada, b_ada.reshape(1, -1))


_C_DQ = 0
_C_DK = _C_DQ + DIFF_W
_C_DV = _C_DK + DIFF_W
_C_SQ = _C_DV + DIFF_W
_C_SK = _C_SQ + DSA_W
_C_SV = _C_SK + DSA_KV_W
_C_END = _C_SV + DSA_KV_W
IDX_W = 384


def _hi_lo(a):
    hi = a.astype(BF16)
    return hi, (a - hi.astype(F32)).astype(BF16)


def _split3(a):
    a1 = a.astype(BF16)
    r = a - a1.astype(F32)
    a2 = r.astype(BF16)
    return a1, a2, (r - a2.astype(F32)).astype(BF16)


def _value_rows(v):
    row = lax.broadcasted_iota(I32, (V_PAD, KEY_BLOCK), 0)
    extra = jnp.where(row == 0, 1.0, 0.0)
    return jnp.concatenate([v.T, extra], axis=0).astype(BF16)


def _proj_kernel(x_ref, sc_ref, sh_ref, g_ref, wm_ref, wih_ref, wil_ref, wi3_ref,
                 dk32, dv32, sk32, sv32, ik32, iw32, dq16, dk16, sq16, sk16, iq16, ik16, dvx, svx,
                 *, transpose_values):
    bt, tt, _ = x_ref.shape
    x = x_ref[...]
    h = (_rms(x) * g_ref[...]) * (1.0 + sc_ref[...]) + sh_ref[...]
    h = h.reshape(bt * tt, D_MODEL)
    hb, hl, h3 = _split3(h)
    z = _dot(hb, wm_ref[...])
    wih = wih_ref[...]
    wil = wil_ref[...]
    zi = (_dot(hb, wih) + _dot(hb, wil) + _dot(hl, wih)
          + _dot(hb, wi3_ref[...]) + _dot(h3, wih) + _dot(hl, wil))

    def put(ref, v):
        ref[...] = v.reshape(bt, tt, v.shape[-1]).astype(ref.dtype)

    zdk = z[:, _C_DK:_C_DV]
    zdv = z[:, _C_DV:_C_SQ]
    zsk = z[:, _C_SK:_C_SV]
    zsv = z[:, _C_SV:_C_END]
    zik = zi[:, 256:256 + IDX_DH]
    for n in range(N_DIFF):
        dk32[:, :, n, :] = zdk[:, n * LANE:(n + 1) * LANE].reshape(bt, tt, LANE)
        dv32[:, :, n, :] = zdv[:, n * LANE:(n + 1) * LANE].reshape(bt, tt, LANE)
    put(sk32, zsk)
    put(sv32, zsv)
    put(ik32, zik)
    put(iw32, zi[:, 256:IDX_W])
    low = lax.broadcasted_iota(I32, (bt * tt, LANE), 1) < DIFF_DH
    halves = lambda v: (jnp.where(low, v, 0.0), jnp.where(low, 0.0, v))
    dq_pairs = [halves(z[:, _C_DQ + n * LANE:_C_DQ + (n + 1) * LANE]) for n in range(N_DIFF)]
    sq_pairs = [halves(z[:, _C_SQ + j * LANE:_C_SQ + (j + 1) * LANE]) for j in range(DSA_GROUP)]
    put(dq16, jnp.concatenate([part for pair in dq_pairs for part in pair], axis=1))
    put(dk16, zdk)
    put(sq16, jnp.concatenate([pair[0] for pair in sq_pairs] + [pair[1] for pair in sq_pairs], axis=1))
    put(sk16, zsk)

    q1, q2, q3 = _split3(zi[:, 0:N_IDX * IDX_DH])
    k1, k2, k3 = _split3(zik)
    head = lambda a, n: a[:, n * IDX_DH:(n + 1) * IDX_DH]
    put(iq16, jnp.concatenate([head(p, n) for n in range(N_IDX)
                               for p in (q1, q1, q2, q1, q3, q2)], axis=1))
    put(ik16, jnp.concatenate([k1, k2, k1, k3, k1, k2], axis=1))

    if transpose_values:
        for kb in range(tt // KEY_BLOCK):
            rows = slice(kb * KEY_BLOCK, (kb + 1) * KEY_BLOCK)
            for n in range(N_DIFF):
                dvx[n, kb] = _value_rows(zdv[rows, n * LANE:(n + 1) * LANE])
            for g in range(N_DSA_KV):
                svx[g, kb] = _value_rows(zsv[rows, g * DSA_DH:(g + 1) * DSA_DH])
    else:
        put(dvx, zdv)
        put(svx, zsv)


def _projection(x, sc, sh, g, wm, wih, wil, wi3, bt, tt, transpose_values):
    B, T, _ = x.shape
    grid = (B // bt, T // tt)
    row = lambda w: pl.BlockSpec((bt, tt, w), lambda b, t: (b, t, 0))
    per_b = pl.BlockSpec((bt, 1, D_MODEL), lambda b, t: (b, 0, 0))
    full = lambda a: pl.BlockSpec(a.shape, lambda b, t: (0,) * a.ndim,
                                  pipeline_mode=pl.Buffered(1))
    outs = [(DIFF_W, F32), (DIFF_W, F32), (DSA_KV_W, F32), (DSA_KV_W, F32), (IDX_DH, F32),
            (LANE, F32), (2 * DIFF_W, BF16), (DIFF_W, BF16), (N_DSA * LANE, BF16),
            (DSA_KV_W, BF16), (N_IDX * 384, BF16), (384, BF16)]
    out_specs = [row(w) for w, _ in outs]
    out_shape = [jax.ShapeDtypeStruct((B, T, w), dt) for w, dt in outs]
    for j in range(2):
        out_specs[j] = pl.BlockSpec((bt, tt, N_DIFF, 2 * DIFF_DH), lambda b, t: (b, t, 0, 0))
        out_shape[j] = jax.ShapeDtypeStruct((B, T, N_DIFF, 2 * DIFF_DH), F32)
    if transpose_values:
        assert bt == 1 and tt % KEY_BLOCK == 0
        nb = tt // KEY_BLOCK
        out_specs += [pl.BlockSpec((None, N_DIFF, nb, V_ROWS, KEY_BLOCK), lambda b, t: (b, 0, t, 0, 0)),
                      pl.BlockSpec((None, N_DSA_KV, nb, DSA_DH + V_PAD, KEY_BLOCK),
                                   lambda b, t: (b, 0, t, 0, 0))]
        out_shape += [jax.ShapeDtypeStruct((B, N_DIFF, T // KEY_BLOCK, V_ROWS, KEY_BLOCK), BF16),
                      jax.ShapeDtypeStruct((B, N_DSA_KV, T // KEY_BLOCK, DSA_DH + V_PAD, KEY_BLOCK), BF16)]
    else:
        out_specs += [row(DIFF_W), row(DSA_KV_W)]
        out_shape += [jax.ShapeDtypeStruct((B, T, DIFF_W), BF16),
                      jax.ShapeDtypeStruct((B, T, DSA_KV_W), BF16)]
    return pl.pallas_call(
        functools.partial(_proj_kernel, transpose_values=transpose_values),
        grid=grid,
        in_specs=[row(D_MODEL), per_b, per_b, full(g), full(wm), full(wih), full(wil), full(wi3)],
        out_specs=out_specs,
        out_shape=out_shape,
        name="in_proj",
        compiler_params=pltpu.CompilerParams(dimension_semantics=("arbitrary", "arbitrary"),
                                             vmem_limit_bytes=VMEM_LIMIT),
    )(x, sc, sh, g, wm, wih, wil, wi3)


def _fold_rows(x, rows, op):
    acc = x[0:rows]
    for r in range(1, x.shape[0] // rows):
        acc = op(acc, x[r * rows:(r + 1) * rows])
    return acc


def _init_bias_tiles(tab_ref, tb_ref):
    k = lax.broadcasted_iota(I32, (LANE, LANE), 0)
    q = lax.broadcasted_iota(I32, (LANE, LANE), 1)
    buckets = []
    for t in range(N_BIAS_TILES - 1):
        rel = k - q - LANE * t
        n = jnp.abs(rel)
        large = jnp.full_like(n, 8)
        for step in BUCKET_STEPS:
            large = large + jnp.where(n >= step, 1, 0)
        buckets.append(jnp.where(rel > 0, NUM_BUCKETS // 2, 0) + jnp.where(n < 8, n, large))

    def per_head(h, carry):
        for t, bucket in enumerate(buckets):
            val = jnp.zeros((LANE, LANE), F32)
            for b in range(NUM_BUCKETS):
                val = jnp.where(bucket == b, tab_ref[b, h] * LOG2E, val)
            tb_ref[h, t] = val
        tb_ref[h, N_BIAS_TILES - 1] = jnp.full((LANE, LANE), tab_ref[FAR_BUCKET, h] * LOG2E, F32)
        return carry

    lax.fori_loop(0, N_HEADS, per_head, 0)


def _attn_kernel(tab_ref, lam_ref, gsub_ref, dq_ref, sq_ref, iq_ref, iw_ref,
                 dk_ref, dv_ref, sk_ref, sv_ref, ik_ref, o_ref,
                 tb_ref, ltri_ref, key_ref, byte_ref, s_ref, acc_ref, ot_ref, *, tq, q_offset):
    tk = KEY_BLOCK
    b = pl.program_id(0)
    i = pl.program_id(1)

    @pl.when((b == 0) & (i == 0))
    def _():
        _init_bias_tiles(tab_ref, tb_ref)
        ltri_ref[...] = jnp.where(lax.broadcasted_iota(I32, (tk, tk), 1)
                                  < lax.broadcasted_iota(I32, (tk, tk), 0), 1.0, 0.0).astype(BF16)

    qpos0 = q_offset + i * tq
    lim_last = CHUNK * ((qpos0 + tq - 1) // CHUNK + 1)
    nblk = (lim_last + tk - 1) // tk
    qq = lax.broadcasted_iota(I32, (1, tq), 1)
    lim_q = CHUNK * ((qpos0 + qq) // CHUNK + 1)
    kk = lax.broadcasted_iota(I32, (tk, tq), 0)

    def key_rows(jb):
        return pl.ds(pl.multiple_of(jb * tk, tk), tk)

    def over_blocks(body, init, unroll=4):
        def run(first, n, carry):
            for u in range(n):
                carry = body(first + u, carry)
            return carry
        carry = lax.fori_loop(0, nblk // unroll, lambda jp, c: run(unroll * jp, unroll, c), init)
        n = unroll // 2
        while n >= 1:
            first = (nblk // (2 * n)) * (2 * n)
            carry = lax.fori_loop(0, (nblk // n) % 2, lambda _, c, first=first, n=n: run(first, n, c),
                                  carry)
            n //= 2
        return carry

    def bias_tile(head, jb):
        rows = []
        for a in range(tk // LANE):
            parts = []
            for c in range(tq // LANE):
                t = jnp.clip((qpos0 + c * LANE - (jb * tk + a * LANE)) // LANE, 0, N_BIAS_TILES - 1)
                parts.append(tb_ref[head, t])
            rows.append(jnp.concatenate(parts, axis=1))
        return jnp.concatenate(rows, axis=0)

    iw = iw_ref[...]

    def score_block(jb, carry):
        kc = ik_ref[key_rows(jb), :]
        sc = jnp.zeros((tk, tq), F32)
        for h in range(N_IDX):
            d = _dot_t(kc, iq_ref[:, h * 384:(h + 1) * 384])
            sc = sc + jnp.maximum(d, 0.0) * iw[h:h + 1]
        sc = jnp.where(kk + jb * tk < lim_q, sc, NEG)
        bits = pltpu.bitcast(sc, I32)
        key = bits ^ ((bits >> 31) & 0x7FFFFFFF)
        key_ref[jb] = key
        for lvl in range(KEY_BYTES):
            byte = ((key >> (8 * lvl)) & 0xFF) if lvl < KEY_BYTES - 1 else (key >> 24) + 128
            byte_ref[lvl, jb] = byte.astype(F32).astype(BF16)
        return carry

    over_blocks(score_block, 0, unroll=4)

    one16 = jnp.ones((), BF16)
    zero16 = jnp.zeros((), BF16)

    def count_digits(lvl, cand):
        def body(jb, acc):
            hit = jnp.where(byte_ref[lvl, jb] >= cand, one16, zero16)
            return acc + _fold_rows(hit, 2 * SUBLANE, jnp.add).astype(F32)
        acc = over_blocks(body, jnp.zeros((2 * SUBLANE, tq), F32))
        return jnp.sum(acc, axis=0, keepdims=True)

    want = jnp.full((1, tq), float(TOPK), F32)
    thr = jnp.zeros((1, tq), I32)
    for lvl in reversed(range(KEY_BYTES)):
        def digit_bit(it, state, lvl=lvl, want=want):
            digit, above = state
            cand = digit + jnp.left_shift(jnp.int32(1), 7 - it).astype(F32)
            cnt = count_digits(lvl, cand.astype(BF16))
            ok = cnt >= want
            return jnp.where(ok, cand, digit), jnp.where(ok, above, cnt)

        digit, above = lax.fori_loop(0, 8, digit_bit,
                                     (jnp.zeros((1, tq), F32), jnp.zeros((1, tq), F32)))
        want = want - above
        digit_i = digit.astype(I32)
        thr = thr | ((digit_i - 128) << 24 if lvl == KEY_BYTES - 1 else digit_i << (8 * lvl))
        if lvl > 0:
            def narrow(jb, carry, lvl=lvl, digit16=digit.astype(BF16)):
                byte_ref[lvl - 1, jb] = jnp.where(byte_ref[lvl, jb] == digit16,
                                                  byte_ref[lvl - 1, jb], -one16)
                return carry
            over_blocks(narrow, 0)

    def mask_block(jb, before):
        kb = key_ref[jb]
        eq = jnp.where(kb == thr, 1.0, 0.0)
        rank = _dot(ltri_ref[...], eq.astype(BF16)) + before
        keep = jnp.where(rank < want, eq, 0.0)
        sel = jnp.where(kb > thr, 0.0, (keep - 1.0) * (-NEG))
        sel = jnp.where(kk + jb * tk < lim_q, sel, NEG)
        key_ref[jb] = pltpu.bitcast(sel, I32)
        return before + jnp.sum(_fold_rows(eq, SUBLANE, jnp.add), axis=0, keepdims=True)

    over_blocks(mask_block, jnp.zeros((1, tq), F32))

    lam = lam_ref[...]
    lam_full = (jnp.exp(jnp.sum(lam[0:1] * lam[1:2], axis=1, keepdims=True))
                - jnp.exp(jnp.sum(lam[2:3] * lam[3:4], axis=1, keepdims=True)) + LAM_INIT)

    def dsa_group(pair):
        heads = [N_STAGE * pair + n for n in range(N_STAGE)]

        def scores(jb):
            kb = sk_ref[key_rows(jb), :]
            sel = pltpu.bitcast(key_ref[jb], F32)
            return [_dot_t(kb, sq_ref[:, h * LANE:(h + 1) * LANE]) + bias_tile(N_DIFF + h, jb) + sel
                    for h in heads]

        def finish():
            for n, h in enumerate(heads):
                a = acc_ref[n]
                out = a[0:DSA_DH] / a[DSA_DH:DSA_DH + 1]
                ot_ref[DIFF_W + h * DSA_DH:DIFF_W + (h + 1) * DSA_DH, :] = out

        return scores, (lambda n, jb: sv_ref[heads[n] // DSA_GROUP, jb]), finish

    def diff_group(h):
        def scores(jb):
            kb = dk_ref[key_rows(jb), h * LANE:(h + 1) * LANE]
            bias = bias_tile(h, jb) + jnp.where(kk + jb * tk < lim_q, 0.0, NEG)
            return [_dot_t(kb, dq_ref[:, (2 * h + part) * LANE:(2 * h + part + 1) * LANE]) + bias
                    for part in range(2)]

        def finish():
            a1 = acc_ref[0]
            a2 = acc_ref[1]
            od = (a1[0:LANE] / a1[LANE:LANE + 1]
                  - lam_full * (a2[0:LANE] / a2[LANE:LANE + 1]))
            od = od * lax.rsqrt(jnp.mean(od * od, axis=0, keepdims=True) + EPS)
            ot_ref[h * LANE:(h + 1) * LANE, :] = (od * gsub_ref[...]) * (1.0 - LAM_INIT)

        return scores, (lambda n, jb: dv_ref[h, jb]), finish

    groups = ([dsa_group(p) for p in range(N_DSA // N_STAGE)]
              + [diff_group(h) for h in range(N_DIFF)])
    m_prev = None
    for k in range(len(groups) + 1):
        produce = groups[k] if k < len(groups) else None
        consume = groups[k - 1] if k > 0 else None

        def step(jb, mx, k=k, produce=produce, consume=consume, m_prev=m_prev):
            if consume is not None:
                for n in range(N_STAGE):
                    p = jnp.exp2(s_ref[(k - 1) % 2, n, jb] - m_prev[n]).astype(BF16)
                    v = consume[1](n, jb)
                    acc_ref[n, 0:v.shape[0]] += _dot(v, p)
            if produce is None:
                return mx
            out = []
            for n, s in enumerate(produce[0](jb)):
                s_ref[k % 2, n, jb] = s
                out.append(jnp.maximum(mx[n], _fold_rows(s, SUBLANE, jnp.maximum)))
            return tuple(out)

        if consume is not None:
            for n in range(N_STAGE):
                acc_ref[n] = jnp.zeros((V_ROWS, tq), F32)
        mx = over_blocks(step, tuple(jnp.full((SUBLANE, tq), -3e38, F32) for _ in range(N_STAGE)),
                         unroll=8)
        if consume is not None:
            consume[2]()
        m_prev = [jnp.max(x, axis=0, keepdims=True) for x in mx]

    o_ref[...] = ot_ref[...].T.astype(o_ref.dtype)


def _fill_key_operands(new, past, out):
    ndk, ndv, nsk, nsv, nik = new
    pk, pv, psk, psv, pik = past
    dk_s, dv_s, sk_s, sv_s, ik_s = out
    P, T, Lp = pk.shape[0], ndk.shape[0], dk_s.shape[0]

    def rows(dst, old, fresh):
        dst[0:P] = old
        dst[P:P + T] = fresh
        dst[P + T:Lp] = jnp.zeros((Lp - P - T, dst.shape[1]), BF16)

    rows(dk_s, pk[...].astype(BF16), ndk[...])
    rows(sk_s, psk[...].astype(BF16), nsk[...])
    k1, k2, k3 = _split3(pik[...])
    rows(ik_s, jnp.concatenate([k1, k2, k1, k3, k1, k2], axis=1), nik[...])

    def values(kb, dv, sv):
        for n in range(N_DIFF):
            dv_s[n, kb] = _value_rows(dv[:, n * LANE:(n + 1) * LANE])
        for g in range(N_DSA_KV):
            sv_s[g, kb] = _value_rows(sv[:, g * DSA_DH:(g + 1) * DSA_DH])

    for kb in range(P // KEY_BLOCK):
        r = slice(kb * KEY_BLOCK, (kb + 1) * KEY_BLOCK)
        values(kb, pv[r, :], psv[r, :])
    tail = lambda a: jnp.concatenate(
        [a[...].astype(F32), jnp.zeros((KEY_BLOCK - T, a.shape[1]), F32)], axis=0)
    values(P // KEY_BLOCK, tail(ndv), tail(nsv))


def _attn_cached_kernel(tab_ref, lam_ref, gsub_ref, dq_ref, sq_ref, iq_ref, iw_ref,
                        ndk, ndv, nsk, nsv, nik, pk, pv, psk, psv, pik, o_ref,
                        tb_ref, ltri_ref, key_ref, byte_ref, s_ref, acc_ref, ot_ref,
                        dk_s, dv_s, sk_s, sv_s, ik_s, *, tq, q_offset):
    _fill_key_operands((ndk, ndv, nsk, nsv, nik), (pk, pv, psk, psv, pik),
                       (dk_s, dv_s, sk_s, sv_s, ik_s))
    _attn_kernel(tab_ref, lam_ref, gsub_ref, dq_ref, sq_ref, iq_ref, iw_ref,
                 dk_s, dv_s, sk_s, sv_s, ik_s, o_ref,
                 tb_ref, ltri_ref, key_ref, byte_ref, s_ref, acc_ref, ot_ref, tq=tq, q_offset=q_offset)


def _attention(tab, lam4, gsub_t, dq, sq, iq, iw_t, keys, tq, q_offset, past=None):
    B, T, _ = dq.shape
    Lp = keys[0].shape[1] if past is None else past[0].shape[1] + KEY_BLOCK
    nblk_max = Lp // KEY_BLOCK
    qspec = lambda w: pl.BlockSpec((None, tq, w), lambda b, i: (b, i, 0))
    once = pl.Buffered(1)
    small = lambda a: pl.BlockSpec(a.shape, lambda b, i: (0,) * a.ndim)
    per_batch = lambda a, mode=None: pl.BlockSpec((None,) + a.shape[1:],
                                                  lambda b, i: (b,) + (0,) * (a.ndim - 1),
                                                  pipeline_mode=mode)
    operand_shapes = [((Lp, DIFF_W), BF16), ((N_DIFF, nblk_max, V_ROWS, KEY_BLOCK), BF16),
                      ((Lp, DSA_KV_W), BF16),
                      ((N_DSA_KV, nblk_max, DSA_DH + V_PAD, KEY_BLOCK), BF16), ((Lp, 384), BF16)]
    scratch = [pltpu.VMEM((N_HEADS, N_BIAS_TILES, LANE, LANE), F32),
               pltpu.VMEM((KEY_BLOCK, KEY_BLOCK), BF16),
               pltpu.VMEM((nblk_max, KEY_BLOCK, tq), I32),
               pltpu.VMEM((KEY_BYTES, nblk_max, KEY_BLOCK, tq), BF16),
               pltpu.VMEM((2, N_STAGE, nblk_max, KEY_BLOCK, tq), F32),
               pltpu.VMEM((N_STAGE, V_ROWS, tq), F32),
               pltpu.VMEM((D_MODEL, tq), F32)]
    if past is None:
        kern = _attn_kernel
        key_specs = [per_batch(a, once) for a in keys]
        key_args = list(keys)
    else:
        assert T == tq and past[0].shape[1] % KEY_BLOCK == 0 and keys[0].shape[1] <= KEY_BLOCK
        kern = _attn_cached_kernel
        key_specs = [per_batch(a) for a in keys] + [per_batch(a) for a in past]
        key_args = list(keys) + list(past)
        scratch += [pltpu.VMEM(shape, dt) for shape, dt in operand_shapes]
    return pl.pallas_call(
        functools.partial(kern, tq=tq, q_offset=q_offset),
        grid=(B, T // tq),
        in_specs=[pl.BlockSpec(memory_space=pltpu.SMEM), small(lam4), small(gsub_t),
                  qspec(2 * DIFF_W), qspec(N_DSA * LANE), qspec(N_IDX * 384),
                  pl.BlockSpec((None, SUBLANE, tq), lambda b, i: (b, 0, i))] + key_specs,
        out_specs=pl.BlockSpec((None, tq, D_MODEL), lambda b, i: (b, i, 0)),
        out_shape=jax.ShapeDtypeStruct((B, T, D_MODEL), BF16),
        name="attn",
        scratch_shapes=scratch,
        compiler_params=pltpu.CompilerParams(dimension_semantics=("arbitrary", "arbitrary"),
                                             vmem_limit_bytes=VMEM_LIMIT),
    )(tab, lam4, gsub_t, dq, sq, iq, iw_t, *key_args)


FF_CUTS = (0, 1536, D_FF)


def _ffn_kernel(x_ref, o_ref, ga1_ref, sc2_ref, sh2_ref, ga2_ref, gpm_ref, gpf_ref, gpo_ref,
                wo_ref, wgu_ref, wd_ref, y_ref):
    bt, tt, _ = x_ref.shape
    rows = bt * tt
    m = _dot(o_ref[...].reshape(rows, D_MODEL), wo_ref[...]).reshape(bt, tt, D_MODEL)
    x1 = x_ref[...] + ga1_ref[...] * (_rms(m) * gpm_ref[...])
    h2 = (_rms(x1) * gpf_ref[...]) * (1.0 + sc2_ref[...]) + sh2_ref[...]
    h2 = h2.reshape(rows, D_MODEL).astype(BF16)
    f = jnp.zeros((rows, D_MODEL), F32)
    for lo, hi in zip(FF_CUTS[:-1], FF_CUTS[1:]):
        g = _dot(h2, wgu_ref[:, lo:hi])
        u = _dot(h2, wgu_ref[:, D_FF + lo:D_FF + hi])
        a = (g * jax.nn.sigmoid(g) * u).astype(BF16)
        f = f + _dot(a, wd_ref[lo:hi, :])
    f = f.reshape(bt, tt, D_MODEL)
    y_ref[...] = x1 + ga2_ref[...] * (_rms(f) * gpo_ref[...])


def _out_ffn(x, o, ga1, sc2, sh2, ga2, gpm, gpf, gpo, wo, wgu, wd, bt, tt):
    B, T, _ = x.shape
    row = pl.BlockSpec((bt, tt, D_MODEL), lambda b, t: (b, t, 0))
    per_b = pl.BlockSpec((bt, 1, D_MODEL), lambda b, t: (b, 0, 0))
    full = lambda a: pl.BlockSpec(a.shape, lambda b, t: (0,) * a.ndim,
                                  pipeline_mode=pl.Buffered(1))
    return pl.pallas_call(
        _ffn_kernel,
        grid=(B // bt, T // tt),
        in_specs=[row, row, per_b, per_b, per_b, per_b, full(gpm), full(gpf), full(gpo),
                  full(wo), full(wgu), full(wd)],
        out_specs=row,
        out_shape=jax.ShapeDtypeStruct((B, T, D_MODEL), F32),
        name="out_ffn",
        compiler_params=pltpu.CompilerParams(dimension_semantics=("arbitrary", "arbitrary"),
                                             vmem_limit_bytes=VMEM_LIMIT),
    )(x, o, ga1, sc2, sh2, ga2, gpm, gpf, gpo, wo, wgu, wd)


def _split_hi_lo(a):
    hi = a.astype(BF16)
    lo = (a - hi.astype(F32)).astype(BF16)
    return hi, lo


def _prep_in_weights(w_in):
    cuts = [0, 512, 1024, 1536, 2048, 2176, 2304, 2560, 2624, 2628]
    wdq, wdk, wdv, wsq, wsk, wsv, wiq, wik, wiw = [w_in[:, a:b] for a, b in zip(cuts[:-1], cuts[1:])]
    scale = DIFF_DH ** -0.5 * LOG2E
    sq_head = lambda h: wsq[:, h * DSA_DH:(h + 1) * DSA_DH]
    sq_cols = [sq_head(j + g * DSA_GROUP) for j in range(DSA_GROUP) for g in range(N_DSA_KV)]
    wm = jnp.concatenate([wdq * scale, wdk, wdv] + [c * scale for c in sq_cols] + [wsk, wsv],
                         axis=1).astype(BF16)
    wi = jnp.concatenate([wiq, wik, wiw, jnp.zeros((D_MODEL, IDX_W - 324), F32)], axis=1)
    wih, wil = _split_hi_lo(wi)
    wi3 = (wi - wih.astype(F32) - wil.astype(F32)).astype(BF16)
    return wm, wih, wil, wi3


def _pad_axis1(a, n):
    return jnp.pad(a, ((0, 0), (0, n - a.shape[1])) + ((0, 0),) * (a.ndim - 2))


def _group(x, mods, past, params, bt, tt, tq):
    (g_pre_mix, g_post_mix, g_pre_ffn, g_post_ffn, wm, wih, wil, wi3, lam4, g_subln,
     wo, wgu, wd, tab) = params
    B, T, _ = x.shape
    sh1, sc1, ga1, sh2, sc2, ga2 = mods
    (dk32, dv32, sk32, sv32, ik32, iw32, dq16, dk16, sq16, sk16, iq16, ik16, dvx, svx) = _projection(
        x, sc1, sh1, g_pre_mix, wm, wih, wil, wi3, bt, tt, transpose_values=past is None)
    iw = iw32[..., IDX_DH:IDX_DH + N_IDX] * (1.0 / 16.0)
    if past is None:
        q_offset = 0
    else:
        q_offset = past[0].shape[1]
        past = tuple(p.reshape(B, q_offset, -1) for p in past)
    tpad = -(-T // tq) * tq
    qpad = lambda a: _pad_axis1(a, tpad)
    iw_t = _pad_axis1(jnp.transpose(qpad(iw), (0, 2, 1)), SUBLANE)
    gsub_t = jnp.broadcast_to(g_subln.reshape(-1, 1), (2 * DIFF_DH, tq))
    o = _attention(tab, lam4, gsub_t, qpad(dq16), qpad(sq16), qpad(iq16), iw_t,
                   (dk16, dvx, sk16, svx, ik16), tq, q_offset, past)[:, :T]
    y = _out_ffn(x, o, ga1, sc2, sh2, ga2, g_post_mix, g_pre_ffn, g_post_ffn, wo, wgu, wd, bt, tt)
    rows = (dk32[None], dv32[None],
            sk32.reshape(1, B, T, N_DSA_KV, DSA_DH), sv32.reshape(1, B, T, N_DSA_KV, DSA_DH),
            ik32.reshape(1, B, T, IDX_DH))
    return y, rows


def kernel(x_prompt, x_sample, cache_diff_k, cache_diff_v, cache_dsa_k, cache_dsa_v, cache_dsa_kidx,
           c_prompt, c_sample, w_ada, b_ada, g_pre_mix, g_post_mix, g_pre_ffn, g_post_ffn, w_in,
           lambda_q1, lambda_k1, lambda_q2, lambda_k2, g_subln, w_out, w_gate_up, w_down,
           rel_bias_table):
    Bp = x_prompt.shape[0]
    c_all = jnp.concatenate([c_prompt, c_sample], axis=0)
    mod = _modulation(c_all, w_ada[0], b_ada[0])
    mods = [mod[:, j * D_MODEL:(j + 1) * D_MODEL][:, None, :] for j in range(6)]
    mods_p = [m[:Bp] for m in mods]
    mods_s = [m[Bp:] for m in mods]

    wm, wih, wil, wi3 = _prep_in_weights(w_in[0])
    lam4 = jnp.concatenate([lambda_q1, lambda_k1, lambda_q2, lambda_k2], axis=0)
    row = lambda g: g[0].reshape(1, -1)
    params = (row(g_pre_mix), row(g_post_mix), row(g_pre_ffn), row(g_post_ffn), wm, wih, wil, wi3,
              lam4, g_subln[0], w_out[0].astype(BF16), w_gate_up[0].astype(BF16),
              w_down[0].astype(BF16), rel_bias_table)

    yp, rp = _group(x_prompt, mods_p, None, params, bt=1, tt=512, tq=256)
    past = (cache_diff_k[0], cache_diff_v[0], cache_dsa_k[0], cache_dsa_v[0], cache_dsa_kidx[0])
    ys, rs = _group(x_sample, mods_s, past, params, bt=8, tt=64, tq=128)
    return (yp, ys) + rp + rs
```

```python
import functools
import math

import jax
import jax.numpy as jnp
from jax import lax
from jax.experimental import pallas as pl
from jax.experimental.pallas import tpu as pltpu

F32 = jnp.float32
BF16 = jnp.bfloat16
I32 = jnp.int32

D_MODEL = 1024
CHUNK = 64
N_DIFF = 4
DIFF_DH = 64
DIFF_W = N_DIFF * 2 * DIFF_DH
N_DSA = 8
N_DSA_KV = 2
DSA_GROUP = N_DSA // N_DSA_KV
DSA_DH = 64
DSA_W = N_DSA * DSA_DH
DSA_KV_W = N_DSA_KV * DSA_DH
N_IDX = 4
IDX_DH = 64
TOPK = 256
N_HEADS = N_DIFF + N_DSA
NUM_BUCKETS = 32
D_FF = 2816
EPS = 1e-6
NEG = -1e30
LOG2E = math.log2(math.e)
LAM_INIT = 0.8 - 0.6 * math.exp(-0.3 * 0)

LANE = 128
SUBLANE = 8
KEY_BLOCK = 256
V_PAD = 16
V_ROWS = 2 * DIFF_DH + V_PAD
VMEM_LIMIT = 56 * 1024 * 1024
FAR_BUCKET = 15
BUCKET_STEPS = (13, 20, 30, 46, 70, 108, 166)
KEY_BYTES = 4
N_BIAS_TILES = 4
N_STAGE = 2


def _rms(x):
    return x * lax.rsqrt(jnp.mean(x * x, axis=-1, keepdims=True) + EPS)


def _dot(a, b):
    return jnp.dot(a, b, preferred_element_type=F32)


def _dot_t(a, b):
    return lax.dot_general(a, b, (((1,), (1,)), ((), ())), preferred_element_type=F32)


def _mod_kernel(c_ref, w_ref, b_ref, o_ref):
    c = c_ref[...]
    s = c * jax.nn.sigmoid(c)
    o_ref[...] = jnp.dot(s, w_ref[...], preferred_element_type=F32,
                         precision=lax.Precision.HIGHEST) + b_ref[...]


def _modulation(c_all, w_ada, b_ada):
    n = c_all.shape[0]
    tn = 1024
    return pl.pallas_call(
        _mod_kernel,
        grid=(6 * D_MODEL // tn,),
        in_specs=[pl.BlockSpec((n, D_MODEL), lambda j: (0, 0)),
                  pl.BlockSpec((D_MODEL, tn), lambda j: (0, j)),
                  pl.BlockSpec((1, tn), lambda j: (0, j))],
        out_specs=pl.BlockSpec((n, tn), lambda j: (0, j)),
        out_shape=jax.ShapeDtypeStruct((n, 6 * D_MODEL), F32),
        name="adaln_mod",
        compiler_params=pltpu.CompilerParams(dimension_semantics=("arbitrary",),
                                             vmem_limit_bytes=VMEM_LIMIT),
    )(c_all, w_ada, b_ada.reshape(1, -1))


_C_DQ = 0
_C_DK = _C_DQ + DIFF_W
_C_DV = _C_DK + DIFF_W
_C_SQ = _C_DV + DIFF_W
_C_SK = _C_SQ + DSA_W
_C_SV = _C_SK + DSA_KV_W
_C_END = _C_SV + DSA_KV_W
IDX_W = 384


def _hi_lo(a):
    hi = a.astype(BF16)
    return hi, (a - hi.astype(F32)).astype(BF16)


def _split3(a):
    a1 = a.astype(BF16)
    r = a - a1.astype(F32)
    a2 = r.astype(BF16)
    return a1, a2, (r - a2.astype(F32)).astype(BF16)


def _value_rows(v):
    row = lax.broadcasted_iota(I32, (V_PAD, KEY_BLOCK), 0)
    extra = jnp.where(row == 0, 1.0, 0.0)
    return jnp.concatenate([v.T, extra], axis=0).astype(BF16)


def _proj_kernel(x_ref, sc_ref, sh_ref, g_ref, wm_ref, wih_ref, wil_ref, wi3_ref,
                 dk32, dv32, sk32, sv32, ik32, iw32, dq16, dk16, sq16, sk16, iq16, ik16, dvx, svx,
                 *, transpose_values):
    bt, tt, _ = x_ref.shape
    x = x_ref[...]
    h = (_rms(x) * g_ref[...]) * (1.0 + sc_ref[...]) + sh_ref[...]
    h = h.reshape(bt * tt, D_MODEL)
    hb, hl, h3 = _split3(h)
    z = _dot(hb, wm_ref[...])
    wih = wih_ref[...]
    wil = wil_ref[...]
    zi = (_dot(hb, wih) + _dot(hb, wil) + _dot(hl, wih)
          + _dot(hb, wi3_ref[...]) + _dot(h3, wih) + _dot(hl, wil))

    def put(ref, v):
        ref[...] = v.reshape(bt, tt, v.shape[-1]).astype(ref.dtype)

    zdk = z[:, _C_DK:_C_DV]
    zdv = z[:, _C_DV:_C_SQ]
    zsk = z[:, _C_SK:_C_SV]
    zsv = z[:, _C_SV:_C_END]
    zik = zi[:, 256:256 + IDX_DH]
    for n in range(N_DIFF):
        dk32[:, :, n, :] = zdk[:, n * LANE:(n + 1) * LANE].reshape(bt, tt, LANE)
        dv32[:, :, n, :] = zdv[:, n * LANE:(n + 1) * LANE].reshape(bt, tt, LANE)
    put(sk32, zsk)
    put(sv32, zsv)
    put(ik32, zik)
    put(iw32, zi[:, 256:IDX_W])
    low = lax.broadcasted_iota(I32, (bt * tt, LANE), 1) < DIFF_DH
    halves = lambda v: (jnp.where(low, v, 0.0), jnp.where(low, 0.0, v))
    dq_pairs = [halves(z[:, _C_DQ + n * LANE:_C_DQ + (n + 1) * LANE]) for n in range(N_DIFF)]
    sq_pairs = [halves(z[:, _C_SQ + j * LANE:_C_SQ + (j + 1) * LANE]) for j in range(DSA_GROUP)]
    put(dq16, jnp.concatenate([part for pair in dq_pairs for part in pair], axis=1))
    put(dk16, zdk)
    put(sq16, jnp.concatenate([pair[0] for pair in sq_pairs] + [pair[1] for pair in sq_pairs], axis=1))
    put(sk16, zsk)

    q1, q2, q3 = _split3(zi[:, 0:N_IDX * IDX_DH])
    k1, k2, k3 = _split3(zik)
    head = lambda a, n: a[:, n * IDX_DH:(n + 1) * IDX_DH]
    put(iq16, jnp.concatenate([head(p, n) for n in range(N_IDX)
                               for p in (q1, q1, q2, q1, q3, q2)], axis=1))
    put(ik16, jnp.concatenate([k1, k2, k1, k3, k1, k2], axis=1))

    if transpose_values:
        for kb in range(tt // KEY_BLOCK):
            rows = slice(kb * KEY_BLOCK, (kb + 1) * KEY_BLOCK)
            for n in range(N_DIFF):
                dvx[n, kb] = _value_rows(zdv[rows, n * LANE:(n + 1) * LANE])
            for g in range(N_DSA_KV):
                svx[g, kb] = _value_rows(zsv[rows, g * DSA_DH:(g + 1) * DSA_DH])
    else:
        put(dvx, zdv)
        put(svx, zsv)


def _projection(x, sc, sh, g, wm, wih, wil, wi3, bt, tt, transpose_values):
    B, T, _ = x.shape
    grid = (B // bt, T // tt)
    row = lambda w: pl.BlockSpec((bt, tt, w), lambda b, t: (b, t, 0))
    per_b = pl.BlockSpec((bt, 1, D_MODEL), lambda b, t: (b, 0, 0))
    full = lambda a: pl.BlockSpec(a.shape, lambda b, t: (0,) * a.ndim,
                                  pipeline_mode=pl.Buffered(1))
    outs = [(DIFF_W, F32), (DIFF_W, F32), (DSA_KV_W, F32), (DSA_KV_W, F32), (IDX_DH, F32),
            (LANE, F32), (2 * DIFF_W, BF16), (DIFF_W, BF16), (N_DSA * LANE, BF16),
            (DSA_KV_W, BF16), (N_IDX * 384, BF16), (384, BF16)]
    out_specs = [row(w) for w, _ in outs]
    out_shape = [jax.ShapeDtypeStruct((B, T, w), dt) for w, dt in outs]
    for j in range(2):
        out_specs[j] = pl.BlockSpec((bt, tt, N_DIFF, 2 * DIFF_DH), lambda b, t: (b, t, 0, 0))
        out_shape[j] = jax.ShapeDtypeStruct((B, T, N_DIFF, 2 * DIFF_DH), F32)
    if transpose_values:
        assert bt == 1 and tt % KEY_BLOCK == 0
        nb = tt // KEY_BLOCK
        out_specs += [pl.BlockSpec((None, N_DIFF, nb, V_ROWS, KEY_BLOCK), lambda b, t: (b, 0, t, 0, 0)),
                      pl.BlockSpec((None, N_DSA_KV, nb, DSA_DH + V_PAD, KEY_BLOCK),
                                   lambda b, t: (b, 0, t, 0, 0))]
        out_shape += [jax.ShapeDtypeStruct((B, N_DIFF, T // KEY_BLOCK, V_ROWS, KEY_BLOCK), BF16),
                      jax.ShapeDtypeStruct((B, N_DSA_KV, T // KEY_BLOCK, DSA_DH + V_PAD, KEY_BLOCK), BF16)]
    else:
        out_specs += [row(DIFF_W), row(DSA_KV_W)]
        out_shape += [jax.ShapeDtypeStruct((B, T, DIFF_W), BF16),
                      jax.ShapeDtypeStruct((B, T, DSA_KV_W), BF16)]
    return pl.pallas_call(
        functools.partial(_proj_kernel, transpose_values=transpose_values),
        grid=grid,
        in_specs=[row(D_MODEL), per_b, per_b, full(g), full(wm), full(wih), full(wil), full(wi3)],
        out_specs=out_specs,
        out_shape=out_shape,
        name="in_proj",
        compiler_params=pltpu.CompilerParams(dimension_semantics=("arbitrary", "arbitrary"),
                                             vmem_limit_bytes=VMEM_LIMIT),
    )(x, sc, sh, g, wm, wih, wil, wi3)


def _fold_rows(x, rows, op):
    acc = x[0:rows]
    for r in range(1, x.shape[0] // rows):
        acc = op(acc, x[r * rows:(r + 1) * rows])
    return acc


def _init_bias_tiles(tab_ref, tb_ref):
    k = lax.broadcasted_iota(I32, (LANE, LANE), 0)
    q = lax.broadcasted_iota(I32, (LANE, LANE), 1)
    buckets = []
    for t in range(N_BIAS_TILES - 1):
        rel = k - q - LANE * t
        n = jnp.abs(rel)
        large = jnp.full_like(n, 8)
        for step in BUCKET_STEPS:
            large = large + jnp.where(n >= step, 1, 0)
        buckets.append(jnp.where(rel > 0, NUM_BUCKETS // 2, 0) + jnp.where(n < 8, n, large))

    def per_head(h, carry):
        for t, bucket in enumerate(buckets):
            val = jnp.zeros((LANE, LANE), F32)
            for b in range(NUM_BUCKETS):
                val = jnp.where(bucket == b, tab_ref[b, h] * LOG2E, val)
            tb_ref[h, t] = val
        tb_ref[h, N_BIAS_TILES - 1] = jnp.full((LANE, LANE), tab_ref[FAR_BUCKET, h] * LOG2E, F32)
        return carry

    lax.fori_loop(0, N_HEADS, per_head, 0)


def _attn_kernel(tab_ref, lam_ref, gsub_ref, dq_ref, sq_ref, iq_ref, iw_ref,
                 dk_ref, dv_ref, sk_ref, sv_ref, ik_ref, o_ref,
                 tb_ref, ltri_ref, key_ref, byte_ref, s_ref, acc_ref, ot_ref, *, tq, q_offset):
    tk = KEY_BLOCK
    b = pl.program_id(0)
    i = pl.program_id(1)

    @pl.when((b == 0) & (i == 0))
    def _():
        _init_bias_tiles(tab_ref, tb_ref)
        ltri_ref[...] = jnp.where(lax.broadcasted_iota(I32, (tk, tk), 1)
                                  < lax.broadcasted_iota(I32, (tk, tk), 0), 1.0, 0.0).astype(BF16)

    qpos0 = q_offset + i * tq
    lim_last = CHUNK * ((qpos0 + tq - 1) // CHUNK + 1)
    nblk = (lim_last + tk - 1) // tk
    qq = lax.broadcasted_iota(I32, (1, tq), 1)
    lim_q = CHUNK * ((qpos0 + qq) // CHUNK + 1)
    kk = lax.broadcasted_iota(I32, (tk, tq), 0)

    def key_rows(jb):
        return pl.ds(pl.multiple_of(jb * tk, tk), tk)

    def over_blocks(body, init, unroll=4):
        def run(first, n, carry):
            for u in range(n):
                carry = body(first + u, carry)
            return carry
        carry = lax.fori_loop(0, nblk // unroll, lambda jp, c: run(unroll * jp, unroll, c), init)
        n = unroll // 2
        while n >= 1:
            first = (nblk // (2 * n)) * (2 * n)
            carry = lax.fori_loop(0, (nblk // n) % 2, lambda _, c, first=first, n=n: run(first, n, c),
                                  carry)
            n //= 2
        return carry

    def bias_tile(head, jb):
        rows = []
        for a in range(tk // LANE):
            parts = []
            for c in range(tq // LANE):
                t = jnp.clip((qpos0 + c * LANE - (jb * tk + a * LANE)) // LANE, 0, N_BIAS_TILES - 1)
                parts.append(tb_ref[head, t])
            rows.append(jnp.concatenate(parts, axis=1))
        return jnp.concatenate(rows, axis=0)

    iw = iw_ref[...]

    def score_block(jb, carry):
        kc = ik_ref[key_rows(jb), :]
        sc = jnp.zeros((tk, tq), F32)
        for h in range(N_IDX):
            d = _dot_t(kc, iq_ref[:, h * 384:(h + 1) * 384])
            sc = sc + jnp.maximum(d, 0.0) * iw[h:h + 1]
        sc = jnp.where(kk + jb * tk < lim_q, sc, NEG)
        bits = pltpu.bitcast(sc, I32)
        key = bits ^ ((bits >> 31) & 0x7FFFFFFF)
        key_ref[jb] = key
        for lvl in range(KEY_BYTES):
            byte = ((key >> (8 * lvl)) & 0xFF) if lvl < KEY_BYTES - 1 else (key >> 24) + 128
            byte_ref[lvl, jb] = byte.astype(F32).astype(BF16)
        return carry

    over_blocks(score_block, 0, unroll=4)

    one16 = jnp.ones((), BF16)
    zero16 = jnp.zeros((), BF16)

    def count_digits(lvl, cand):
        def body(jb, acc):
            hit = jnp.where(byte_ref[lvl, jb] >= cand, one16, zero16)
            return acc + _fold_rows(hit, 2 * SUBLANE, jnp.add).astype(F32)
        acc = over_blocks(body, jnp.zeros((2 * SUBLANE, tq), F32))
        return jnp.sum(acc, axis=0, keepdims=True)

    want = jnp.full((1, tq), float(TOPK), F32)
    thr = jnp.zeros((1, tq), I32)
    for lvl in reversed(range(KEY_BYTES)):
        def digit_bit(it, state, lvl=lvl, want=want):
            digit, above = state
            cand = digit + jnp.left_shift(jnp.int32(1), 7 - it).astype(F32)
            cnt = count_digits(lvl, cand.astype(BF16))
            ok = cnt >= want
            return jnp.where(ok, cand, digit), jnp.where(ok, above, cnt)

        digit, above = lax.fori_loop(0, 8, digit_bit,
                                     (jnp.zeros((1, tq), F32), jnp.zeros((1, tq), F32)))
        want = want - above
        digit_i = digit.astype(I32)
        thr = thr | ((digit_i - 128) << 24 if lvl == KEY_BYTES - 1 else digit_i << (8 * lvl))
        if lvl > 0:
            def narrow(jb, carry, lvl=lvl, digit16=digit.astype(BF16)):
                byte_ref[lvl - 1, jb] = jnp.where(byte_ref[lvl, jb] == digit16,
                                                  byte_ref[lvl - 1, jb], -one16)
                return carry
            over_blocks(narrow, 0)

    def mask_block(jb, before):
        kb = key_ref[jb]
        eq = jnp.where(kb == thr, 1.0, 0.0)
        rank = _dot(ltri_ref[...], eq.astype(BF16)) + before
        keep = jnp.where(rank < want, eq, 0.0)
        sel = jnp.where(kb > thr, 0.0, (keep - 1.0) * (-NEG))
        sel = jnp.where(kk + jb * tk < lim_q, sel, NEG)
        key_ref[jb] = pltpu.bitcast(sel, I32)
        return before + jnp.sum(_fold_rows(eq, SUBLANE, jnp.add), axis=0, keepdims=True)

    over_blocks(mask_block, jnp.zeros((1, tq), F32))

    lam = lam_ref[...]
    lam_full = (jnp.exp(jnp.sum(lam[0:1] * lam[1:2], axis=1, keepdims=True))
                - jnp.exp(jnp.sum(lam[2:3] * lam[3:4], axis=1, keepdims=True)) + LAM_INIT)

    def dsa_group(pair):
        heads = [N_STAGE * pair + n for n in range(N_STAGE)]

        def scores(jb):
            kb = sk_ref[key_rows(jb), :]
            sel = pltpu.bitcast(key_ref[jb], F32)
            return [_dot_t(kb, sq_ref[:, h * LANE:(h + 1) * LANE]) + bias_tile(N_DIFF + h, jb) + sel
                    for h in heads]

        def finish():
            for n, h in enumerate(heads):
                a = acc_ref[n]
                out = a[0:DSA_DH] / a[DSA_DH:DSA_DH + 1]
                ot_ref[DIFF_W + h * DSA_DH:DIFF_W + (h + 1) * DSA_DH, :] = out

        return scores, (lambda n, jb: sv_ref[heads[n] // DSA_GROUP, jb]), finish

    def diff_group(h):
        def scores(jb):
            kb = dk_ref[key_rows(jb), h * LANE:(h + 1) * LANE]
            bias = bias_tile(h, jb) + jnp.where(kk + jb * tk < lim_q, 0.0, NEG)
            return [_dot_t(kb, dq_ref[:, (2 * h + part) * LANE:(2 * h + part + 1) * LANE]) + bias
                    for part in range(2)]

        def finish():
            a1 = acc_ref[0]
            a2 = acc_ref[1]
            od = (a1[0:LANE] / a1[LANE:LANE + 1]
                  - lam_full * (a2[0:LANE] / a2[LANE:LANE + 1]))
            od = od * lax.rsqrt(jnp.mean(od * od, axis=0, keepdims=True) + EPS)
            ot_ref[h * LANE:(h + 1) * LANE, :] = (od * gsub_ref[...]) * (1.0 - LAM_INIT)

        return scores, (lambda n, jb: dv_ref[h, jb]), finish

    groups = ([dsa_group(p) for p in range(N_DSA // N_STAGE)]
              + [diff_group(h) for h in range(N_DIFF)])
    m_prev = None
    for k in range(len(groups) + 1):
        produce = groups[k] if k < len(groups) else None
        consume = groups[k - 1] if k > 0 else None

        def step(jb, mx, k=k, produce=produce, consume=consume, m_prev=m_prev):
            if consume is not None:
                for n in range(N_STAGE):
                    p = jnp.exp2(s_ref[(k - 1) % 2, n, jb] - m_prev[n]).astype(BF16)
                    v = consume[1](n, jb)
                    acc_ref[n, 0:v.shape[0]] += _dot(v, p)
            if produce is None:
                return mx
            out = []
            for n, s in enumerate(produce[0](jb)):
                s_ref[k % 2, n, jb] = s
                out.append(jnp.maximum(mx[n], _fold_rows(s, SUBLANE, jnp.maximum)))
            return tuple(out)

        if consume is not None:
            for n in range(N_STAGE):
                acc_ref[n] = jnp.zeros((V_ROWS, tq), F32)
        mx = over_blocks(step, tuple(jnp.full((SUBLANE, tq), -3e38, F32) for _ in range(N_STAGE)),
                         unroll=8)
        if consume is not None:
            consume[2]()
        m_prev = [jnp.max(x, axis=0, keepdims=True) for x in mx]

    o_ref[...] = ot_ref[...].T.astype(o_ref.dtype)


def _fill_key_operands(new, past, out):
    ndk, ndv, nsk, nsv, nik = new
    pk, pv, psk, psv, pik = past
    dk_s, dv_s, sk_s, sv_s, ik_s = out
    P, T, Lp = pik.shape[0], ndk.shape[0], dk_s.shape[0]

    def rows(dst, fresh):
        dst[P:P + T] = fresh
        dst[P + T:Lp] = jnp.zeros((Lp - P - T, dst.shape[1]), BF16)

    for n in range(N_DIFF):
        dk_s[0:P, n * LANE:(n + 1) * LANE] = pk[n][...].astype(BF16)
    rows(dk_s, ndk[...])
    sk_s[0:P] = psk[...].astype(BF16)
    rows(sk_s, nsk[...])
    k1, k2, k3 = _split3(pik[...])
    ik_s[0:P] = jnp.concatenate([k1, k2, k1, k3, k1, k2], axis=1)
    rows(ik_s, nik[...])

    for kb in range(P // KEY_BLOCK):
        r = slice(kb * KEY_BLOCK, (kb + 1) * KEY_BLOCK)
        for n in range(N_DIFF):
            dv_s[n, kb] = _value_rows(pv[n][r, :])
        for g in range(N_DSA_KV):
            sv_s[g, kb] = _value_rows(psv[r, g * DSA_DH:(g + 1) * DSA_DH])
    tail = lambda a: jnp.concatenate(
        [a[...].astype(F32), jnp.zeros((KEY_BLOCK - T, a.shape[1]), F32)], axis=0)
    tail_dv, tail_sv = tail(ndv), tail(nsv)
    for n in range(N_DIFF):
        dv_s[n, P // KEY_BLOCK] = _value_rows(tail_dv[:, n * LANE:(n + 1) * LANE])
    for g in range(N_DSA_KV):
        sv_s[g, P // KEY_BLOCK] = _value_rows(tail_sv[:, g * DSA_DH:(g + 1) * DSA_DH])


def _attn_cached_kernel(tab_ref, lam_ref, gsub_ref, dq_ref, sq_ref, iq_ref, iw_ref,
                        ndk, ndv, nsk, nsv, nik, pk_hbm, pv_hbm, psk, psv, pik, o_ref,
                        tb_ref, ltri_ref, key_ref, byte_ref, s_ref, acc_ref, ot_ref,
                        dk_s, dv_s, sk_s, sv_s, ik_s, kv_buf, kv_sem, *, tq, q_offset):
    b = pl.program_id(0)
    slot = b % 2

    def cache_copies(batch, into):
        return [pltpu.make_async_copy(src.at[batch, :, n, :], kv_buf.at[into, t, n],
                                      kv_sem.at[into, t, n])
                for t, src in enumerate((pk_hbm, pv_hbm)) for n in range(N_DIFF)]

    @pl.when(b == 0)
    def _():
        for copy in cache_copies(0, 0):
            copy.start()

    for copy in cache_copies(b, slot):
        copy.wait()

    @pl.when(b + 1 < pl.num_programs(0))
    def _():
        for copy in cache_copies(b + 1, 1 - slot):
            copy.start()

    pk = [kv_buf.at[slot, 0, n] for n in range(N_DIFF)]
    pv = [kv_buf.at[slot, 1, n] for n in range(N_DIFF)]
    _fill_key_operands((ndk, ndv, nsk, nsv, nik), (pk, pv, psk, psv, pik),
                       (dk_s, dv_s, sk_s, sv_s, ik_s))
    _attn_kernel(tab_ref, lam_ref, gsub_ref, dq_ref, sq_ref, iq_ref, iw_ref,
                 dk_s, dv_s, sk_s, sv_s, ik_s, o_ref,
                 tb_ref, ltri_ref, key_ref, byte_ref, s_ref, acc_ref, ot_ref, tq=tq, q_offset=q_offset)


def _attention(tab, lam4, gsub_t, dq, sq, iq, iw_t, keys, tq, q_offset, past=None):
    B, T, _ = dq.shape
    Lp = keys[0].shape[1] if past is None else past[0].shape[1] + KEY_BLOCK
    nblk_max = Lp // KEY_BLOCK
    qspec = lambda w: pl.BlockSpec((None, tq, w), lambda b, i: (b, i, 0))
    once = pl.Buffered(1)
    small = lambda a: pl.BlockSpec(a.shape, lambda b, i: (0,) * a.ndim)
    per_batch = lambda a, mode=None: pl.BlockSpec((None,) + a.shape[1:],
                                                  lambda b, i: (b,) + (0,) * (a.ndim - 1),
                                                  pipeline_mode=mode)
    operand_shapes = [((Lp, DIFF_W), BF16), ((N_DIFF, nblk_max, V_ROWS, KEY_BLOCK), BF16),
                      ((Lp, DSA_KV_W), BF16),
                      ((N_DSA_KV, nblk_max, DSA_DH + V_PAD, KEY_BLOCK), BF16), ((Lp, 384), BF16)]
    scratch = [pltpu.VMEM((N_HEADS, N_BIAS_TILES, LANE, LANE), F32),
               pltpu.VMEM((KEY_BLOCK, KEY_BLOCK), BF16),
               pltpu.VMEM((nblk_max, KEY_BLOCK, tq), I32),
               pltpu.VMEM((KEY_BYTES, nblk_max, KEY_BLOCK, tq), BF16),
               pltpu.VMEM((2, N_STAGE, nblk_max, KEY_BLOCK, tq), F32),
               pltpu.VMEM((N_STAGE, V_ROWS, tq), F32),
               pltpu.VMEM((D_MODEL, tq), F32)]
    if past is None:
        kern = _attn_kernel
        key_specs = [per_batch(a, once) for a in keys]
        key_args = list(keys)
    else:
        assert T == tq and past[0].shape[1] % KEY_BLOCK == 0 and keys[0].shape[1] <= KEY_BLOCK
        kern = _attn_cached_kernel
        pk, pv, psk, psv, pik = past
        P = pk.shape[1]
        flat = lambda a: a.reshape(B, P, -1)
        hbm = pl.BlockSpec(memory_space=pl.ANY)
        key_specs = [per_batch(a) for a in keys] + [hbm, hbm] + [
            per_batch(flat(a)) for a in (psk, psv, pik)]
        key_args = list(keys) + [pk, pv, flat(psk), flat(psv), flat(pik)]
        scratch += [pltpu.VMEM(shape, dt) for shape, dt in operand_shapes]
        scratch += [pltpu.VMEM((2, 2, N_DIFF, P, 2 * DIFF_DH), F32),
                    pltpu.SemaphoreType.DMA((2, 2, N_DIFF))]
    return pl.pallas_call(
        functools.partial(kern, tq=tq, q_offset=q_offset),
        grid=(B, T // tq),
        in_specs=[pl.BlockSpec(memory_space=pltpu.SMEM), small(lam4), small(gsub_t),
                  qspec(2 * DIFF_W), qspec(N_DSA * LANE), qspec(N_IDX * 384),
                  pl.BlockSpec((None, SUBLANE, tq), lambda b, i: (b, 0, i))] + key_specs,
        out_specs=pl.BlockSpec((None, tq, D_MODEL), lambda b, i: (b, i, 0)),
        out_shape=jax.ShapeDtypeStruct((B, T, D_MODEL), BF16),
        name="attn",
        scratch_shapes=scratch,
        compiler_params=pltpu.CompilerParams(dimension_semantics=("arbitrary", "arbitrary"),
                                             vmem_limit_bytes=VMEM_LIMIT),
    )(tab, lam4, gsub_t, dq, sq, iq, iw_t, *key_args)


FF_CUTS = (0, 1536, D_FF)


def _ffn_kernel(x_ref, o_ref, ga1_ref, sc2_ref, sh2_ref, ga2_ref, gpm_ref, gpf_ref, gpo_ref,
                wo_ref, wgu_ref, wd_ref, y_ref):
    bt, tt, _ = x_ref.shape
    rows = bt * tt
    m = _dot(o_ref[...].reshape(rows, D_MODEL), wo_ref[...]).reshape(bt, tt, D_MODEL)
    x1 = x_ref[...] + ga1_ref[...] * (_rms(m) * gpm_ref[...])
    h2 = (_rms(x1) * gpf_ref[...]) * (1.0 + sc2_ref[...]) + sh2_ref[...]
    h2 = h2.reshape(rows, D_MODEL).astype(BF16)
    f = jnp.zeros((rows, D_MODEL), F32)
    for lo, hi in zip(FF_CUTS[:-1], FF_CUTS[1:]):
        g = _dot(h2, wgu_ref[:, lo:hi])
        u = _dot(h2, wgu_ref[:, D_FF + lo:D_FF + hi])
        a = (g * jax.nn.sigmoid(g) * u).astype(BF16)
        f = f + _dot(a, wd_ref[lo:hi, :])
    f = f.reshape(bt, tt, D_MODEL)
    y_ref[...] = x1 + ga2_ref[...] * (_rms(f) * gpo_ref[...])


def _out_ffn(x, o, ga1, sc2, sh2, ga2, gpm, gpf, gpo, wo, wgu, wd, bt, tt):
    B, T, _ = x.shape
    row = pl.BlockSpec((bt, tt, D_MODEL), lambda b, t: (b, t, 0))
    per_b = pl.BlockSpec((bt, 1, D_MODEL), lambda b, t: (b, 0, 0))
    full = lambda a: pl.BlockSpec(a.shape, lambda b, t: (0,) * a.ndim,
                                  pipeline_mode=pl.Buffered(1))
    return pl.pallas_call(
        _ffn_kernel,
        grid=(B // bt, T // tt),
        in_specs=[row, row, per_b, per_b, per_b, per_b, full(gpm), full(gpf), full(gpo),
                  full(wo), full(wgu), full(wd)],
        out_specs=row,
        out_shape=jax.ShapeDtypeStruct((B, T, D_MODEL), F32),
        name="out_ffn",
        compiler_params=pltpu.CompilerParams(dimension_semantics=("arbitrary", "arbitrary"),
                                             vmem_limit_bytes=VMEM_LIMIT),
    )(x, o, ga1, sc2, sh2, ga2, gpm, gpf, gpo, wo, wgu, wd)


def _split_hi_lo(a):
    hi = a.astype(BF16)
    lo = (a - hi.astype(F32)).astype(BF16)
    return hi, lo


def _prep_in_weights(w_in):
    cuts = [0, 512, 1024, 1536, 2048, 2176, 2304, 2560, 2624, 2628]
    wdq, wdk, wdv, wsq, wsk, wsv, wiq, wik, wiw = [w_in[:, a:b] for a, b in zip(cuts[:-1], cuts[1:])]
    scale = DIFF_DH ** -0.5 * LOG2E
    sq_head = lambda h: wsq[:, h * DSA_DH:(h + 1) * DSA_DH]
    sq_cols = [sq_head(j + g * DSA_GROUP) for j in range(DSA_GROUP) for g in range(N_DSA_KV)]
    wm = jnp.concatenate([wdq * scale, wdk, wdv] + [c * scale for c in sq_cols] + [wsk, wsv],
                         axis=1).astype(BF16)
    wi = jnp.concatenate([wiq, wik, wiw, jnp.zeros((D_MODEL, IDX_W - 324), F32)], axis=1)
    wih, wil = _split_hi_lo(wi)
    wi3 = (wi - wih.astype(F32) - wil.astype(F32)).astype(BF16)
    return wm, wih, wil, wi3


def _pad_axis1(a, n):
    return jnp.pad(a, ((0, 0), (0, n - a.shape[1])) + ((0, 0),) * (a.ndim - 2))


def _group(x, mods, past, params, bt, tt, tq):
    (g_pre_mix, g_post_mix, g_pre_ffn, g_post_ffn, wm, wih, wil, wi3, lam4, g_subln,
     wo, wgu, wd, tab) = params
    B, T, _ = x.shape
    sh1, sc1, ga1, sh2, sc2, ga2 = mods
    (dk32, dv32, sk32, sv32, ik32, iw32, dq16, dk16, sq16, sk16, iq16, ik16, dvx, svx) = _projection(
        x, sc1, sh1, g_pre_mix, wm, wih, wil, wi3, bt, tt, transpose_values=past is None)
    iw = iw32[..., IDX_DH:IDX_DH + N_IDX] * (1.0 / 16.0)
    if past is None:
        q_offset = 0
    else:
        q_offset = past[0].shape[1]
    tpad = -(-T // tq) * tq
    qpad = lambda a: _pad_axis1(a, tpad)
    iw_t = _pad_axis1(jnp.transpose(qpad(iw), (0, 2, 1)), SUBLANE)
    gsub_t = jnp.broadcast_to(g_subln.reshape(-1, 1), (2 * DIFF_DH, tq))
    o = _attention(tab, lam4, gsub_t, qpad(dq16), qpad(sq16), qpad(iq16), iw_t,
                   (dk16, dvx, sk16, svx, ik16), tq, q_offset, past)[:, :T]
    y = _out_ffn(x, o, ga1, sc2, sh2, ga2, g_post_mix, g_pre_ffn, g_post_ffn, wo, wgu, wd, bt, tt)
    rows = (dk32[None], dv32[None],
            sk32.reshape(1, B, T, N_DSA_KV, DSA_DH), sv32.reshape(1, B, T, N_DSA_KV, DSA_DH),
            ik32.reshape(1, B, T, IDX_DH))
    return y, rows


def kernel(x_prompt, x_sample, cache_diff_k, cache_diff_v, cache_dsa_k, cache_dsa_v, cache_dsa_kidx,
           c_prompt, c_sample, w_ada, b_ada, g_pre_mix, g_post_mix, g_pre_ffn, g_post_ffn, w_in,
           lambda_q1, lambda_k1, lambda_q2, lambda_k2, g_subln, w_out, w_gate_up, w_down,
           rel_bias_table):
    Bp = x_prompt.shape[0]
    c_all = jnp.concatenate([c_prompt, c_sample], axis=0)
    mod = _modulation(c_all, w_ada[0], b_ada[0])
    mods = [mod[:, j * D_MODEL:(j + 1) * D_MODEL][:, None, :] for j in range(6)]
    mods_p = [m[:Bp] for m in mods]
    mods_s = [m[Bp:] for m in mods]

    wm, wih, wil, wi3 = _prep_in_weights(w_in[0])
    lam4 = jnp.concatenate([lambda_q1, lambda_k1, lambda_q2, lambda_k2], axis=0)
    row = lambda g: g[0].reshape(1, -1)
    params = (row(g_pre_mix), row(g_post_mix), row(g_pre_ffn), row(g_post_ffn), wm, wih, wil, wi3,
              lam4, g_subln[0], w_out[0].astype(BF16), w_gate_up[0].astype(BF16),
              w_down[0].astype(BF16), rel_bias_table)

    yp, rp = _group(x_prompt, mods_p, None, params, bt=1, tt=512, tq=256)
    past = (cache_diff_k[0], cache_diff_v[0], cache_dsa_k[0], cache_dsa_v[0], cache_dsa_kidx[0])
    ys, rs = _group(x_sample, mods_s, past, params, bt=8, tt=64, tq=128)
    return (yp, ys) + rp + rs
```

```python
import functools
import math

import jax
import jax.numpy as jnp
from jax import lax
from jax.experimental import pallas as pl
from jax.experimental.pallas import tpu as pltpu

F32 = jnp.float32
BF16 = jnp.bfloat16
I32 = jnp.int32

D_MODEL = 1024
CHUNK = 64
N_DIFF = 4
DIFF_DH = 64
DIFF_W = N_DIFF * 2 * DIFF_DH
N_DSA = 8
N_DSA_KV = 2
DSA_GROUP = N_DSA // N_DSA_KV
DSA_DH = 64
DSA_W = N_DSA * DSA_DH
DSA_KV_W = N_DSA_KV * DSA_DH
N_IDX = 4
IDX_DH = 64
TOPK = 256
N_HEADS = N_DIFF + N_DSA
NUM_BUCKETS = 32
D_FF = 2816
EPS = 1e-6
NEG = -1e30
LOG2E = math.log2(math.e)
LAM_INIT = 0.8 - 0.6 * math.exp(-0.3 * 0)

LANE = 128
SUBLANE = 8
KEY_BLOCK = 256
V_PAD = 16
V_ROWS = 2 * DIFF_DH + V_PAD
VMEM_LIMIT = 56 * 1024 * 1024
FAR_BUCKET = 15
BUCKET_STEPS = (13, 20, 30, 46, 70, 108, 166)
KEY_BYTES = 4
N_BIAS_TILES = 4
N_STAGE = 2


def _rms(x):
    return x * lax.rsqrt(jnp.mean(x * x, axis=-1, keepdims=True) + EPS)


def _dot(a, b):
    return jnp.dot(a, b, preferred_element_type=F32)


def _dot_t(a, b):
    return lax.dot_general(a, b, (((1,), (1,)), ((), ())), preferred_element_type=F32)


def _mod_kernel(c_ref, w_ref, b_ref, o_ref):
    c = c_ref[...]
    s = c * jax.nn.sigmoid(c)
    o_ref[...] = jnp.dot(s, w_ref[...], preferred_element_type=F32,
                         precision=lax.Precision.HIGHEST) + b_ref[...]


def _modulation(c_all, w_ada, b_ada):
    n = c_all.shape[0]
    tn = 1024
    return pl.pallas_call(
        _mod_kernel,
        grid=(6 * D_MODEL // tn,),
        in_specs=[pl.BlockSpec((n, D_MODEL), lambda j: (0, 0)),
                  pl.BlockSpec((D_MODEL, tn), lambda j: (0, j)),
                  pl.BlockSpec((1, tn), lambda j: (0, j))],
        out_specs=pl.BlockSpec((n, tn), lambda j: (0, j)),
        out_shape=jax.ShapeDtypeStruct((n, 6 * D_MODEL), F32),
        name="adaln_mod",
        compiler_params=pltpu.CompilerParams(dimension_semantics=("arbitrary",),
                                             vmem_limit_bytes=VMEM_LIMIT),
    )(c_all, w_ada, b_ada.reshape(1, -1))


_C_DQ = 0
_C_DK = _C_DQ + DIFF_W
_C_DV = _C_DK + DIFF_W
_C_SQ = _C_DV + DIFF_W
_C_SK = _C_SQ + DSA_W
_C_SV = _C_SK + DSA_KV_W
_C_END = _C_SV + DSA_KV_W
IDX_W = 384


def _split3(a):
    a1 = a.astype(BF16)
    r = a - a1.astype(F32)
    a2 = r.astype(BF16)
    return a1, a2, (r - a2.astype(F32)).astype(BF16)


def _value_rows(v):
    row = lax.broadcasted_iota(I32, (V_PAD, KEY_BLOCK), 0)
    extra = jnp.where(row == 0, 1.0, 0.0)
    return jnp.concatenate([v.T, extra], axis=0).astype(BF16)


def _proj_kernel(x_ref, sc_ref, sh_ref, g_ref, wm_ref, wih_ref, wil_ref, wi3_ref,
                 dk32, dv32, sk32, sv32, ik32, iw32, dq16, dk16, sq16, sk16, iq16, ik16, dvx, svx,
                 *, transpose_values):
    bt, tt, _ = x_ref.shape
    x = x_ref[...]
    h = (_rms(x) * g_ref[...]) * (1.0 + sc_ref[...]) + sh_ref[...]
    h = h.reshape(bt * tt, D_MODEL)
    hb, hl, h3 = _split3(h)
    z = _dot(hb, wm_ref[...])
    wih = wih_ref[...]
    wil = wil_ref[...]
    zi = (_dot(hb, wih) + _dot(hb, wil) + _dot(hl, wih)
          + _dot(hb, wi3_ref[...]) + _dot(h3, wih) + _dot(hl, wil))

    def put(ref, v):
        ref[...] = v.reshape(bt, tt, v.shape[-1]).astype(ref.dtype)

    zdk = z[:, _C_DK:_C_DV]
    zdv = z[:, _C_DV:_C_SQ]
    zsk = z[:, _C_SK:_C_SV]
    zsv = z[:, _C_SV:_C_END]
    zik = zi[:, 256:256 + IDX_DH]
    for n in range(N_DIFF):
        dk32[:, :, n, :] = zdk[:, n * LANE:(n + 1) * LANE].reshape(bt, tt, LANE)
        dv32[:, :, n, :] = zdv[:, n * LANE:(n + 1) * LANE].reshape(bt, tt, LANE)
    put(sk32, zsk)
    put(sv32, zsv)
    put(ik32, zik)
    put(iw32, zi[:, 256:IDX_W])
    low = lax.broadcasted_iota(I32, (bt * tt, LANE), 1) < DIFF_DH
    halves = lambda v: (jnp.where(low, v, 0.0), jnp.where(low, 0.0, v))
    dq_pairs = [halves(z[:, _C_DQ + n * LANE:_C_DQ + (n + 1) * LANE]) for n in range(N_DIFF)]
    sq_pairs = [halves(z[:, _C_SQ + j * LANE:_C_SQ + (j + 1) * LANE]) for j in range(DSA_GROUP)]
    put(dq16, jnp.concatenate([part for pair in dq_pairs for part in pair], axis=1))
    put(dk16, zdk)
    put(sq16, jnp.concatenate([pair[0] for pair in sq_pairs] + [pair[1] for pair in sq_pairs], axis=1))
    put(sk16, zsk)

    q1, q2, q3 = _split3(zi[:, 0:N_IDX * IDX_DH])
    k1, k2, k3 = _split3(zik)
    head = lambda a, n: a[:, n * IDX_DH:(n + 1) * IDX_DH]
    put(iq16, jnp.concatenate([head(p, n) for n in range(N_IDX)
                               for p in (q1, q1, q2, q1, q3, q2)], axis=1))
    put(ik16, jnp.concatenate([k1, k2, k1, k3, k1, k2], axis=1))

    if transpose_values:
        for kb in range(tt // KEY_BLOCK):
            rows = slice(kb * KEY_BLOCK, (kb + 1) * KEY_BLOCK)
            for n in range(N_DIFF):
                dvx[n, kb] = _value_rows(zdv[rows, n * LANE:(n + 1) * LANE])
            for g in range(N_DSA_KV):
                svx[g, kb] = _value_rows(zsv[rows, g * DSA_DH:(g + 1) * DSA_DH])
    else:
        put(dvx, zdv)
        put(svx, zsv)


def _projection(x, sc, sh, g, wm, wih, wil, wi3, bt, tt, transpose_values):
    B, T, _ = x.shape
    grid = (B // bt, T // tt)
    row = lambda w: pl.BlockSpec((bt, tt, w), lambda b, t: (b, t, 0))
    per_b = pl.BlockSpec((bt, 1, D_MODEL), lambda b, t: (b, 0, 0))
    full = lambda a: pl.BlockSpec(a.shape, lambda b, t: (0,) * a.ndim,
                                  pipeline_mode=pl.Buffered(1))
    outs = [(DIFF_W, F32), (DIFF_W, F32), (DSA_KV_W, F32), (DSA_KV_W, F32), (IDX_DH, F32),
            (LANE, F32), (2 * DIFF_W, BF16), (DIFF_W, BF16), (N_DSA * LANE, BF16),
            (DSA_KV_W, BF16), (N_IDX * 384, BF16), (384, BF16)]
    out_specs = [row(w) for w, _ in outs]
    out_shape = [jax.ShapeDtypeStruct((B, T, w), dt) for w, dt in outs]
    for j in range(2):
        out_specs[j] = pl.BlockSpec((bt, tt, N_DIFF, 2 * DIFF_DH), lambda b, t: (b, t, 0, 0))
        out_shape[j] = jax.ShapeDtypeStruct((B, T, N_DIFF, 2 * DIFF_DH), F32)
    if transpose_values:
        assert bt == 1 and tt % KEY_BLOCK == 0
        nb = tt // KEY_BLOCK
        out_specs += [pl.BlockSpec((None, N_DIFF, nb, V_ROWS, KEY_BLOCK), lambda b, t: (b, 0, t, 0, 0)),
                      pl.BlockSpec((None, N_DSA_KV, nb, DSA_DH + V_PAD, KEY_BLOCK),
                                   lambda b, t: (b, 0, t, 0, 0))]
        out_shape += [jax.ShapeDtypeStruct((B, N_DIFF, T // KEY_BLOCK, V_ROWS, KEY_BLOCK), BF16),
                      jax.ShapeDtypeStruct((B, N_DSA_KV, T // KEY_BLOCK, DSA_DH + V_PAD, KEY_BLOCK), BF16)]
    else:
        out_specs += [row(DIFF_W), row(DSA_KV_W)]
        out_shape += [jax.ShapeDtypeStruct((B, T, DIFF_W), BF16),
                      jax.ShapeDtypeStruct((B, T, DSA_KV_W), BF16)]
    return pl.pallas_call(
        functools.partial(_proj_kernel, transpose_values=transpose_values),
        grid=grid,
        in_specs=[row(D_MODEL), per_b, per_b, full(g), full(wm), full(wih), full(wil), full(wi3)],
        out_specs=out_specs,
        out_shape=out_shape,
        name="in_proj",
        compiler_params=pltpu.CompilerParams(dimension_semantics=("arbitrary", "arbitrary"),
                                             vmem_limit_bytes=VMEM_LIMIT),
    )(x, sc, sh, g, wm, wih, wil, wi3)


def _fold_rows(x, rows, op):
    acc = x[0:rows]
    for r in range(1, x.shape[0] // rows):
        acc = op(acc, x[r * rows:(r + 1) * rows])
    return acc


def _init_bias_tiles(tab_ref, tb_ref):
    k = lax.broadcasted_iota(I32, (LANE, LANE), 0)
    q = lax.broadcasted_iota(I32, (LANE, LANE), 1)
    buckets = []
    for t in range(N_BIAS_TILES - 1):
        rel = k - q - LANE * t
        n = jnp.abs(rel)
        large = jnp.full_like(n, 8)
        for step in BUCKET_STEPS:
            large = large + jnp.where(n >= step, 1, 0)
        buckets.append(jnp.where(rel > 0, NUM_BUCKETS // 2, 0) + jnp.where(n < 8, n, large))

    def per_head(h, carry):
        for t, bucket in enumerate(buckets):
            val = jnp.zeros((LANE, LANE), F32)
            for b in range(NUM_BUCKETS):
                val = jnp.where(bucket == b, tab_ref[b, h] * LOG2E, val)
            tb_ref[h, t] = val
        tb_ref[h, N_BIAS_TILES - 1] = jnp.full((LANE, LANE), tab_ref[FAR_BUCKET, h] * LOG2E, F32)
        return carry

    lax.fori_loop(0, N_HEADS, per_head, 0)


def _attn_kernel(tab_ref, lam_ref, gsub_ref, dq_ref, sq_ref, iq_ref, iw_ref,
                 dk_ref, dv_ref, sk_ref, sv_ref, ik_ref, o_ref,
                 tb_ref, ltri_ref, key_ref, byte_ref, s_ref, acc_ref, ot_ref, *, tq, q_offset):
    tk = KEY_BLOCK
    b = pl.program_id(0)
    i = pl.program_id(1)

    @pl.when((b == 0) & (i == 0))
    def _():
        _init_bias_tiles(tab_ref, tb_ref)
        ltri_ref[...] = jnp.where(lax.broadcasted_iota(I32, (tk, tk), 1)
                                  < lax.broadcasted_iota(I32, (tk, tk), 0), 1.0, 0.0).astype(BF16)

    qpos0 = q_offset + i * tq
    lim_last = CHUNK * ((qpos0 + tq - 1) // CHUNK + 1)
    nblk = (lim_last + tk - 1) // tk
    qq = lax.broadcasted_iota(I32, (1, tq), 1)
    lim_q = CHUNK * ((qpos0 + qq) // CHUNK + 1)
    kk = lax.broadcasted_iota(I32, (tk, tq), 0)

    def key_rows(jb):
        return pl.ds(pl.multiple_of(jb * tk, tk), tk)

    def over_blocks(body, init, unroll=4):
        def run(first, n, carry):
            for u in range(n):
                carry = body(first + u, carry)
            return carry
        carry = lax.fori_loop(0, nblk // unroll, lambda jp, c: run(unroll * jp, unroll, c), init)
        n = unroll // 2
        while n >= 1:
            first = (nblk // (2 * n)) * (2 * n)
            carry = lax.fori_loop(0, (nblk // n) % 2, lambda _, c, first=first, n=n: run(first, n, c),
                                  carry)
            n //= 2
        return carry

    def bias_tile(head, jb):
        rows = []
        for a in range(tk // LANE):
            parts = []
            for c in range(tq // LANE):
                t = jnp.clip((qpos0 + c * LANE - (jb * tk + a * LANE)) // LANE, 0, N_BIAS_TILES - 1)
                parts.append(tb_ref[head, t])
            rows.append(jnp.concatenate(parts, axis=1))
        return jnp.concatenate(rows, axis=0)

    iw = iw_ref[...]

    def score_block(jb, carry):
        kc = ik_ref[key_rows(jb), :]
        sc = jnp.zeros((tk, tq), F32)
        for h in range(N_IDX):
            d = _dot_t(kc, iq_ref[:, h * 384:(h + 1) * 384])
            sc = sc + jnp.maximum(d, 0.0) * iw[h:h + 1]
        sc = jnp.where(kk + jb * tk < lim_q, sc, NEG)
        bits = pltpu.bitcast(sc, I32)
        key = bits ^ ((bits >> 31) & 0x7FFFFFFF)
        key_ref[jb] = key
        for lvl in range(KEY_BYTES):
            byte = ((key >> (8 * lvl)) & 0xFF) if lvl < KEY_BYTES - 1 else (key >> 24) + 128
            byte_ref[lvl, jb] = byte.astype(F32).astype(BF16)
        return carry

    over_blocks(score_block, 0, unroll=4)

    one16 = jnp.ones((), BF16)
    zero16 = jnp.zeros((), BF16)

    def count_digits(lvl, cand):
        def body(jb, acc):
            hit = jnp.where(byte_ref[lvl, jb] >= cand, one16, zero16)
            return acc + _fold_rows(hit, 2 * SUBLANE, jnp.add).astype(F32)
        acc = over_blocks(body, jnp.zeros((2 * SUBLANE, tq), F32))
        return jnp.sum(acc, axis=0, keepdims=True)

    want = jnp.full((1, tq), float(TOPK), F32)
    thr = jnp.zeros((1, tq), I32)
    for lvl in reversed(range(KEY_BYTES)):
        def digit_bit(it, state, lvl=lvl, want=want):
            digit, above = state
            cand = digit + jnp.left_shift(jnp.int32(1), 7 - it).astype(F32)
            cnt = count_digits(lvl, cand.astype(BF16))
            ok = cnt >= want
            return jnp.where(ok, cand, digit), jnp.where(ok, above, cnt)

        digit, above = lax.fori_loop(0, 8, digit_bit,
                                     (jnp.zeros((1, tq), F32), jnp.zeros((1, tq), F32)))
        want = want - above
        digit_i = digit.astype(I32)
        thr = thr | ((digit_i - 128) << 24 if lvl == KEY_BYTES - 1 else digit_i << (8 * lvl))
        if lvl > 0:
            def narrow(jb, carry, lvl=lvl, digit16=digit.astype(BF16)):
                byte_ref[lvl - 1, jb] = jnp.where(byte_ref[lvl, jb] == digit16,
                                                  byte_ref[lvl - 1, jb], -one16)
                return carry
            over_blocks(narrow, 0)

    def mask_block(jb, before):
        kb = key_ref[jb]
        eq = jnp.where(kb == thr, 1.0, 0.0)
        rank = _dot(ltri_ref[...], eq.astype(BF16)) + before
        keep = jnp.where(rank < want, eq, 0.0)
        sel = jnp.where(kb > thr, 0.0, (keep - 1.0) * (-NEG))
        sel = jnp.where(kk + jb * tk < lim_q, sel, NEG)
        key_ref[jb] = pltpu.bitcast(sel, I32)
        return before + jnp.sum(_fold_rows(eq, SUBLANE, jnp.add), axis=0, keepdims=True)

    over_blocks(mask_block, jnp.zeros((1, tq), F32))

    lam = lam_ref[...]
    lam_full = (jnp.exp(jnp.sum(lam[0:1] * lam[1:2], axis=1, keepdims=True))
                - jnp.exp(jnp.sum(lam[2:3] * lam[3:4], axis=1, keepdims=True)) + LAM_INIT)

    def dsa_group(pair):
        heads = [N_STAGE * pair + n for n in range(N_STAGE)]

        def scores(jb):
            kb = sk_ref[key_rows(jb), :]
            sel = pltpu.bitcast(key_ref[jb], F32)
            return [_dot_t(kb, sq_ref[:, h * LANE:(h + 1) * LANE]) + bias_tile(N_DIFF + h, jb) + sel
                    for h in heads]

        def finish():
            for n, h in enumerate(heads):
                a = acc_ref[n]
                out = a[0:DSA_DH] / a[DSA_DH:DSA_DH + 1]
                ot_ref[DIFF_W + h * DSA_DH:DIFF_W + (h + 1) * DSA_DH, :] = out

        return scores, (lambda n, jb: sv_ref[heads[n] // DSA_GROUP, jb]), finish

    def diff_group(h):
        def scores(jb):
            kb = dk_ref[key_rows(jb), h * LANE:(h + 1) * LANE]
            bias = bias_tile(h, jb) + jnp.where(kk + jb * tk < lim_q, 0.0, NEG)
            return [_dot_t(kb, dq_ref[:, (2 * h + part) * LANE:(2 * h + part + 1) * LANE]) + bias
                    for part in range(2)]

        def finish():
            a1 = acc_ref[0]
            a2 = acc_ref[1]
            od = (a1[0:LANE] / a1[LANE:LANE + 1]
                  - lam_full * (a2[0:LANE] / a2[LANE:LANE + 1]))
            od = od * lax.rsqrt(jnp.mean(od * od, axis=0, keepdims=True) + EPS)
            ot_ref[h * LANE:(h + 1) * LANE, :] = (od * gsub_ref[...]) * (1.0 - LAM_INIT)

        return scores, (lambda n, jb: dv_ref[h, jb]), finish

    groups = ([dsa_group(p) for p in range(N_DSA // N_STAGE)]
              + [diff_group(h) for h in range(N_DIFF)])
    m_prev = None
    for k in range(len(groups) + 1):
        produce = groups[k] if k < len(groups) else None
        consume = groups[k - 1] if k > 0 else None

        def step(jb, mx, k=k, produce=produce, consume=consume, m_prev=m_prev):
            if consume is not None:
                for n in range(N_STAGE):
                    p = jnp.exp2(s_ref[(k - 1) % 2, n, jb] - m_prev[n]).astype(BF16)
                    v = consume[1](n, jb)
                    acc_ref[n, 0:v.shape[0]] += _dot(v, p)
            if produce is None:
                return mx
            out = []
            for n, s in enumerate(produce[0](jb)):
                s_ref[k % 2, n, jb] = s
                out.append(jnp.maximum(mx[n], _fold_rows(s, SUBLANE, jnp.maximum)))
            return tuple(out)

        if consume is not None:
            for n in range(N_STAGE):
                acc_ref[n] = jnp.zeros((V_ROWS, tq), F32)
        mx = over_blocks(step, tuple(jnp.full((SUBLANE, tq), -3e38, F32) for _ in range(N_STAGE)),
                         unroll=8)
        if consume is not None:
            consume[2]()
        m_prev = [jnp.max(x, axis=0, keepdims=True) for x in mx]

    o_ref[...] = ot_ref[...].T.astype(o_ref.dtype)


def _fill_key_operands(new, past, out):
    ndk, ndv, nsk, nsv, nik = new
    pk, pv, psk, psv, pik = past
    dk_s, dv_s, sk_s, sv_s, ik_s = out
    P, T, Lp = pik.shape[0], ndk.shape[0], dk_s.shape[0]

    def rows(dst, fresh):
        dst[P:P + T] = fresh
        dst[P + T:Lp] = jnp.zeros((Lp - P - T, dst.shape[1]), BF16)

    for n in range(N_DIFF):
        dk_s[0:P, n * LANE:(n + 1) * LANE] = pk[n][...].astype(BF16)
    rows(dk_s, ndk[...])
    sk_s[0:P] = psk[...].astype(BF16)
    rows(sk_s, nsk[...])
    k1, k2, k3 = _split3(pik[...])
    ik_s[0:P] = jnp.concatenate([k1, k2, k1, k3, k1, k2], axis=1)
    rows(ik_s, nik[...])

    for kb in range(P // KEY_BLOCK):
        r = slice(kb * KEY_BLOCK, (kb + 1) * KEY_BLOCK)
        for n in range(N_DIFF):
            dv_s[n, kb] = _value_rows(pv[n][r, :])
        for g in range(N_DSA_KV):
            sv_s[g, kb] = _value_rows(psv[r, g * DSA_DH:(g + 1) * DSA_DH])
    tail = lambda a: jnp.concatenate(
        [a[...].astype(F32), jnp.zeros((KEY_BLOCK - T, a.shape[1]), F32)], axis=0)
    tail_dv, tail_sv = tail(ndv), tail(nsv)
    for n in range(N_DIFF):
        dv_s[n, P // KEY_BLOCK] = _value_rows(tail_dv[:, n * LANE:(n + 1) * LANE])
    for g in range(N_DSA_KV):
        sv_s[g, P // KEY_BLOCK] = _value_rows(tail_sv[:, g * DSA_DH:(g + 1) * DSA_DH])


def _attn_cached_kernel(tab_ref, lam_ref, gsub_ref, dq_ref, sq_ref, iq_ref, iw_ref,
                        ndk, ndv, nsk, nsv, nik, pk_hbm, pv_hbm, psk, psv, pik, o_ref,
                        tb_ref, ltri_ref, key_ref, byte_ref, s_ref, acc_ref, ot_ref,
                        dk_s, dv_s, sk_s, sv_s, ik_s, kv_buf, kv_sem, *, tq, q_offset):
    b = pl.program_id(0)
    slot = b % 2

    def cache_copies(batch, into):
        return [pltpu.make_async_copy(src.at[batch, :, n, :], kv_buf.at[into, t, n],
                                      kv_sem.at[into, t, n])
                for t, src in enumerate((pk_hbm, pv_hbm)) for n in range(N_DIFF)]

    @pl.when(b == 0)
    def _():
        for copy in cache_copies(0, 0):
            copy.start()

    for copy in cache_copies(b, slot):
        copy.wait()

    @pl.when(b + 1 < pl.num_programs(0))
    def _():
        for copy in cache_copies(b + 1, 1 - slot):
            copy.start()

    pk = [kv_buf.at[slot, 0, n] for n in range(N_DIFF)]
    pv = [kv_buf.at[slot, 1, n] for n in range(N_DIFF)]
    _fill_key_operands((ndk, ndv, nsk, nsv, nik), (pk, pv, psk, psv, pik),
                       (dk_s, dv_s, sk_s, sv_s, ik_s))
    _attn_kernel(tab_ref, lam_ref, gsub_ref, dq_ref, sq_ref, iq_ref, iw_ref,
                 dk_s, dv_s, sk_s, sv_s, ik_s, o_ref,
                 tb_ref, ltri_ref, key_ref, byte_ref, s_ref, acc_ref, ot_ref, tq=tq, q_offset=q_offset)


def _attention(tab, lam4, gsub_t, dq, sq, iq, iw_t, keys, tq, q_offset, past=None):
    B, T, _ = dq.shape
    Lp = keys[0].shape[1] if past is None else past[0].shape[1] + KEY_BLOCK
    nblk_max = Lp // KEY_BLOCK
    qspec = lambda w: pl.BlockSpec((None, tq, w), lambda b, i: (b, i, 0))
    once = pl.Buffered(1)
    small = lambda a: pl.BlockSpec(a.shape, lambda b, i: (0,) * a.ndim)
    per_batch = lambda a, mode=None: pl.BlockSpec((None,) + a.shape[1:],
                                                  lambda b, i: (b,) + (0,) * (a.ndim - 1),
                                                  pipeline_mode=mode)
    operand_shapes = [((Lp, DIFF_W), BF16), ((N_DIFF, nblk_max, V_ROWS, KEY_BLOCK), BF16),
                      ((Lp, DSA_KV_W), BF16),
                      ((N_DSA_KV, nblk_max, DSA_DH + V_PAD, KEY_BLOCK), BF16), ((Lp, 384), BF16)]
    scratch = [pltpu.VMEM((N_HEADS, N_BIAS_TILES, LANE, LANE), F32),
               pltpu.VMEM((KEY_BLOCK, KEY_BLOCK), BF16),
               pltpu.VMEM((nblk_max, KEY_BLOCK, tq), I32),
               pltpu.VMEM((KEY_BYTES, nblk_max, KEY_BLOCK, tq), BF16),
               pltpu.VMEM((2, N_STAGE, nblk_max, KEY_BLOCK, tq), F32),
               pltpu.VMEM((N_STAGE, V_ROWS, tq), F32),
               pltpu.VMEM((D_MODEL, tq), F32)]
    if past is None:
        kern = _attn_kernel
        key_specs = [per_batch(a, once) for a in keys]
        key_args = list(keys)
    else:
        assert T == tq and past[0].shape[1] % KEY_BLOCK == 0 and keys[0].shape[1] <= KEY_BLOCK
        kern = _attn_cached_kernel
        pk, pv, psk, psv, pik = past
        P = pk.shape[1]
        flat = lambda a: a.reshape(B, P, -1)
        hbm = pl.BlockSpec(memory_space=pl.ANY)
        key_specs = [per_batch(a) for a in keys] + [hbm, hbm] + [
            per_batch(flat(a)) for a in (psk, psv, pik)]
        key_args = list(keys) + [pk, pv, flat(psk), flat(psv), flat(pik)]
        scratch += [pltpu.VMEM(shape, dt) for shape, dt in operand_shapes]
        scratch += [pltpu.VMEM((2, 2, N_DIFF, P, 2 * DIFF_DH), F32),
                    pltpu.SemaphoreType.DMA((2, 2, N_DIFF))]
    return pl.pallas_call(
        functools.partial(kern, tq=tq, q_offset=q_offset),
        grid=(B, T // tq),
        in_specs=[pl.BlockSpec(memory_space=pltpu.SMEM), small(lam4), small(gsub_t),
                  qspec(2 * DIFF_W), qspec(N_DSA * LANE), qspec(N_IDX * 384),
                  pl.BlockSpec((None, SUBLANE, tq), lambda b, i: (b, 0, i))] + key_specs,
        out_specs=pl.BlockSpec((None, tq, D_MODEL), lambda b, i: (b, i, 0)),
        out_shape=jax.ShapeDtypeStruct((B, T, D_MODEL), BF16),
        name="attn",
        scratch_shapes=scratch,
        compiler_params=pltpu.CompilerParams(dimension_semantics=("arbitrary", "arbitrary"),
                                             vmem_limit_bytes=VMEM_LIMIT),
    )(tab, lam4, gsub_t, dq, sq, iq, iw_t, *key_args)


FF_CUTS = (0, 1536, D_FF)


def _ffn_kernel(x_ref, o_ref, ga1_ref, sc2_ref, sh2_ref, ga2_ref, gpm_ref, gpf_ref, gpo_ref,
                wo_ref, wgu_ref, wd_ref, y_ref):
    bt, tt, _ = x_ref.shape
    rows = bt * tt
    m = _dot(o_ref[...].reshape(rows, D_MODEL), wo_ref[...]).reshape(bt, tt, D_MODEL)
    x1 = x_ref[...] + ga1_ref[...] * (_rms(m) * gpm_ref[...])
    h2 = (_rms(x1) * gpf_ref[...]) * (1.0 + sc2_ref[...]) + sh2_ref[...]
    h2 = h2.reshape(rows, D_MODEL).astype(BF16)
    f = jnp.zeros((rows, D_MODEL), F32)
    for lo, hi in zip(FF_CUTS[:-1], FF_CUTS[1:]):
        g = _dot(h2, wgu_ref[:, lo:hi])
        u = _dot(h2, wgu_ref[:, D_FF + lo:D_FF + hi])
        a = (g * jax.nn.sigmoid(g) * u).astype(BF16)
        f = f + _dot(a, wd_ref[lo:hi, :])
    f = f.reshape(bt, tt, D_MODEL)
    y_ref[...] = x1 + ga2_ref[...] * (_rms(f) * gpo_ref[...])


def _out_ffn(x, o, ga1, sc2, sh2, ga2, gpm, gpf, gpo, wo, wgu, wd, bt, tt):
    B, T, _ = x.shape
    row = pl.BlockSpec((bt, tt, D_MODEL), lambda b, t: (b, t, 0))
    per_b = pl.BlockSpec((bt, 1, D_MODEL), lambda b, t: (b, 0, 0))
    full = lambda a: pl.BlockSpec(a.shape, lambda b, t: (0,) * a.ndim,
                                  pipeline_mode=pl.Buffered(1))
    return pl.pallas_call(
        _ffn_kernel,
        grid=(B // bt, T // tt),
        in_specs=[row, row, per_b, per_b, per_b, per_b, full(gpm), full(gpf), full(gpo),
                  full(wo), full(wgu), full(wd)],
        out_specs=row,
        out_shape=jax.ShapeDtypeStruct((B, T, D_MODEL), F32),
        name="out_ffn",
        compiler_params=pltpu.CompilerParams(dimension_semantics=("arbitrary", "arbitrary"),
                                             vmem_limit_bytes=VMEM_LIMIT),
    )(x, o, ga1, sc2, sh2, ga2, gpm, gpf, gpo, wo, wgu, wd)


def _prep_in_weights(w_in):
    cuts = [0, 512, 1024, 1536, 2048, 2176, 2304, 2560, 2624, 2628]
    wdq, wdk, wdv, wsq, wsk, wsv, wiq, wik, wiw = [w_in[:, a:b] for a, b in zip(cuts[:-1], cuts[1:])]
    scale = DIFF_DH ** -0.5 * LOG2E
    sq_head = lambda h: wsq[:, h * DSA_DH:(h + 1) * DSA_DH]
    sq_cols = [sq_head(j + g * DSA_GROUP) for j in range(DSA_GROUP) for g in range(N_DSA_KV)]
    wm = jnp.concatenate([wdq * scale, wdk, wdv] + [c * scale for c in sq_cols] + [wsk, wsv],
                         axis=1).astype(BF16)
    wi = jnp.concatenate([wiq, wik, wiw, jnp.zeros((D_MODEL, IDX_W - 324), F32)], axis=1)
    return (wm,) + _split3(wi)


def _pad_axis1(a, n):
    return jnp.pad(a, ((0, 0), (0, n - a.shape[1])) + ((0, 0),) * (a.ndim - 2))


def _group(x, mods, past, params, bt, tt, tq):
    (g_pre_mix, g_post_mix, g_pre_ffn, g_post_ffn, wm, wih, wil, wi3, lam4, g_subln,
     wo, wgu, wd, tab) = params
    B, T, _ = x.shape
    sh1, sc1, ga1, sh2, sc2, ga2 = mods
    (dk32, dv32, sk32, sv32, ik32, iw32, dq16, dk16, sq16, sk16, iq16, ik16, dvx, svx) = _projection(
        x, sc1, sh1, g_pre_mix, wm, wih, wil, wi3, bt, tt, transpose_values=past is None)
    iw = iw32[..., IDX_DH:IDX_DH + N_IDX] * (1.0 / 16.0)
    if past is None:
        q_offset = 0
    else:
        q_offset = past[0].shape[1]
    tpad = -(-T // tq) * tq
    qpad = lambda a: _pad_axis1(a, tpad)
    iw_t = _pad_axis1(jnp.transpose(qpad(iw), (0, 2, 1)), SUBLANE)
    gsub_t = jnp.broadcast_to(g_subln.reshape(-1, 1), (2 * DIFF_DH, tq))
    o = _attention(tab, lam4, gsub_t, qpad(dq16), qpad(sq16), qpad(iq16), iw_t,
                   (dk16, dvx, sk16, svx, ik16), tq, q_offset, past)[:, :T]
    y = _out_ffn(x, o, ga1, sc2, sh2, ga2, g_post_mix, g_pre_ffn, g_post_ffn, wo, wgu, wd, bt, tt)
    rows = (dk32[None], dv32[None],
            sk32.reshape(1, B, T, N_DSA_KV, DSA_DH), sv32.reshape(1, B, T, N_DSA_KV, DSA_DH),
            ik32.reshape(1, B, T, IDX_DH))
    return y, rows


def kernel(x_prompt, x_sample, cache_diff_k, cache_diff_v, cache_dsa_k, cache_dsa_v, cache_dsa_kidx,
           c_prompt, c_sample, w_ada, b_ada, g_pre_mix, g_post_mix, g_pre_ffn, g_post_ffn, w_in,
           lambda_q1, lambda_k1, lambda_q2, lambda_k2, g_subln, w_out, w_gate_up, w_down,
           rel_bias_table):
    Bp = x_prompt.shape[0]
    c_all = jnp.concatenate([c_prompt, c_sample], axis=0)
    mod = _modulation(c_all, w_ada[0], b_ada[0])
    mods = [mod[:, j * D_MODEL:(j + 1) * D_MODEL][:, None, :] for j in range(6)]
    mods_p = [m[:Bp] for m in mods]
    mods_s = [m[Bp:] for m in mods]

    wm, wih, wil, wi3 = _prep_in_weights(w_in[0])
    lam4 = jnp.concatenate([lambda_q1, lambda_k1, lambda_q2, lambda_k2], axis=0)
    row = lambda g: g[0].reshape(1, -1)
    params = (row(g_pre_mix), row(g_post_mix), row(g_pre_ffn), row(g_post_ffn), wm, wih, wil, wi3,
              lam4, g_subln[0], w_out[0].astype(BF16), w_gate_up[0].astype(BF16),
              w_down[0].astype(BF16), rel_bias_table)

    yp, rp = _group(x_prompt, mods_p, None, params, bt=1, tt=512, tq=256)
    past = (cache_diff_k[0], cache_diff_v[0], cache_dsa_k[0], cache_dsa_v[0], cache_dsa_kidx[0])
    ys, rs = _group(x_sample, mods_s, past, params, bt=8, tt=64, tq=128)
    return (yp, ys) + rp + rs
```

```python
import functools
import math

import jax
import jax.numpy as jnp
from jax import lax
from jax.experimental import pallas as pl
from jax.experimental.pallas import tpu as pltpu

F32 = jnp.float32
BF16 = jnp.bfloat16
I32 = jnp.int32

D_MODEL = 1024
CHUNK = 64
N_DIFF = 4
DIFF_DH = 64
DIFF_W = N_DIFF * 2 * DIFF_DH
N_DSA = 8
N_DSA_KV = 2
DSA_GROUP = N_DSA // N_DSA_KV
DSA_DH = 64
DSA_W = N_DSA * DSA_DH
DSA_KV_W = N_DSA_KV * DSA_DH
N_IDX = 4
IDX_DH = 64
TOPK = 256
N_HEADS = N_DIFF + N_DSA
NUM_BUCKETS = 32
D_FF = 2816
EPS = 1e-6
NEG = -1e30
LOG2E = math.log2(math.e)
LAM_INIT = 0.8 - 0.6 * math.exp(-0.3 * 0)

LANE = 128
SUBLANE = 8
KEY_BLOCK = 256
V_PAD = 16
V_ROWS = 2 * DIFF_DH + V_PAD
VMEM_LIMIT = 56 * 1024 * 1024
FAR_BUCKET = 15
BUCKET_STEPS = (13, 20, 30, 46, 70, 108, 166)
KEY_BYTES = 4
N_BIAS_TILES = 4
N_STAGE = 2


def _rms(x):
    return x * lax.rsqrt(jnp.mean(x * x, axis=-1, keepdims=True) + EPS)


def _dot(a, b):
    return jnp.dot(a, b, preferred_element_type=F32)


def _dot_t(a, b):
    return lax.dot_general(a, b, (((1,), (1,)), ((), ())), preferred_element_type=F32)


def _mod_kernel(c_ref, w_ref, b_ref, o_ref):
    c = c_ref[...]
    s = c * jax.nn.sigmoid(c)
    o_ref[...] = jnp.dot(s, w_ref[...], preferred_element_type=F32,
                         precision=lax.Precision.HIGHEST) + b_ref[...]


def _modulation(c_all, w_ada, b_ada):
    n = c_all.shape[0]
    tn = 1024
    return pl.pallas_call(
        _mod_kernel,
        grid=(6 * D_MODEL // tn,),
        in_specs=[pl.BlockSpec((n, D_MODEL), lambda j: (0, 0)),
                  pl.BlockSpec((D_MODEL, tn), lambda j: (0, j)),
                  pl.BlockSpec((1, tn), lambda j: (0, j))],
        out_specs=pl.BlockSpec((n, tn), lambda j: (0, j)),
        out_shape=jax.ShapeDtypeStruct((n, 6 * D_MODEL), F32),
        name="adaln_mod",
        compiler_params=pltpu.CompilerParams(dimension_semantics=("arbitrary",),
                                             vmem_limit_bytes=VMEM_LIMIT),
    )(c_all, w_ada, b_ada.reshape(1, -1))


_C_DQ = 0
_C_DK = _C_DQ + DIFF_W
_C_DV = _C_DK + DIFF_W
_C_SQ = _C_DV + DIFF_W
_C_SK = _C_SQ + DSA_W
_C_SV = _C_SK + DSA_KV_W
_C_END = _C_SV + DSA_KV_W
IDX_W = 384


def _split3(a):
    a1 = a.astype(BF16)
    r = a - a1.astype(F32)
    a2 = r.astype(BF16)
    return a1, a2, (r - a2.astype(F32)).astype(BF16)


def _value_rows(v):
    row = lax.broadcasted_iota(I32, (V_PAD, KEY_BLOCK), 0)
    extra = jnp.where(row == 0, 1.0, 0.0)
    return jnp.concatenate([v.T, extra], axis=0).astype(BF16)


def _proj_kernel(x_ref, sc_ref, sh_ref, g_ref, wm_ref, wq1_ref, wq2_ref, wq3_ref, wn12_ref, wn31_ref,
                 dk32, dv32, sk32, sv32, ik32, iw32, dq16, dk16, sq16, sk16, iq16, ik16, dvx, svx,
                 *, transpose_values):
    bt, tt, _ = x_ref.shape
    x = x_ref[...]
    h = (_rms(x) * g_ref[...]) * (1.0 + sc_ref[...]) + sh_ref[...]
    h = h.reshape(bt * tt, D_MODEL)
    hb, hl, h3 = _split3(h)
    z = _dot(hb, wm_ref[...])
    wq1 = wq1_ref[...]
    wq2 = wq2_ref[...]
    zq = (_dot(hb, wq1) + _dot(hb, wq2) + _dot(hl, wq1)
          + _dot(hb, wq3_ref[...]) + _dot(h3, wq1) + _dot(hl, wq2))
    wn12 = wn12_ref[...]
    wn31 = wn31_ref[...]
    p1, p2 = _dot(hb, wn12), _dot(hl, wn12)
    p3, p4 = _dot(hb, wn31), _dot(h3, wn31)
    zn = (p1[:, :LANE] + p1[:, LANE:] + p2[:, :LANE] + p3[:, :LANE] + p4[:, LANE:] + p2[:, LANE:])
    zi = jnp.concatenate([zq, zn], axis=1)

    def put(ref, v):
        ref[...] = v.reshape(bt, tt, v.shape[-1]).astype(ref.dtype)

    zdk = z[:, _C_DK:_C_DV]
    zdv = z[:, _C_DV:_C_SQ]
    zsk = z[:, _C_SK:_C_SV]
    zsv = z[:, _C_SV:_C_END]
    zik = zi[:, 256:256 + IDX_DH]
    for n in range(N_DIFF):
        dk32[:, :, n, :] = zdk[:, n * LANE:(n + 1) * LANE].reshape(bt, tt, LANE)
        dv32[:, :, n, :] = zdv[:, n * LANE:(n + 1) * LANE].reshape(bt, tt, LANE)
    put(sk32, zsk)
    put(sv32, zsv)
    put(ik32, zik)
    put(iw32, zi[:, 256:IDX_W])
    low = lax.broadcasted_iota(I32, (bt * tt, LANE), 1) < DIFF_DH
    halves = lambda v: (jnp.where(low, v, 0.0), jnp.where(low, 0.0, v))
    dq_pairs = [halves(z[:, _C_DQ + n * LANE:_C_DQ + (n + 1) * LANE]) for n in range(N_DIFF)]
    sq_pairs = [halves(z[:, _C_SQ + j * LANE:_C_SQ + (j + 1) * LANE]) for j in range(DSA_GROUP)]
    put(dq16, jnp.concatenate([part for pair in dq_pairs for part in pair], axis=1))
    put(dk16, zdk)
    put(sq16, jnp.concatenate([pair[0] for pair in sq_pairs] + [pair[1] for pair in sq_pairs], axis=1))
    put(sk16, zsk)

    q1, q2, q3 = _split3(zi[:, 0:N_IDX * IDX_DH])
    k1, k2, k3 = _split3(zik)
    head = lambda a, n: a[:, n * IDX_DH:(n + 1) * IDX_DH]
    put(iq16, jnp.concatenate([head(p, n) for n in range(N_IDX)
                               for p in (q1, q1, q2, q1, q3, q2)], axis=1))
    put(ik16, jnp.concatenate([k1, k2, k1, k3, k1, k2], axis=1))

    if transpose_values:
        for kb in range(tt // KEY_BLOCK):
            rows = slice(kb * KEY_BLOCK, (kb + 1) * KEY_BLOCK)
            for n in range(N_DIFF):
                dvx[n, kb] = _value_rows(zdv[rows, n * LANE:(n + 1) * LANE])
            for g in range(N_DSA_KV):
                svx[g, kb] = _value_rows(zsv[rows, g * DSA_DH:(g + 1) * DSA_DH])
    else:
        put(dvx, zdv)
        put(svx, zsv)


def _projection(x, sc, sh, g, w_proj, bt, tt, transpose_values):
    B, T, _ = x.shape
    grid = (B // bt, T // tt)
    row = lambda w: pl.BlockSpec((bt, tt, w), lambda b, t: (b, t, 0))
    per_b = pl.BlockSpec((bt, 1, D_MODEL), lambda b, t: (b, 0, 0))
    full = lambda a: pl.BlockSpec(a.shape, lambda b, t: (0,) * a.ndim,
                                  pipeline_mode=pl.Buffered(1))
    outs = [(DIFF_W, F32), (DIFF_W, F32), (DSA_KV_W, F32), (DSA_KV_W, F32), (IDX_DH, F32),
            (LANE, F32), (2 * DIFF_W, BF16), (DIFF_W, BF16), (N_DSA * LANE, BF16),
            (DSA_KV_W, BF16), (N_IDX * 384, BF16), (384, BF16)]
    out_specs = [row(w) for w, _ in outs]
    out_shape = [jax.ShapeDtypeStruct((B, T, w), dt) for w, dt in outs]
    for j in range(2):
        out_specs[j] = pl.BlockSpec((bt, tt, N_DIFF, 2 * DIFF_DH), lambda b, t: (b, t, 0, 0))
        out_shape[j] = jax.ShapeDtypeStruct((B, T, N_DIFF, 2 * DIFF_DH), F32)
    if transpose_values:
        assert bt == 1 and tt % KEY_BLOCK == 0
        nb = tt // KEY_BLOCK
        out_specs += [pl.BlockSpec((None, N_DIFF, nb, V_ROWS, KEY_BLOCK), lambda b, t: (b, 0, t, 0, 0)),
                      pl.BlockSpec((None, N_DSA_KV, nb, DSA_DH + V_PAD, KEY_BLOCK),
                                   lambda b, t: (b, 0, t, 0, 0))]
        out_shape += [jax.ShapeDtypeStruct((B, N_DIFF, T // KEY_BLOCK, V_ROWS, KEY_BLOCK), BF16),
                      jax.ShapeDtypeStruct((B, N_DSA_KV, T // KEY_BLOCK, DSA_DH + V_PAD, KEY_BLOCK), BF16)]
    else:
        out_specs += [row(DIFF_W), row(DSA_KV_W)]
        out_shape += [jax.ShapeDtypeStruct((B, T, DIFF_W), BF16),
                      jax.ShapeDtypeStruct((B, T, DSA_KV_W), BF16)]
    return pl.pallas_call(
        functools.partial(_proj_kernel, transpose_values=transpose_values),
        grid=grid,
        in_specs=[row(D_MODEL), per_b, per_b, full(g)] + [full(w) for w in w_proj],
        out_specs=out_specs,
        out_shape=out_shape,
        name="in_proj",
        compiler_params=pltpu.CompilerParams(dimension_semantics=("arbitrary", "arbitrary"),
                                             vmem_limit_bytes=VMEM_LIMIT),
    )(x, sc, sh, g, *w_proj)


def _fold_rows(x, rows, op):
    acc = x[0:rows]
    for r in range(1, x.shape[0] // rows):
        acc = op(acc, x[r * rows:(r + 1) * rows])
    return acc


def _init_bias_tiles(tab_ref, tb_ref):
    k = lax.broadcasted_iota(I32, (LANE, LANE), 0)
    q = lax.broadcasted_iota(I32, (LANE, LANE), 1)
    buckets = []
    for t in range(N_BIAS_TILES - 1):
        rel = k - q - LANE * t
        n = jnp.abs(rel)
        large = jnp.full_like(n, 8)
        for step in BUCKET_STEPS:
            large = large + jnp.where(n >= step, 1, 0)
        buckets.append(jnp.where(rel > 0, NUM_BUCKETS // 2, 0) + jnp.where(n < 8, n, large))

    def per_head(h, carry):
        for t, bucket in enumerate(buckets):
            val = jnp.zeros((LANE, LANE), F32)
            for b in range(NUM_BUCKETS):
                val = jnp.where(bucket == b, tab_ref[b, h] * LOG2E, val)
            tb_ref[h, t] = val
        tb_ref[h, N_BIAS_TILES - 1] = jnp.full((LANE, LANE), tab_ref[FAR_BUCKET, h] * LOG2E, F32)
        return carry

    lax.fori_loop(0, N_HEADS, per_head, 0)


def _attn_kernel(tab_ref, lam_ref, gsub_ref, dq_ref, sq_ref, iq_ref, iw_ref,
                 dk_ref, dv_ref, sk_ref, sv_ref, ik_ref, o_ref,
                 tb_ref, ltri_ref, key_ref, byte_ref, s_ref, acc_ref, ot_ref, *, tq, q_offset):
    tk = KEY_BLOCK
    b = pl.program_id(0)
    i = pl.program_id(1)

    @pl.when((b == 0) & (i == 0))
    def _():
        _init_bias_tiles(tab_ref, tb_ref)
        ltri_ref[...] = jnp.where(lax.broadcasted_iota(I32, (tk, tk), 1)
                                  < lax.broadcasted_iota(I32, (tk, tk), 0), 1.0, 0.0).astype(BF16)

    qpos0 = q_offset + i * tq
    lim_last = CHUNK * ((qpos0 + tq - 1) // CHUNK + 1)
    nblk = (lim_last + tk - 1) // tk
    qq = lax.broadcasted_iota(I32, (1, tq), 1)
    lim_q = CHUNK * ((qpos0 + qq) // CHUNK + 1)
    kk = lax.broadcasted_iota(I32, (tk, tq), 0)

    def key_rows(jb):
        return pl.ds(pl.multiple_of(jb * tk, tk), tk)

    def over_blocks(body, init, unroll=4):
        def run(first, n, carry):
            for u in range(n):
                carry = body(first + u, carry)
            return carry
        carry = lax.fori_loop(0, nblk // unroll, lambda jp, c: run(unroll * jp, unroll, c), init)
        n = unroll // 2
        while n >= 1:
            first = (nblk // (2 * n)) * (2 * n)
            carry = lax.fori_loop(0, (nblk // n) % 2, lambda _, c, first=first, n=n: run(first, n, c),
                                  carry)
            n //= 2
        return carry

    def bias_tile(head, jb):
        rows = []
        for a in range(tk // LANE):
            parts = []
            for c in range(tq // LANE):
                t = jnp.clip((qpos0 + c * LANE - (jb * tk + a * LANE)) // LANE, 0, N_BIAS_TILES - 1)
                parts.append(tb_ref[head, t])
            rows.append(jnp.concatenate(parts, axis=1))
        return jnp.concatenate(rows, axis=0)

    iw = iw_ref[...]

    def score_block(jb, carry):
        kc = ik_ref[key_rows(jb), :]
        sc = jnp.zeros((tk, tq), F32)
        for h in range(N_IDX):
            d = _dot_t(kc, iq_ref[:, h * 384:(h + 1) * 384])
            sc = sc + jnp.maximum(d, 0.0) * iw[h:h + 1]
        sc = jnp.where(kk + jb * tk < lim_q, sc, NEG)
        bits = pltpu.bitcast(sc, I32)
        key = bits ^ ((bits >> 31) & 0x7FFFFFFF)
        key_ref[jb] = key
        for lvl in range(KEY_BYTES):
            byte = ((key >> (8 * lvl)) & 0xFF) if lvl < KEY_BYTES - 1 else (key >> 24) + 128
            byte_ref[lvl, jb] = byte.astype(F32).astype(BF16)
        return carry

    over_blocks(score_block, 0, unroll=4)

    one16 = jnp.ones((), BF16)
    zero16 = jnp.zeros((), BF16)

    def count_digits(lvl, cand):
        def body(jb, acc):
            hit = jnp.where(byte_ref[lvl, jb] >= cand, one16, zero16)
            return acc + _fold_rows(hit, 2 * SUBLANE, jnp.add).astype(F32)
        acc = over_blocks(body, jnp.zeros((2 * SUBLANE, tq), F32))
        return jnp.sum(acc, axis=0, keepdims=True)

    want = jnp.full((1, tq), float(TOPK), F32)
    thr = jnp.zeros((1, tq), I32)
    for lvl in reversed(range(KEY_BYTES)):
        def digit_bit(it, state, lvl=lvl, want=want):
            digit, above = state
            cand = digit + jnp.left_shift(jnp.int32(1), 7 - it).astype(F32)
            cnt = count_digits(lvl, cand.astype(BF16))
            ok = cnt >= want
            return jnp.where(ok, cand, digit), jnp.where(ok, above, cnt)

        digit, above = lax.fori_loop(0, 8, digit_bit,
                                     (jnp.zeros((1, tq), F32), jnp.zeros((1, tq), F32)))
        want = want - above
        digit_i = digit.astype(I32)
        thr = thr | ((digit_i - 128) << 24 if lvl == KEY_BYTES - 1 else digit_i << (8 * lvl))
        if lvl > 0:
            def narrow(jb, carry, lvl=lvl, digit16=digit.astype(BF16)):
                byte_ref[lvl - 1, jb] = jnp.where(byte_ref[lvl, jb] == digit16,
                                                  byte_ref[lvl - 1, jb], -one16)
                return carry
            over_blocks(narrow, 0)

    def mask_block(jb, before):
        kb = key_ref[jb]
        eq = jnp.where(kb == thr, 1.0, 0.0)
        rank = _dot(ltri_ref[...], eq.astype(BF16)) + before
        keep = jnp.where(rank < want, eq, 0.0)
        sel = jnp.where(kb > thr, 0.0, (keep - 1.0) * (-NEG))
        sel = jnp.where(kk + jb * tk < lim_q, sel, NEG)
        key_ref[jb] = pltpu.bitcast(sel, I32)
        return before + jnp.sum(_fold_rows(eq, SUBLANE, jnp.add), axis=0, keepdims=True)

    over_blocks(mask_block, jnp.zeros((1, tq), F32))

    lam = lam_ref[...]
    lam_full = (jnp.exp(jnp.sum(lam[0:1] * lam[1:2], axis=1, keepdims=True))
                - jnp.exp(jnp.sum(lam[2:3] * lam[3:4], axis=1, keepdims=True)) + LAM_INIT)

    def dsa_group(pair):
        heads = [N_STAGE * pair + n for n in range(N_STAGE)]

        def scores(jb):
            kb = sk_ref[key_rows(jb), :]
            sel = pltpu.bitcast(key_ref[jb], F32)
            return [_dot_t(kb, sq_ref[:, h * LANE:(h + 1) * LANE]) + bias_tile(N_DIFF + h, jb) + sel
                    for h in heads]

        def finish():
            for n, h in enumerate(heads):
                a = acc_ref[n]
                out = a[0:DSA_DH] / a[DSA_DH:DSA_DH + 1]
                ot_ref[DIFF_W + h * DSA_DH:DIFF_W + (h + 1) * DSA_DH, :] = out

        return scores, (lambda n, jb: sv_ref[heads[n] // DSA_GROUP, jb]), finish

    def diff_group(h):
        def scores(jb):
            kb = dk_ref[key_rows(jb), h * LANE:(h + 1) * LANE]
            bias = bias_tile(h, jb) + jnp.where(kk + jb * tk < lim_q, 0.0, NEG)
            return [_dot_t(kb, dq_ref[:, (2 * h + part) * LANE:(2 * h + part + 1) * LANE]) + bias
                    for part in range(2)]

        def finish():
            a1 = acc_ref[0]
            a2 = acc_ref[1]
            od = (a1[0:LANE] / a1[LANE:LANE + 1]
                  - lam_full * (a2[0:LANE] / a2[LANE:LANE + 1]))
            od = od * lax.rsqrt(jnp.mean(od * od, axis=0, keepdims=True) + EPS)
            ot_ref[h * LANE:(h + 1) * LANE, :] = (od * gsub_ref[...]) * (1.0 - LAM_INIT)

        return scores, (lambda n, jb: dv_ref[h, jb]), finish

    groups = ([dsa_group(p) for p in range(N_DSA // N_STAGE)]
              + [diff_group(h) for h in range(N_DIFF)])
    m_prev = None
    for k in range(len(groups) + 1):
        produce = groups[k] if k < len(groups) else None
        consume = groups[k - 1] if k > 0 else None

        def step(jb, mx, k=k, produce=produce, consume=consume, m_prev=m_prev):
            if consume is not None:
                for n in range(N_STAGE):
                    p = jnp.exp2(s_ref[(k - 1) % 2, n, jb] - m_prev[n]).astype(BF16)
                    v = consume[1](n, jb)
                    acc_ref[n, 0:v.shape[0]] += _dot(v, p)
            if produce is None:
                return mx
            out = []
            for n, s in enumerate(produce[0](jb)):
                s_ref[k % 2, n, jb] = s
                out.append(jnp.maximum(mx[n], _fold_rows(s, SUBLANE, jnp.maximum)))
            return tuple(out)

        if consume is not None:
            for n in range(N_STAGE):
                acc_ref[n] = jnp.zeros((V_ROWS, tq), F32)
        mx = over_blocks(step, tuple(jnp.full((SUBLANE, tq), -3e38, F32) for _ in range(N_STAGE)),
                         unroll=8)
        if consume is not None:
            consume[2]()
        m_prev = [jnp.max(x, axis=0, keepdims=True) for x in mx]

    o_ref[...] = ot_ref[...].T.astype(o_ref.dtype)


def _fill_key_operands(new, past, out):
    ndk, ndv, nsk, nsv, nik = new
    pk, pv, psk, psv, pik = past
    dk_s, dv_s, sk_s, sv_s, ik_s = out
    P, T, Lp = pik.shape[0], ndk.shape[0], dk_s.shape[0]

    def rows(dst, fresh):
        dst[P:P + T] = fresh
        dst[P + T:Lp] = jnp.zeros((Lp - P - T, dst.shape[1]), BF16)

    for n in range(N_DIFF):
        dk_s[0:P, n * LANE:(n + 1) * LANE] = pk[n][...].astype(BF16)
    rows(dk_s, ndk[...])
    sk_s[0:P] = psk[...].astype(BF16)
    rows(sk_s, nsk[...])
    k1, k2, k3 = _split3(pik[...])
    ik_s[0:P] = jnp.concatenate([k1, k2, k1, k3, k1, k2], axis=1)
    rows(ik_s, nik[...])

    for kb in range(P // KEY_BLOCK):
        r = slice(kb * KEY_BLOCK, (kb + 1) * KEY_BLOCK)
        for n in range(N_DIFF):
            dv_s[n, kb] = _value_rows(pv[n][r, :])
        for g in range(N_DSA_KV):
            sv_s[g, kb] = _value_rows(psv[r, g * DSA_DH:(g + 1) * DSA_DH])
    tail = lambda a: jnp.concatenate(
        [a[...].astype(F32), jnp.zeros((KEY_BLOCK - T, a.shape[1]), F32)], axis=0)
    tail_dv, tail_sv = tail(ndv), tail(nsv)
    for n in range(N_DIFF):
        dv_s[n, P // KEY_BLOCK] = _value_rows(tail_dv[:, n * LANE:(n + 1) * LANE])
    for g in range(N_DSA_KV):
        sv_s[g, P // KEY_BLOCK] = _value_rows(tail_sv[:, g * DSA_DH:(g + 1) * DSA_DH])


def _attn_cached_kernel(tab_ref, lam_ref, gsub_ref, dq_ref, sq_ref, iq_ref, iw_ref,
                        ndk, ndv, nsk, nsv, nik, pk_hbm, pv_hbm, psk, psv, pik, o_ref,
                        tb_ref, ltri_ref, key_ref, byte_ref, s_ref, acc_ref, ot_ref,
                        dk_s, dv_s, sk_s, sv_s, ik_s, kv_buf, kv_sem, *, tq, q_offset):
    b = pl.program_id(0)
    slot = b % 2

    def cache_copies(batch, into):
        return [pltpu.make_async_copy(src.at[batch, :, n, :], kv_buf.at[into, t, n],
                                      kv_sem.at[into, t, n])
                for t, src in enumerate((pk_hbm, pv_hbm)) for n in range(N_DIFF)]

    @pl.when(b == 0)
    def _():
        for copy in cache_copies(0, 0):
            copy.start()

    for copy in cache_copies(b, slot):
        copy.wait()

    @pl.when(b + 1 < pl.num_programs(0))
    def _():
        for copy in cache_copies(b + 1, 1 - slot):
            copy.start()

    pk = [kv_buf.at[slot, 0, n] for n in range(N_DIFF)]
    pv = [kv_buf.at[slot, 1, n] for n in range(N_DIFF)]
    _fill_key_operands((ndk, ndv, nsk, nsv, nik), (pk, pv, psk, psv, pik),
                       (dk_s, dv_s, sk_s, sv_s, ik_s))
    _attn_kernel(tab_ref, lam_ref, gsub_ref, dq_ref, sq_ref, iq_ref, iw_ref,
                 dk_s, dv_s, sk_s, sv_s, ik_s, o_ref,
                 tb_ref, ltri_ref, key_ref, byte_ref, s_ref, acc_ref, ot_ref, tq=tq, q_offset=q_offset)


def _attention(tab, lam4, gsub_t, dq, sq, iq, iw_t, keys, tq, q_offset, past=None):
    B, T, _ = dq.shape
    Lp = keys[0].shape[1] if past is None else past[0].shape[1] + KEY_BLOCK
    nblk_max = Lp // KEY_BLOCK
    qspec = lambda w: pl.BlockSpec((None, tq, w), lambda b, i: (b, i, 0))
    once = pl.Buffered(1)
    small = lambda a: pl.BlockSpec(a.shape, lambda b, i: (0,) * a.ndim)
    per_batch = lambda a, mode=None: pl.BlockSpec((None,) + a.shape[1:],
                                                  lambda b, i: (b,) + (0,) * (a.ndim - 1),
                                                  pipeline_mode=mode)
    operand_shapes = [((Lp, DIFF_W), BF16), ((N_DIFF, nblk_max, V_ROWS, KEY_BLOCK), BF16),
                      ((Lp, DSA_KV_W), BF16),
                      ((N_DSA_KV, nblk_max, DSA_DH + V_PAD, KEY_BLOCK), BF16), ((Lp, 384), BF16)]
    scratch = [pltpu.VMEM((N_HEADS, N_BIAS_TILES, LANE, LANE), F32),
               pltpu.VMEM((KEY_BLOCK, KEY_BLOCK), BF16),
               pltpu.VMEM((nblk_max, KEY_BLOCK, tq), I32),
               pltpu.VMEM((KEY_BYTES, nblk_max, KEY_BLOCK, tq), BF16),
               pltpu.VMEM((2, N_STAGE, nblk_max, KEY_BLOCK, tq), F32),
               pltpu.VMEM((N_STAGE, V_ROWS, tq), F32),
               pltpu.VMEM((D_MODEL, tq), F32)]
    if past is None:
        kern = _attn_kernel
        key_specs = [per_batch(a, once) for a in keys]
        key_args = list(keys)
    else:
        assert T == tq and past[0].shape[1] % KEY_BLOCK == 0 and keys[0].shape[1] <= KEY_BLOCK
        kern = _attn_cached_kernel
        pk, pv, psk, psv, pik = past
        P = pk.shape[1]
        flat = lambda a: a.reshape(B, P, -1)
        hbm = pl.BlockSpec(memory_space=pl.ANY)
        key_specs = [per_batch(a) for a in keys] + [hbm, hbm] + [
            per_batch(flat(a)) for a in (psk, psv, pik)]
        key_args = list(keys) + [pk, pv, flat(psk), flat(psv), flat(pik)]
        scratch += [pltpu.VMEM(shape, dt) for shape, dt in operand_shapes]
        scratch += [pltpu.VMEM((2, 2, N_DIFF, P, 2 * DIFF_DH), F32),
                    pltpu.SemaphoreType.DMA((2, 2, N_DIFF))]
    return pl.pallas_call(
        functools.partial(kern, tq=tq, q_offset=q_offset),
        grid=(B, T // tq),
        in_specs=[pl.BlockSpec(memory_space=pltpu.SMEM), small(lam4), small(gsub_t),
                  qspec(2 * DIFF_W), qspec(N_DSA * LANE), qspec(N_IDX * 384),
                  pl.BlockSpec((None, SUBLANE, tq), lambda b, i: (b, 0, i))] + key_specs,
        out_specs=pl.BlockSpec((None, tq, D_MODEL), lambda b, i: (b, i, 0)),
        out_shape=jax.ShapeDtypeStruct((B, T, D_MODEL), BF16),
        name="attn",
        scratch_shapes=scratch,
        compiler_params=pltpu.CompilerParams(dimension_semantics=("arbitrary", "arbitrary"),
                                             vmem_limit_bytes=VMEM_LIMIT),
    )(tab, lam4, gsub_t, dq, sq, iq, iw_t, *key_args)


FF_CUTS = (0, 1536, D_FF)


def _ffn_kernel(x_ref, o_ref, ga1_ref, sc2_ref, sh2_ref, ga2_ref, gpm_ref, gpf_ref, gpo_ref,
                wo_ref, wgu_ref, wd_ref, y_ref):
    bt, tt, _ = x_ref.shape
    rows = bt * tt
    m = _dot(o_ref[...].reshape(rows, D_MODEL), wo_ref[...]).reshape(bt, tt, D_MODEL)
    x1 = x_ref[...] + ga1_ref[...] * (_rms(m) * gpm_ref[...])
    h2 = (_rms(x1) * gpf_ref[...]) * (1.0 + sc2_ref[...]) + sh2_ref[...]
    h2 = h2.reshape(rows, D_MODEL).astype(BF16)
    f = jnp.zeros((rows, D_MODEL), F32)
    for lo, hi in zip(FF_CUTS[:-1], FF_CUTS[1:]):
        g = _dot(h2, wgu_ref[:, lo:hi])
        u = _dot(h2, wgu_ref[:, D_FF + lo:D_FF + hi])
        a = (g * jax.nn.sigmoid(g) * u).astype(BF16)
        f = f + _dot(a, wd_ref[lo:hi, :])
    f = f.reshape(bt, tt, D_MODEL)
    y_ref[...] = x1 + ga2_ref[...] * (_rms(f) * gpo_ref[...])


def _out_ffn(x, o, ga1, sc2, sh2, ga2, gpm, gpf, gpo, wo, wgu, wd, bt, tt):
    B, T, _ = x.shape
    row = pl.BlockSpec((bt, tt, D_MODEL), lambda b, t: (b, t, 0))
    per_b = pl.BlockSpec((bt, 1, D_MODEL), lambda b, t: (b, 0, 0))
    full = lambda a: pl.BlockSpec(a.shape, lambda b, t: (0,) * a.ndim,
                                  pipeline_mode=pl.Buffered(1))
    return pl.pallas_call(
        _ffn_kernel,
        grid=(B // bt, T // tt),
        in_specs=[row, row, per_b, per_b, per_b, per_b, full(gpm), full(gpf), full(gpo),
                  full(wo), full(wgu), full(wd)],
        out_specs=row,
        out_shape=jax.ShapeDtypeStruct((B, T, D_MODEL), F32),
        name="out_ffn",
        compiler_params=pltpu.CompilerParams(dimension_semantics=("arbitrary", "arbitrary"),
                                             vmem_limit_bytes=VMEM_LIMIT),
    )(x, o, ga1, sc2, sh2, ga2, gpm, gpf, gpo, wo, wgu, wd)


def _prep_in_weights(w_in):
    cuts = [0, 512, 1024, 1536, 2048, 2176, 2304, 2560, 2624, 2628]
    wdq, wdk, wdv, wsq, wsk, wsv, wiq, wik, wiw = [w_in[:, a:b] for a, b in zip(cuts[:-1], cuts[1:])]
    scale = DIFF_DH ** -0.5 * LOG2E
    sq_head = lambda h: wsq[:, h * DSA_DH:(h + 1) * DSA_DH]
    sq_cols = [sq_head(j + g * DSA_GROUP) for j in range(DSA_GROUP) for g in range(N_DSA_KV)]
    wm = jnp.concatenate([wdq * scale, wdk, wdv] + [c * scale for c in sq_cols] + [wsk, wsv],
                         axis=1).astype(BF16)
    n1, n2, n3 = _split3(jnp.concatenate(
        [wik, wiw, jnp.zeros((D_MODEL, LANE - IDX_DH - N_IDX), F32)], axis=1))
    return (wm,) + _split3(wiq) + (jnp.concatenate([n1, n2], axis=1), jnp.concatenate([n3, n1], axis=1))


def _pad_axis1(a, n):
    return jnp.pad(a, ((0, 0), (0, n - a.shape[1])) + ((0, 0),) * (a.ndim - 2))


def _group(x, mods, past, params, bt, tt, tq):
    (g_pre_mix, g_post_mix, g_pre_ffn, g_post_ffn, w_proj, lam4, g_subln,
     wo, wgu, wd, tab) = params
    B, T, _ = x.shape
    sh1, sc1, ga1, sh2, sc2, ga2 = mods
    (dk32, dv32, sk32, sv32, ik32, iw32, dq16, dk16, sq16, sk16, iq16, ik16, dvx, svx) = _projection(
        x, sc1, sh1, g_pre_mix, w_proj, bt, tt, transpose_values=past is None)
    iw = iw32[..., IDX_DH:IDX_DH + N_IDX] * (1.0 / 16.0)
    if past is None:
        q_offset = 0
    else:
        q_offset = past[0].shape[1]
    tpad = -(-T // tq) * tq
    qpad = lambda a: _pad_axis1(a, tpad)
    iw_t = _pad_axis1(jnp.transpose(qpad(iw), (0, 2, 1)), SUBLANE)
    gsub_t = jnp.broadcast_to(g_subln.reshape(-1, 1), (2 * DIFF_DH, tq))
    o = _attention(tab, lam4, gsub_t, qpad(dq16), qpad(sq16), qpad(iq16), iw_t,
                   (dk16, dvx, sk16, svx, ik16), tq, q_offset, past)[:, :T]
    y = _out_ffn(x, o, ga1, sc2, sh2, ga2, g_post_mix, g_pre_ffn, g_post_ffn, wo, wgu, wd, bt, tt)
    rows = (dk32[None], dv32[None],
            sk32.reshape(1, B, T, N_DSA_KV, DSA_DH), sv32.reshape(1, B, T, N_DSA_KV, DSA_DH),
            ik32.reshape(1, B, T, IDX_DH))
    return y, rows


def kernel(x_prompt, x_sample, cache_diff_k, cache_diff_v, cache_dsa_k, cache_dsa_v, cache_dsa_kidx,
           c_prompt, c_sample, w_ada, b_ada, g_pre_mix, g_post_mix, g_pre_ffn, g_post_ffn, w_in,
           lambda_q1, lambda_k1, lambda_q2, lambda_k2, g_subln, w_out, w_gate_up, w_down,
           rel_bias_table):
    Bp = x_prompt.shape[0]
    c_all = jnp.concatenate([c_prompt, c_sample], axis=0)
    mod = _modulation(c_all, w_ada[0], b_ada[0])
    mods = [mod[:, j * D_MODEL:(j + 1) * D_MODEL][:, None, :] for j in range(6)]
    mods_p = [m[:Bp] for m in mods]
    mods_s = [m[Bp:] for m in mods]

    w_proj = _prep_in_weights(w_in[0])
    lam4 = jnp.concatenate([lambda_q1, lambda_k1, lambda_q2, lambda_k2], axis=0)
    row = lambda g: g[0].reshape(1, -1)
    params = (row(g_pre_mix), row(g_post_mix), row(g_pre_ffn), row(g_post_ffn), w_proj,
              lam4, g_subln[0], w_out[0].astype(BF16), w_gate_up[0].astype(BF16),
              w_down[0].astype(BF16), rel_bias_table)

    yp, rp = _group(x_prompt, mods_p, None, params, bt=1, tt=512, tq=256)
    past = (cache_diff_k[0], cache_diff_v[0], cache_dsa_k[0], cache_dsa_v[0], cache_dsa_kidx[0])
    ys, rs = _group(x_sample, mods_s, past, params, bt=8, tt=64, tq=128)
    return (yp, ys) + rp + rs
```

```python
import functools
import math

import jax
import jax.numpy as jnp
from jax import lax
from jax.experimental import pallas as pl
from jax.experimental.pallas import tpu as pltpu

F32 = jnp.float32
BF16 = jnp.bfloat16
I32 = jnp.int32

D_MODEL = 1024
CHUNK = 64
N_DIFF = 4
DIFF_DH = 64
DIFF_W = N_DIFF * 2 * DIFF_DH
N_DSA = 8
N_DSA_KV = 2
DSA_GROUP = N_DSA // N_DSA_KV
DSA_DH = 64
DSA_W = N_DSA * DSA_DH
DSA_KV_W = N_DSA_KV * DSA_DH
N_IDX = 4
IDX_DH = 64
TOPK = 256
N_HEADS = N_DIFF + N_DSA
NUM_BUCKETS = 32
D_FF = 2816
EPS = 1e-6
NEG = -1e30
LOG2E = math.log2(math.e)
LAM_INIT = 0.8 - 0.6 * math.exp(-0.3 * 0)

LANE = 128
SUBLANE = 8
KEY_BLOCK = 256
V_PAD = 16
V_ROWS = 2 * DIFF_DH + V_PAD
VMEM_LIMIT = 56 * 1024 * 1024
FAR_BUCKET = 15
BUCKET_STEPS = (13, 20, 30, 46, 70, 108, 166)
KEY_BYTES = 4
N_BIAS_TILES = 4
N_STAGE = 2


def _rms(x):
    return x * lax.rsqrt(jnp.mean(x * x, axis=-1, keepdims=True) + EPS)


def _dot(a, b):
    return jnp.dot(a, b, preferred_element_type=F32)


def _dot_t(a, b):
    return lax.dot_general(a, b, (((1,), (1,)), ((), ())), preferred_element_type=F32)


def _mod_kernel(c_ref, w_ref, b_ref, o_ref):
    c = c_ref[...]
    s = c * jax.nn.sigmoid(c)
    o_ref[...] = jnp.dot(s, w_ref[...], preferred_element_type=F32,
                         precision=lax.Precision.HIGHEST) + b_ref[...]


def _modulation(c_all, w_ada, b_ada):
    n = c_all.shape[0]
    tn = 1024
    return pl.pallas_call(
        _mod_kernel,
        grid=(6 * D_MODEL // tn,),
        in_specs=[pl.BlockSpec((n, D_MODEL), lambda j: (0, 0)),
                  pl.BlockSpec((D_MODEL, tn), lambda j: (0, j)),
                  pl.BlockSpec((1, tn), lambda j: (0, j))],
        out_specs=pl.BlockSpec((n, tn), lambda j: (0, j)),
        out_shape=jax.ShapeDtypeStruct((n, 6 * D_MODEL), F32),
        name="adaln_mod",
        compiler_params=pltpu.CompilerParams(dimension_semantics=("arbitrary",),
                                             vmem_limit_bytes=VMEM_LIMIT),
    )(c_all, w_ada, b_ada.reshape(1, -1))


_C_DQ = 0
_C_DK = _C_DQ + DIFF_W
_C_DV = _C_DK + DIFF_W
_C_SQ = _C_DV + DIFF_W
_C_SK = _C_SQ + DSA_W
_C_SV = _C_SK + DSA_KV_W
_C_END = _C_SV + DSA_KV_W
IDX_W = 384


def _split3(a):
    a1 = a.astype(BF16)
    r = a - a1.astype(F32)
    a2 = r.astype(BF16)
    return a1, a2, (r - a2.astype(F32)).astype(BF16)


def _value_rows(v):
    row = lax.broadcasted_iota(I32, (V_PAD, KEY_BLOCK), 0)
    extra = jnp.where(row == 0, 1.0, 0.0)
    return jnp.concatenate([v.T, extra], axis=0).astype(BF16)


def _proj_kernel(x_ref, sc_ref, sh_ref, g_ref, wm_ref, wq1_ref, wq2_ref, wq3_ref, wn12_ref, wn31_ref,
                 dk32, dv32, sk32, sv32, ik32, iw32, dq16, dk16, sq16, sk16, iq16, ik16, dvx, svx,
                 *, transpose_values):
    bt, tt, _ = x_ref.shape
    x = x_ref[...]
    h = (_rms(x) * g_ref[...]) * (1.0 + sc_ref[...]) + sh_ref[...]
    h = h.reshape(bt * tt, D_MODEL)
    hb, hl, h3 = _split3(h)
    z = _dot(hb, wm_ref[...])
    wq1 = wq1_ref[...]
    wq2 = wq2_ref[...]
    zq = (_dot(hb, wq1) + _dot(hb, wq2) + _dot(hl, wq1)
          + _dot(hb, wq3_ref[...]) + _dot(h3, wq1) + _dot(hl, wq2))
    wn12 = wn12_ref[...]
    wn31 = wn31_ref[...]
    p1, p2 = _dot(hb, wn12), _dot(hl, wn12)
    p3, p4 = _dot(hb, wn31), _dot(h3, wn31)
    zn = (p1[:, :LANE] + p1[:, LANE:] + p2[:, :LANE] + p3[:, :LANE] + p4[:, LANE:] + p2[:, LANE:])
    zi = jnp.concatenate([zq, zn], axis=1)

    def put(ref, v):
        ref[...] = v.reshape(bt, tt, v.shape[-1]).astype(ref.dtype)

    zdk = z[:, _C_DK:_C_DV]
    zdv = z[:, _C_DV:_C_SQ]
    zsk = z[:, _C_SK:_C_SV]
    zsv = z[:, _C_SV:_C_END]
    zik = zi[:, 256:256 + IDX_DH]
    for n in range(N_DIFF):
        dk32[:, :, n, :] = zdk[:, n * LANE:(n + 1) * LANE].reshape(bt, tt, LANE)
        dv32[:, :, n, :] = zdv[:, n * LANE:(n + 1) * LANE].reshape(bt, tt, LANE)
    put(sk32, zsk)
    put(sv32, zsv)
    put(ik32, zik)
    put(iw32, zi[:, 256:IDX_W])
    low = lax.broadcasted_iota(I32, (bt * tt, LANE), 1) < DIFF_DH
    halves = lambda v: (jnp.where(low, v, 0.0), jnp.where(low, 0.0, v))
    dq_pairs = [halves(z[:, _C_DQ + n * LANE:_C_DQ + (n + 1) * LANE]) for n in range(N_DIFF)]
    sq_pairs = [halves(z[:, _C_SQ + j * LANE:_C_SQ + (j + 1) * LANE]) for j in range(DSA_GROUP)]
    put(dq16, jnp.concatenate([part for pair in dq_pairs for part in pair], axis=1))
    put(dk16, zdk)
    put(sq16, jnp.concatenate([pair[0] for pair in sq_pairs] + [pair[1] for pair in sq_pairs], axis=1))
    put(sk16, zsk)

    q1, q2, q3 = _split3(zi[:, 0:N_IDX * IDX_DH])
    k1, k2, k3 = _split3(zik)
    head = lambda a, n: a[:, n * IDX_DH:(n + 1) * IDX_DH]
    put(iq16, jnp.concatenate([head(p, n) for n in range(N_IDX)
                               for p in (q1, q1, q2, q1, q3, q2)], axis=1))
    put(ik16, jnp.concatenate([k1, k2, k1, k3, k1, k2], axis=1))

    if transpose_values:
        for kb in range(tt // KEY_BLOCK):
            rows = slice(kb * KEY_BLOCK, (kb + 1) * KEY_BLOCK)
            for n in range(N_DIFF):
                dvx[n, kb] = _value_rows(zdv[rows, n * LANE:(n + 1) * LANE])
            for g in range(N_DSA_KV):
                svx[g, kb] = _value_rows(zsv[rows, g * DSA_DH:(g + 1) * DSA_DH])
    else:
        put(dvx, zdv)
        put(svx, zsv)


def _projection(x, sc, sh, g, w_proj, bt, tt, transpose_values):
    B, T, _ = x.shape
    grid = (B // bt, T // tt)
    row = lambda w: pl.BlockSpec((bt, tt, w), lambda b, t: (b, t, 0))
    per_b = pl.BlockSpec((bt, 1, D_MODEL), lambda b, t: (b, 0, 0))
    full = lambda a: pl.BlockSpec(a.shape, lambda b, t: (0,) * a.ndim,
                                  pipeline_mode=pl.Buffered(1))
    outs = [(DIFF_W, F32), (DIFF_W, F32), (DSA_KV_W, F32), (DSA_KV_W, F32), (IDX_DH, F32),
            (LANE, F32), (2 * DIFF_W, BF16), (DIFF_W, BF16), (N_DSA * LANE, BF16),
            (DSA_KV_W, BF16), (N_IDX * 384, BF16), (384, BF16)]
    out_specs = [row(w) for w, _ in outs]
    out_shape = [jax.ShapeDtypeStruct((B, T, w), dt) for w, dt in outs]
    for j in range(2):
        out_specs[j] = pl.BlockSpec((bt, tt, N_DIFF, 2 * DIFF_DH), lambda b, t: (b, t, 0, 0))
        out_shape[j] = jax.ShapeDtypeStruct((B, T, N_DIFF, 2 * DIFF_DH), F32)
    if transpose_values:
        assert bt == 1 and tt % KEY_BLOCK == 0
        nb = tt // KEY_BLOCK
        out_specs += [pl.BlockSpec((None, N_DIFF, nb, V_ROWS, KEY_BLOCK), lambda b, t: (b, 0, t, 0, 0)),
                      pl.BlockSpec((None, N_DSA_KV, nb, DSA_DH + V_PAD, KEY_BLOCK),
                                   lambda b, t: (b, 0, t, 0, 0))]
        out_shape += [jax.ShapeDtypeStruct((B, N_DIFF, T // KEY_BLOCK, V_ROWS, KEY_BLOCK), BF16),
                      jax.ShapeDtypeStruct((B, N_DSA_KV, T // KEY_BLOCK, DSA_DH + V_PAD, KEY_BLOCK), BF16)]
    else:
        out_specs += [row(DIFF_W), row(DSA_KV_W)]
        out_shape += [jax.ShapeDtypeStruct((B, T, DIFF_W), BF16),
                      jax.ShapeDtypeStruct((B, T, DSA_KV_W), BF16)]
    return pl.pallas_call(
        functools.partial(_proj_kernel, transpose_values=transpose_values),
        grid=grid,
        in_specs=[row(D_MODEL), per_b, per_b, full(g)] + [full(w) for w in w_proj],
        out_specs=out_specs,
        out_shape=out_shape,
        name="in_proj",
        compiler_params=pltpu.CompilerParams(dimension_semantics=("arbitrary", "arbitrary"),
                                             vmem_limit_bytes=VMEM_LIMIT),
    )(x, sc, sh, g, *w_proj)


def _fold_rows(x, rows, op):
    acc = x[0:rows]
    for r in range(1, x.shape[0] // rows):
        acc = op(acc, x[r * rows:(r + 1) * rows])
    return acc


def _init_bias_tiles(tab_ref, tb_ref):
    k = lax.broadcasted_iota(I32, (LANE, LANE), 0)
    q = lax.broadcasted_iota(I32, (LANE, LANE), 1)
    buckets = []
    for t in range(N_BIAS_TILES - 1):
        rel = k - q - LANE * t
        n = jnp.abs(rel)
        large = jnp.full_like(n, 8)
        for step in BUCKET_STEPS:
            large = large + jnp.where(n >= step, 1, 0)
        buckets.append(jnp.where(rel > 0, NUM_BUCKETS // 2, 0) + jnp.where(n < 8, n, large))

    def per_head(h, carry):
        for t, bucket in enumerate(buckets):
            val = jnp.zeros((LANE, LANE), F32)
            for b in range(NUM_BUCKETS):
                val = jnp.where(bucket == b, tab_ref[b, h] * LOG2E, val)
            tb_ref[h, t] = val
        tb_ref[h, N_BIAS_TILES - 1] = jnp.full((LANE, LANE), tab_ref[FAR_BUCKET, h] * LOG2E, F32)
        return carry

    lax.fori_loop(0, N_HEADS, per_head, 0)


def _attn_kernel(tab_ref, lam_ref, gsub_ref, dq_ref, sq_ref, iq_ref, iw_ref,
                 dk_ref, dv_ref, sk_ref, sv_ref, ik_ref, o_ref,
                 tb_ref, ltri_ref, key_ref, byte_ref, s_ref, acc_ref, ot_ref, *, tq, q_offset, q_tiles):
    tk = KEY_BLOCK
    b = pl.program_id(0)
    i = pl.program_id(1)

    @pl.when((b == 0) & (i == 0))
    def _():
        _init_bias_tiles(tab_ref, tb_ref)
        ltri_ref[...] = jnp.where(lax.broadcasted_iota(I32, (tk, tk), 1)
                                  < lax.broadcasted_iota(I32, (tk, tk), 0), 1.0, 0.0).astype(BF16)

    qpos0 = q_offset if q_tiles == 1 else q_offset + i * tq
    lim_last = CHUNK * ((qpos0 + tq - 1) // CHUNK + 1)
    nblk = (lim_last + tk - 1) // tk
    qq = lax.broadcasted_iota(I32, (1, tq), 1)
    lim_q = CHUNK * ((qpos0 + qq) // CHUNK + 1)
    kk = lax.broadcasted_iota(I32, (tk, tq), 0)

    def key_rows(jb):
        start = jb * tk
        return pl.ds(start if isinstance(start, int) else pl.multiple_of(start, tk), tk)

    def over_blocks(body, init, unroll=4):
        def run(first, n, carry):
            for u in range(n):
                carry = body(first + u, carry)
            return carry
        if isinstance(nblk, int):
            return run(0, nblk, init)
        carry = lax.fori_loop(0, nblk // unroll, lambda jp, c: run(unroll * jp, unroll, c), init)
        n = unroll // 2
        while n >= 1:
            first = (nblk // (2 * n)) * (2 * n)
            carry = lax.fori_loop(0, (nblk // n) % 2, lambda _, c, first=first, n=n: run(first, n, c),
                                  carry)
            n //= 2
        return carry

    def bias_tile(head, jb):
        rows = []
        for a in range(tk // LANE):
            parts = []
            for c in range(tq // LANE):
                t = (qpos0 + c * LANE - (jb * tk + a * LANE)) // LANE
                t = min(max(t, 0), N_BIAS_TILES - 1) if isinstance(t, int) else jnp.clip(t, 0, N_BIAS_TILES - 1)
                parts.append(tb_ref[head, t])
            rows.append(jnp.concatenate(parts, axis=1))
        return jnp.concatenate(rows, axis=0)

    iw = iw_ref[...]

    def score_block(jb, carry):
        kc = ik_ref[key_rows(jb), :]
        sc = jnp.zeros((tk, tq), F32)
        for h in range(N_IDX):
            d = _dot_t(kc, iq_ref[:, h * 384:(h + 1) * 384])
            sc = sc + jnp.maximum(d, 0.0) * iw[h:h + 1]
        sc = jnp.where(kk + jb * tk < lim_q, sc, NEG)
        bits = pltpu.bitcast(sc, I32)
        key = bits ^ ((bits >> 31) & 0x7FFFFFFF)
        key_ref[jb] = key
        for lvl in range(KEY_BYTES):
            byte = ((key >> (8 * lvl)) & 0xFF) if lvl < KEY_BYTES - 1 else (key >> 24) + 128
            byte_ref[lvl, jb] = byte.astype(F32).astype(BF16)
        return carry

    over_blocks(score_block, 0, unroll=4)

    one16 = jnp.ones((), BF16)
    zero16 = jnp.zeros((), BF16)

    def count_digits(lvl, cand):
        def body(jb, acc):
            hit = jnp.where(byte_ref[lvl, jb] >= cand, one16, zero16)
            return acc + _fold_rows(hit, 2 * SUBLANE, jnp.add).astype(F32)
        acc = over_blocks(body, jnp.zeros((2 * SUBLANE, tq), F32))
        return jnp.sum(acc, axis=0, keepdims=True)

    want = jnp.full((1, tq), float(TOPK), F32)
    thr = jnp.zeros((1, tq), I32)
    for lvl in reversed(range(KEY_BYTES)):
        def digit_bit(it, state, lvl=lvl, want=want):
            digit, above = state
            cand = digit + jnp.left_shift(jnp.int32(1), 7 - it).astype(F32)
            cnt = count_digits(lvl, cand.astype(BF16))
            ok = cnt >= want
            return jnp.where(ok, cand, digit), jnp.where(ok, above, cnt)

        digit, above = lax.fori_loop(0, 8, digit_bit,
                                     (jnp.zeros((1, tq), F32), jnp.zeros((1, tq), F32)))
        want = want - above
        digit_i = digit.astype(I32)
        thr = thr | ((digit_i - 128) << 24 if lvl == KEY_BYTES - 1 else digit_i << (8 * lvl))
        if lvl > 0:
            def narrow(jb, carry, lvl=lvl, digit16=digit.astype(BF16)):
                byte_ref[lvl - 1, jb] = jnp.where(byte_ref[lvl, jb] == digit16,
                                                  byte_ref[lvl - 1, jb], -one16)
                return carry
            over_blocks(narrow, 0)

    def mask_block(jb, before):
        kb = key_ref[jb]
        eq = jnp.where(kb == thr, 1.0, 0.0)
        rank = _dot(ltri_ref[...], eq.astype(BF16)) + before
        keep = jnp.where(rank < want, eq, 0.0)
        sel = jnp.where(kb > thr, 0.0, (keep - 1.0) * (-NEG))
        sel = jnp.where(kk + jb * tk < lim_q, sel, NEG)
        key_ref[jb] = pltpu.bitcast(sel, I32)
        return before + jnp.sum(_fold_rows(eq, SUBLANE, jnp.add), axis=0, keepdims=True)

    over_blocks(mask_block, jnp.zeros((1, tq), F32))

    lam = lam_ref[...]
    lam_full = (jnp.exp(jnp.sum(lam[0:1] * lam[1:2], axis=1, keepdims=True))
                - jnp.exp(jnp.sum(lam[2:3] * lam[3:4], axis=1, keepdims=True)) + LAM_INIT)

    def dsa_group(pair):
        heads = [N_STAGE * pair + n for n in range(N_STAGE)]

        def scores(jb):
            kb = sk_ref[key_rows(jb), :]
            sel = pltpu.bitcast(key_ref[jb], F32)
            return [_dot_t(kb, sq_ref[:, h * LANE:(h + 1) * LANE]) + bias_tile(N_DIFF + h, jb) + sel
                    for h in heads]

        def finish():
            for n, h in enumerate(heads):
                a = acc_ref[n]
                out = a[0:DSA_DH] / a[DSA_DH:DSA_DH + 1]
                ot_ref[DIFF_W + h * DSA_DH:DIFF_W + (h + 1) * DSA_DH, :] = out

        return scores, (lambda n, jb: sv_ref[heads[n] // DSA_GROUP, jb]), finish

    def diff_group(h):
        def scores(jb):
            kb = dk_ref[key_rows(jb), h * LANE:(h + 1) * LANE]
            bias = bias_tile(h, jb) + jnp.where(kk + jb * tk < lim_q, 0.0, NEG)
            return [_dot_t(kb, dq_ref[:, (2 * h + part) * LANE:(2 * h + part + 1) * LANE]) + bias
                    for part in range(2)]

        def finish():
            a1 = acc_ref[0]
            a2 = acc_ref[1]
            od = (a1[0:LANE] / a1[LANE:LANE + 1]
                  - lam_full * (a2[0:LANE] / a2[LANE:LANE + 1]))
            od = od * lax.rsqrt(jnp.mean(od * od, axis=0, keepdims=True) + EPS)
            ot_ref[h * LANE:(h + 1) * LANE, :] = (od * gsub_ref[...]) * (1.0 - LAM_INIT)

        return scores, (lambda n, jb: dv_ref[h, jb]), finish

    groups = ([dsa_group(p) for p in range(N_DSA // N_STAGE)]
              + [diff_group(h) for h in range(N_DIFF)])
    m_prev = None
    for k in range(len(groups) + 1):
        produce = groups[k] if k < len(groups) else None
        consume = groups[k - 1] if k > 0 else None

        def step(jb, mx, k=k, produce=produce, consume=consume, m_prev=m_prev):
            if consume is not None:
                for n in range(N_STAGE):
                    p = jnp.exp2(s_ref[(k - 1) % 2, n, jb] - m_prev[n]).astype(BF16)
                    v = consume[1](n, jb)
                    acc_ref[n, 0:v.shape[0]] += _dot(v, p)
            if produce is None:
                return mx
            out = []
            for n, s in enumerate(produce[0](jb)):
                s_ref[k % 2, n, jb] = s
                out.append(jnp.maximum(mx[n], _fold_rows(s, SUBLANE, jnp.maximum)))
            return tuple(out)

        if consume is not None:
            for n in range(N_STAGE):
                acc_ref[n] = jnp.zeros((V_ROWS, tq), F32)
        mx = over_blocks(step, tuple(jnp.full((SUBLANE, tq), -3e38, F32) for _ in range(N_STAGE)),
                         unroll=8)
        if consume is not None:
            consume[2]()
        m_prev = [jnp.max(x, axis=0, keepdims=True) for x in mx]

    o_ref[...] = ot_ref[...].T.astype(o_ref.dtype)


def _fill_key_operands(new, past, out):
    ndk, ndv, nsk, nsv, nik = new
    pk, pv, psk, psv, pik = past
    dk_s, dv_s, sk_s, sv_s, ik_s = out
    P, T, Lp = pik.shape[0], ndk.shape[0], dk_s.shape[0]

    def rows(dst, fresh):
        dst[P:P + T] = fresh
        dst[P + T:Lp] = jnp.zeros((Lp - P - T, dst.shape[1]), BF16)

    for n in range(N_DIFF):
        dk_s[0:P, n * LANE:(n + 1) * LANE] = pk[n][...].astype(BF16)
    rows(dk_s, ndk[...])
    sk_s[0:P] = psk[...].astype(BF16)
    rows(sk_s, nsk[...])
    k1, k2, k3 = _split3(pik[...])
    ik_s[0:P] = jnp.concatenate([k1, k2, k1, k3, k1, k2], axis=1)
    rows(ik_s, nik[...])

    for kb in range(P // KEY_BLOCK):
        r = slice(kb * KEY_BLOCK, (kb + 1) * KEY_BLOCK)
        for n in range(N_DIFF):
            dv_s[n, kb] = _value_rows(pv[n][r, :])
        for g in range(N_DSA_KV):
            sv_s[g, kb] = _value_rows(psv[r, g * DSA_DH:(g + 1) * DSA_DH])
    tail = lambda a: jnp.concatenate(
        [a[...].astype(F32), jnp.zeros((KEY_BLOCK - T, a.shape[1]), F32)], axis=0)
    tail_dv, tail_sv = tail(ndv), tail(nsv)
    for n in range(N_DIFF):
        dv_s[n, P // KEY_BLOCK] = _value_rows(tail_dv[:, n * LANE:(n + 1) * LANE])
    for g in range(N_DSA_KV):
        sv_s[g, P // KEY_BLOCK] = _value_rows(tail_sv[:, g * DSA_DH:(g + 1) * DSA_DH])


def _attn_cached_kernel(tab_ref, lam_ref, gsub_ref, dq_ref, sq_ref, iq_ref, iw_ref,
                        ndk, ndv, nsk, nsv, nik, pk_hbm, pv_hbm, psk, psv, pik, o_ref,
                        tb_ref, ltri_ref, key_ref, byte_ref, s_ref, acc_ref, ot_ref,
                        dk_s, dv_s, sk_s, sv_s, ik_s, kv_buf, kv_sem, *, tq, q_offset, q_tiles):
    b = pl.program_id(0)
    slot = b % 2

    def cache_copies(batch, into):
        return [pltpu.make_async_copy(src.at[batch, :, n, :], kv_buf.at[into, t, n],
                                      kv_sem.at[into, t, n])
                for t, src in enumerate((pk_hbm, pv_hbm)) for n in range(N_DIFF)]

    @pl.when(b == 0)
    def _():
        for copy in cache_copies(0, 0):
            copy.start()

    for copy in cache_copies(b, slot):
        copy.wait()

    @pl.when(b + 1 < pl.num_programs(0))
    def _():
        for copy in cache_copies(b + 1, 1 - slot):
            copy.start()

    pk = [kv_buf.at[slot, 0, n] for n in range(N_DIFF)]
    pv = [kv_buf.at[slot, 1, n] for n in range(N_DIFF)]
    _fill_key_operands((ndk, ndv, nsk, nsv, nik), (pk, pv, psk, psv, pik),
                       (dk_s, dv_s, sk_s, sv_s, ik_s))
    _attn_kernel(tab_ref, lam_ref, gsub_ref, dq_ref, sq_ref, iq_ref, iw_ref,
                 dk_s, dv_s, sk_s, sv_s, ik_s, o_ref,
                 tb_ref, ltri_ref, key_ref, byte_ref, s_ref, acc_ref, ot_ref,
                 tq=tq, q_offset=q_offset, q_tiles=q_tiles)


def _attention(tab, lam4, gsub_t, dq, sq, iq, iw_t, keys, tq, q_offset, past=None):
    B, T, _ = dq.shape
    Lp = keys[0].shape[1] if past is None else past[0].shape[1] + KEY_BLOCK
    nblk_max = Lp // KEY_BLOCK
    qspec = lambda w: pl.BlockSpec((None, tq, w), lambda b, i: (b, i, 0))
    once = pl.Buffered(1)
    small = lambda a: pl.BlockSpec(a.shape, lambda b, i: (0,) * a.ndim)
    per_batch = lambda a, mode=None: pl.BlockSpec((None,) + a.shape[1:],
                                                  lambda b, i: (b,) + (0,) * (a.ndim - 1),
                                                  pipeline_mode=mode)
    operand_shapes = [((Lp, DIFF_W), BF16), ((N_DIFF, nblk_max, V_ROWS, KEY_BLOCK), BF16),
                      ((Lp, DSA_KV_W), BF16),
                      ((N_DSA_KV, nblk_max, DSA_DH + V_PAD, KEY_BLOCK), BF16), ((Lp, 384), BF16)]
    scratch = [pltpu.VMEM((N_HEADS, N_BIAS_TILES, LANE, LANE), F32),
               pltpu.VMEM((KEY_BLOCK, KEY_BLOCK), BF16),
               pltpu.VMEM((nblk_max, KEY_BLOCK, tq), I32),
               pltpu.VMEM((KEY_BYTES, nblk_max, KEY_BLOCK, tq), BF16),
               pltpu.VMEM((2, N_STAGE, nblk_max, KEY_BLOCK, tq), F32),
               pltpu.VMEM((N_STAGE, V_ROWS, tq), F32),
               pltpu.VMEM((D_MODEL, tq), F32)]
    if past is None:
        kern = _attn_kernel
        key_specs = [per_batch(a, once) for a in keys]
        key_args = list(keys)
    else:
        assert T == tq and past[0].shape[1] % KEY_BLOCK == 0 and keys[0].shape[1] <= KEY_BLOCK
        kern = _attn_cached_kernel
        pk, pv, psk, psv, pik = past
        P = pk.shape[1]
        flat = lambda a: a.reshape(B, P, -1)
        hbm = pl.BlockSpec(memory_space=pl.ANY)
        key_specs = [per_batch(a) for a in keys] + [hbm, hbm] + [
            per_batch(flat(a)) for a in (psk, psv, pik)]
        key_args = list(keys) + [pk, pv, flat(psk), flat(psv), flat(pik)]
        scratch += [pltpu.VMEM(shape, dt) for shape, dt in operand_shapes]
        scratch += [pltpu.VMEM((2, 2, N_DIFF, P, 2 * DIFF_DH), F32),
                    pltpu.SemaphoreType.DMA((2, 2, N_DIFF))]
    return pl.pallas_call(
        functools.partial(kern, tq=tq, q_offset=q_offset, q_tiles=T // tq),
        grid=(B, T // tq),
        in_specs=[pl.BlockSpec(memory_space=pltpu.SMEM), small(lam4), small(gsub_t),
                  qspec(2 * DIFF_W), qspec(N_DSA * LANE), qspec(N_IDX * 384),
                  pl.BlockSpec((None, SUBLANE, tq), lambda b, i: (b, 0, i))] + key_specs,
        out_specs=pl.BlockSpec((None, tq, D_MODEL), lambda b, i: (b, i, 0)),
        out_shape=jax.ShapeDtypeStruct((B, T, D_MODEL), BF16),
        name="attn",
        scratch_shapes=scratch,
        compiler_params=pltpu.CompilerParams(dimension_semantics=("arbitrary", "arbitrary"),
                                             vmem_limit_bytes=VMEM_LIMIT),
    )(tab, lam4, gsub_t, dq, sq, iq, iw_t, *key_args)


FF_CUTS = (0, 1536, D_FF)


def _ffn_kernel(x_ref, o_ref, ga1_ref, sc2_ref, sh2_ref, ga2_ref, gpm_ref, gpf_ref, gpo_ref,
                wo_ref, wgu_ref, wd_ref, y_ref):
    bt, tt, _ = x_ref.shape
    rows = bt * tt
    m = _dot(o_ref[...].reshape(rows, D_MODEL), wo_ref[...]).reshape(bt, tt, D_MODEL)
    x1 = x_ref[...] + ga1_ref[...] * (_rms(m) * gpm_ref[...])
    h2 = (_rms(x1) * gpf_ref[...]) * (1.0 + sc2_ref[...]) + sh2_ref[...]
    h2 = h2.reshape(rows, D_MODEL).astype(BF16)
    f = jnp.zeros((rows, D_MODEL), F32)
    for lo, hi in zip(FF_CUTS[:-1], FF_CUTS[1:]):
        g = _dot(h2, wgu_ref[:, lo:hi])
        u = _dot(h2, wgu_ref[:, D_FF + lo:D_FF + hi])
        a = (g * jax.nn.sigmoid(g) * u).astype(BF16)
        f = f + _dot(a, wd_ref[lo:hi, :])
    f = f.reshape(bt, tt, D_MODEL)
    y_ref[...] = x1 + ga2_ref[...] * (_rms(f) * gpo_ref[...])


def _out_ffn(x, o, ga1, sc2, sh2, ga2, gpm, gpf, gpo, wo, wgu, wd, bt, tt):
    B, T, _ = x.shape
    row = pl.BlockSpec((bt, tt, D_MODEL), lambda b, t: (b, t, 0))
    per_b = pl.BlockSpec((bt, 1, D_MODEL), lambda b, t: (b, 0, 0))
    full = lambda a: pl.BlockSpec(a.shape, lambda b, t: (0,) * a.ndim,
                                  pipeline_mode=pl.Buffered(1))
    return pl.pallas_call(
        _ffn_kernel,
        grid=(B // bt, T // tt),
        in_specs=[row, row, per_b, per_b, per_b, per_b, full(gpm), full(gpf), full(gpo),
                  full(wo), full(wgu), full(wd)],
        out_specs=row,
        out_shape=jax.ShapeDtypeStruct((B, T, D_MODEL), F32),
        name="out_ffn",
        compiler_params=pltpu.CompilerParams(dimension_semantics=("arbitrary", "arbitrary"),
                                             vmem_limit_bytes=VMEM_LIMIT),
    )(x, o, ga1, sc2, sh2, ga2, gpm, gpf, gpo, wo, wgu, wd)


def _prep_in_weights(w_in):
    cuts = [0, 512, 1024, 1536, 2048, 2176, 2304, 2560, 2624, 2628]
    wdq, wdk, wdv, wsq, wsk, wsv, wiq, wik, wiw = [w_in[:, a:b] for a, b in zip(cuts[:-1], cuts[1:])]
    scale = DIFF_DH ** -0.5 * LOG2E
    sq_head = lambda h: wsq[:, h * DSA_DH:(h + 1) * DSA_DH]
    sq_cols = [sq_head(j + g * DSA_GROUP) for j in range(DSA_GROUP) for g in range(N_DSA_KV)]
    wm = jnp.concatenate([wdq * scale, wdk, wdv] + [c * scale for c in sq_cols] + [wsk, wsv],
                         axis=1).astype(BF16)
    n1, n2, n3 = _split3(jnp.concatenate(
        [wik, wiw, jnp.zeros((D_MODEL, LANE - IDX_DH - N_IDX), F32)], axis=1))
    return (wm,) + _split3(wiq) + (jnp.concatenate([n1, n2], axis=1), jnp.concatenate([n3, n1], axis=1))


def _pad_axis1(a, n):
    return jnp.pad(a, ((0, 0), (0, n - a.shape[1])) + ((0, 0),) * (a.ndim - 2))


def _group(x, mods, past, params, bt, tt, tq):
    (g_pre_mix, g_post_mix, g_pre_ffn, g_post_ffn, w_proj, lam4, g_subln,
     wo, wgu, wd, tab) = params
    B, T, _ = x.shape
    sh1, sc1, ga1, sh2, sc2, ga2 = mods
    (dk32, dv32, sk32, sv32, ik32, iw32, dq16, dk16, sq16, sk16, iq16, ik16, dvx, svx) = _projection(
        x, sc1, sh1, g_pre_mix, w_proj, bt, tt, transpose_values=past is None)
    iw = iw32[..., IDX_DH:IDX_DH + N_IDX] * (1.0 / 16.0)
    if past is None:
        q_offset = 0
    else:
        q_offset = past[0].shape[1]
    tpad = -(-T // tq) * tq
    qpad = lambda a: _pad_axis1(a, tpad)
    iw_t = _pad_axis1(jnp.transpose(qpad(iw), (0, 2, 1)), SUBLANE)
    gsub_t = jnp.broadcast_to(g_subln.reshape(-1, 1), (2 * DIFF_DH, tq))
    o = _attention(tab, lam4, gsub_t, qpad(dq16), qpad(sq16), qpad(iq16), iw_t,
                   (dk16, dvx, sk16, svx, ik16), tq, q_offset, past)[:, :T]
    y = _out_ffn(x, o, ga1, sc2, sh2, ga2, g_post_mix, g_pre_ffn, g_post_ffn, wo, wgu, wd, bt, tt)
    rows = (dk32[None], dv32[None],
            sk32.reshape(1, B, T, N_DSA_KV, DSA_DH), sv32.reshape(1, B, T, N_DSA_KV, DSA_DH),
            ik32.reshape(1, B, T, IDX_DH))
    return y, rows


def kernel(x_prompt, x_sample, cache_diff_k, cache_diff_v, cache_dsa_k, cache_dsa_v, cache_dsa_kidx,
           c_prompt, c_sample, w_ada, b_ada, g_pre_mix, g_post_mix, g_pre_ffn, g_post_ffn, w_in,
           lambda_q1, lambda_k1, lambda_q2, lambda_k2, g_subln, w_out, w_gate_up, w_down,
           rel_bias_table):
    Bp = x_prompt.shape[0]
    c_all = jnp.concatenate([c_prompt, c_sample], axis=0)
    mod = _modulation(c_all, w_ada[0], b_ada[0])
    mods = [mod[:, j * D_MODEL:(j + 1) * D_MODEL][:, None, :] for j in range(6)]
    mods_p = [m[:Bp] for m in mods]
    mods_s = [m[Bp:] for m in mods]

    w_proj = _prep_in_weights(w_in[0])
    lam4 = jnp.concatenate([lambda_q1, lambda_k1, lambda_q2, lambda_k2], axis=0)
    row = lambda g: g[0].reshape(1, -1)
    params = (row(g_pre_mix), row(g_post_mix), row(g_pre_ffn), row(g_post_ffn), w_proj,
              lam4, g_subln[0], w_out[0].astype(BF16), w_gate_up[0].astype(BF16),
              w_down[0].astype(BF16), rel_bias_table)

    yp, rp = _group(x_prompt, mods_p, None, params, bt=1, tt=512, tq=256)
    past = (cache_diff_k[0], cache_diff_v[0], cache_dsa_k[0], cache_dsa_v[0], cache_dsa_kidx[0])
    ys, rs = _group(x_sample, mods_s, past, params, bt=8, tt=64, tq=128)
    return (yp, ys) + rp + rs
```
